```python
import jax
import jax.numpy as jnp
from jax import lax
import numpy as np

D_MODEL = 1024
BATCH = 8
SEQ = 2048
DEPTH = 2
DEC_BATCH = 32
DEC_SEQ = 8
PAST_LEN = 16384
PAGE_SIZE = 128

HEAD_DIM = 64
A_GROUPS = D_MODEL // (2 * HEAD_DIM)
A_WIDTH = A_GROUPS * HEAD_DIM
A_CHUNK = 128
B_HEADS = D_MODEL // (2 * HEAD_DIM)
B_WIDTH = B_HEADS * HEAD_DIM
B_BLOCK = 256
B_TOPK = 3
B_Q_CHUNK = 16
EVEN_IN = 2 * A_WIDTH + 3 * B_WIDTH
C_HEADS = D_MODEL // HEAD_DIM
C_WIDTH = C_HEADS * HEAD_DIM
C_PATTERNS = ((128, 1), (512, 4), (2048, 16))
C_WINDOW = max(w for w, _ in C_PATTERNS)
C_Q_CHUNK = 32
D_FF = ((8 * D_MODEL // 3 + 255) // 256) * 256
N_EXPERTS = 8
TOP_K = 2
N_EVEN = (DEPTH + 1) // 2
N_ODD = DEPTH // 2
RMS_EPS = 1e-6
LN_EPS = 1e-5

kernel_name = 'hybrid_gmlp_moba_dilated_decoder_step'


def rms_norm(x, g):
    xf = x.astype(jnp.float32)
    y = xf * lax.rsqrt(jnp.mean(xf * xf, axis=-1, keepdims=True) + RMS_EPS)
    return (y * g.astype(jnp.float32)).astype(x.dtype)


def layer_norm(x, g, b):
    xf = x.astype(jnp.float32)
    mu = jnp.mean(xf, axis=-1, keepdims=True)
    var = jnp.mean(jnp.square(xf - mu), axis=-1, keepdims=True)
    y = (xf - mu) * lax.rsqrt(var + LN_EPS) * g.astype(jnp.float32) + b.astype(jnp.float32)
    return y.astype(x.dtype)


def swiglu(h, w_gate, w_up, w_down):
    return (jax.nn.silu(h @ w_gate) * (h @ w_up)) @ w_down


def moe_swiglu(h, router, w_gate, w_up, w_down):
    logits = (h @ router).astype(jnp.float32)
    top_v, top_i = lax.top_k(logits, TOP_K)
    gates = jax.nn.softmax(top_v, axis=-1)
    combine = jnp.einsum('nk,nke->ne', gates, jax.nn.one_hot(top_i, N_EXPERTS, dtype=jnp.float32))
    out = jnp.zeros(h.shape, jnp.float32)
    for e in range(N_EXPERTS):
        out = out + combine[:, e:e + 1] * swiglu(h, w_gate[e], w_up[e], w_down[e]).astype(jnp.float32)
    return out.astype(h.dtype)


def chunk_spatial_gate(u, v, w_s, b_s):
    bsz, t, _ = v.shape
    n_chunks = -(-t // A_CHUNK)
    vc = jnp.pad(v, ((0, 0), (0, n_chunks * A_CHUNK - t), (0, 0)))
    vc = vc.reshape(bsz, n_chunks, A_CHUNK, A_GROUPS, HEAD_DIM)
    causal = jnp.tril(jnp.ones((A_CHUNK, A_CHUNK), dtype=bool))
    w = jnp.where(causal[None], w_s, jnp.zeros_like(w_s))
    mixed = jnp.einsum('gts,bcsgd->bctgd', w, vc) + jnp.transpose(b_s)[None, None, :, :, None]
    mixed = mixed.reshape(bsz, n_chunks * A_CHUNK, A_WIDTH)[:, :t]
    return u * mixed


def to_blocks(x):
    bsz, t = x.shape[:2]
    nb = -(-t // B_BLOCK)
    x = jnp.pad(x, ((0, 0), (0, nb * B_BLOCK - t), (0, 0), (0, 0)))
    return x.reshape(bsz, nb, B_BLOCK, B_HEADS, HEAD_DIM)


def moba_attend(q, q_pos, kb, vb, block_mean):
    nb = kb.shape[1]
    own = q_pos // B_BLOCK
    gate = jnp.einsum('bthd,bnhd->bthn', q.astype(jnp.float32), block_mean)
    fully_past = jnp.arange(nb)[None, :] < own[:, None]
    gate = jnp.where(fully_past[None, :, None, :], gate, -jnp.inf)
    top_s, top_i = lax.top_k(gate, min(B_TOPK, nb))
    lead = top_i.shape[:3] + (1,)
    sel = jnp.concatenate([top_i, jnp.broadcast_to(own[None, :, None, None], lead).astype(top_i.dtype)], axis=-1)
    sel_ok = jnp.concatenate([jnp.isfinite(top_s), jnp.ones(lead, dtype=bool)], axis=-1)
    b_ix = jnp.arange(q.shape[0])[:, None, None, None]
    h_ix = jnp.arange(B_HEADS)[None, None, :, None]
    k_g = kb[b_ix, sel, :, h_ix]
    v_g = vb[b_ix, sel, :, h_ix]
    s = jnp.einsum('bthd,bthnkd->bthnk', q, k_g).astype(jnp.float32) * (HEAD_DIM ** -0.5)
    key_pos = sel[..., None] * B_BLOCK + jnp.arange(B_BLOCK)
    ok = sel_ok[..., None] & (key_pos <= q_pos[None, :, None, None, None])
    s = jnp.where(ok, s, -jnp.inf)
    p = jax.nn.softmax(s.reshape(s.shape[:3] + (-1,)), axis=-1).reshape(s.shape)
    return jnp.einsum('bthnk,bthnkd->bthd', p.astype(vb.dtype), v_g)


def moba_prompt(q, k, v):
    bsz, t = q.shape[:2]
    kb = to_blocks(k)
    vb = to_blocks(v)
    block_mean = jnp.mean(kb.astype(jnp.float32), axis=2)
    n_q = t // B_Q_CHUNK
    qs = q.reshape(bsz, n_q, B_Q_CHUNK, B_HEADS, HEAD_DIM).swapaxes(0, 1)
    pos = jnp.arange(t, dtype=jnp.int32).reshape(n_q, B_Q_CHUNK)
    out = lax.map(lambda a: moba_attend(a[0], a[1], kb, vb, block_mean), (qs, pos))
    return out.swapaxes(0, 1).reshape(bsz, t, B_WIDTH)


def moba_sample(q, k, v, cache_k, cache_v, page_table, layer):
    bsz, t = q.shape[:2]
    past = page_table.shape[1] * cache_k.shape[2]
    k_past = cache_k[layer, page_table].reshape(bsz, past, B_HEADS, HEAD_DIM)
    v_past = cache_v[layer, page_table].reshape(bsz, past, B_HEADS, HEAD_DIM)
    kb = to_blocks(jnp.concatenate([k_past, k], axis=1))
    vb = to_blocks(jnp.concatenate([v_past, v], axis=1))
    block_mean = jnp.mean(kb.astype(jnp.float32), axis=2)
    pos = past + jnp.arange(t, dtype=jnp.int32)
    return moba_attend(q, pos, kb, vb, block_mean).reshape(bsz, t, B_WIDTH)


def dilated_attend(q, q_idx, k_ext, v_ext):
    outs, lses = [], []
    for window, dil in C_PATTERNS:
        offs = dil * jnp.arange(window // dil + 1)
        idx = q_idx[:, None] - offs[None, :]
        ok = idx >= 0
        idx = jnp.maximum(idx, 0)
        k_g = k_ext[:, idx]
        v_g = v_ext[:, idx]
        s = jnp.einsum('bthd,btnhd->bthn', q, k_g).astype(jnp.float32) * (HEAD_DIM ** -0.5)
        s = jnp.where(ok[None, :, None, :], s, -jnp.inf)
        lse = jax.nn.logsumexp(s, axis=-1, keepdims=True)
        p = jnp.exp(s - lse)
        outs.append(jnp.einsum('bthn,btnhd->bthd', p.astype(v_ext.dtype), v_g))
        lses.append(lse)
    mix = jax.nn.softmax(jnp.concatenate(lses, axis=-1), axis=-1)
    out = jnp.einsum('bthr,bthrd->bthd', mix, jnp.stack(outs, axis=-2).astype(jnp.float32))
    return out.astype(q.dtype)


def dilated_prompt(q, k, v):
    bsz, t = q.shape[:2]
    n_q = t // C_Q_CHUNK
    qs = q.reshape(bsz, n_q, C_Q_CHUNK, C_HEADS, HEAD_DIM).swapaxes(0, 1)
    idx = jnp.arange(t, dtype=jnp.int32).reshape(n_q, C_Q_CHUNK)
    out = lax.map(lambda a: dilated_attend(a[0], a[1], k, v), (qs, idx))
    return out.swapaxes(0, 1).reshape(bsz, t, C_WIDTH)


def dilated_sample(q, k, v, cache_k, cache_v, layer):
    bsz, t = q.shape[:2]
    w = cache_k.shape[2]
    k_ext = jnp.concatenate([cache_k[layer], k], axis=1)
    v_ext = jnp.concatenate([cache_v[layer], v], axis=1)
    idx = w + jnp.arange(t, dtype=jnp.int32)
    return dilated_attend(q, idx, k_ext, v_ext).reshape(bsz, t, C_WIDTH)


def run_group(x, b_attend, c_attend, norm_mix, norm_ffn, norm_final, w_in_even, w_out_even, a_ln_g, a_ln_b,
              a_ws, a_bs, ffn_gate, ffn_up, ffn_down, w_in_odd, w_out_odd, router, moe_gate, moe_up, moe_down):
    bsz, t, _ = x.shape
    a_v, b_k, b_v, c_k, c_v = [], [], [], [], []
    a_keep = ((t - 1) // A_CHUNK) * A_CHUNK
    c_keep = min(C_WINDOW, t)
    for layer in range(DEPTH):
        i = layer // 2
        h = rms_norm(x, norm_mix[layer])
        if layer % 2 == 0:
            splits = [A_WIDTH, 2 * A_WIDTH, 2 * A_WIDTH + B_WIDTH, 2 * A_WIDTH + 2 * B_WIDTH]
            u, v, q, k, vv = jnp.split(h @ w_in_even[i], splits, axis=-1)
            u = jax.nn.gelu(u, approximate=False)
            v = layer_norm(jax.nn.gelu(v, approximate=False), a_ln_g[i], a_ln_b[i])
            a_out = chunk_spatial_gate(u, v, a_ws[i], a_bs[i])
            heads = (bsz, t, B_HEADS, HEAD_DIM)
            k = k.reshape(heads)
            vv = vv.reshape(heads)
            b_out = b_attend(i, q.reshape(heads), k, vv)
            x = x + jnp.concatenate([a_out, b_out], axis=-1) @ w_out_even[i]
            x = x + swiglu(rms_norm(x, norm_ffn[layer]), ffn_gate[i], ffn_up[i], ffn_down[i])
            a_v.append(v[:, a_keep:])
            b_k.append(k)
            b_v.append(vv)
        else:
            q, k, v = jnp.split(h @ w_in_odd[i], 3, axis=-1)
            heads = (bsz, t, C_HEADS, HEAD_DIM)
            k = k.reshape(heads)
            v = v.reshape(heads)
            c_out = c_attend(i, q.reshape(heads), k, v)
            x = x + c_out @ w_out_odd[i]
            hf = rms_norm(x, norm_ffn[layer]).reshape(bsz * t, D_MODEL)
            x = x + moe_swiglu(hf, router[i], moe_gate[i], moe_up[i], moe_down[i]).reshape(bsz, t, D_MODEL)
            c_k.append(k[:, t - c_keep:])
            c_v.append(v[:, t - c_keep:])
    y = rms_norm(x, norm_final)
    return (y, jnp.stack(a_v), jnp.stack(b_k), jnp.stack(b_v), jnp.stack(c_k), jnp.stack(c_v))


def setup_inputs(seed: int = 0) -> dict:
    key = jax.random.key(seed)
    keys = jax.random.split(key, 26)
    n_pages = PAST_LEN // PAGE_SIZE
    n_used = DEC_BATCH * n_pages
    n_phys = (5 * n_used + 3) // 4
    c_rows = min(C_WINDOW, PAST_LEN)

    def nrm(i, shape, scale=1.0):
        return jax.random.normal(keys[i], shape, jnp.float32) * scale

    page_table = jax.random.permutation(keys[6], n_phys)[:n_used].reshape(DEC_BATCH, n_pages).astype(jnp.int32)
    return {
        'x_prompt': nrm(0, (BATCH, SEQ, D_MODEL)),
        'x_sample': nrm(1, (DEC_BATCH, DEC_SEQ, D_MODEL)),
        'cache_b_k': nrm(2, (N_EVEN, n_phys, PAGE_SIZE, B_HEADS, HEAD_DIM)),
        'cache_b_v': nrm(3, (N_EVEN, n_phys, PAGE_SIZE, B_HEADS, HEAD_DIM)),
        'cache_c_k': nrm(4, (N_ODD, DEC_BATCH, c_rows, C_HEADS, HEAD_DIM)),
        'cache_c_v': nrm(5, (N_ODD, DEC_BATCH, c_rows, C_HEADS, HEAD_DIM)),
        'page_table': page_table,
        'norm_mix': 1.0 + nrm(7, (DEPTH, D_MODEL), 0.1),
        'norm_ffn': 1.0 + nrm(8, (DEPTH, D_MODEL), 0.1),
        'norm_final': 1.0 + nrm(9, (D_MODEL,), 0.1),
        'w_in_even': nrm(10, (N_EVEN, D_MODEL, EVEN_IN), D_MODEL ** -0.5),
        'w_out_even': nrm(11, (N_EVEN, A_WIDTH + B_WIDTH, D_MODEL), (A_WIDTH + B_WIDTH) ** -0.5),
        'a_ln_g': 1.0 + nrm(12, (N_EVEN, A_WIDTH), 0.1),
        'a_ln_b': nrm(13, (N_EVEN, A_WIDTH), 0.02),
        'a_ws': nrm(14, (N_EVEN, A_GROUPS, A_CHUNK, A_CHUNK), A_CHUNK ** -0.5),
        'a_bs': 1.0 + nrm(15, (N_EVEN, A_GROUPS, A_CHUNK), 0.1),
        'ffn_gate': nrm(16, (N_EVEN, D_MODEL, D_FF), D_MODEL ** -0.5),
        'ffn_up': nrm(17, (N_EVEN, D_MODEL, D_FF), D_MODEL ** -0.5),
        'ffn_down': nrm(18, (N_EVEN, D_FF, D_MODEL), D_FF ** -0.5),
        'w_in_odd': nrm(19, (N_ODD, D_MODEL, 3 * C_WIDTH), D_MODEL ** -0.5),
        'w_out_odd': nrm(20, (N_ODD, C_WIDTH, D_MODEL), C_WIDTH ** -0.5),
        'router': nrm(21, (N_ODD, D_MODEL, N_EXPERTS), D_MODEL ** -0.5),
        'moe_gate': nrm(22, (N_ODD, N_EXPERTS, D_MODEL, D_FF), D_MODEL ** -0.5),
        'moe_up': nrm(23, (N_ODD, N_EXPERTS, D_MODEL, D_FF), D_MODEL ** -0.5),
        'moe_down': nrm(24, (N_ODD, N_EXPERTS, D_FF, D_MODEL), D_FF ** -0.5),
    }


def reference(x_prompt, x_sample, cache_b_k, cache_b_v, cache_c_k, cache_c_v, page_table, norm_mix, norm_ffn,
              norm_final, w_in_even, w_out_even, a_ln_g, a_ln_b, a_ws, a_bs, ffn_gate, ffn_up, ffn_down,
              w_in_odd, w_out_odd, router, moe_gate, moe_up, moe_down):
    def b_prompt(i, q, k, v):
        return moba_prompt(q, k, v)

    def c_prompt(i, q, k, v):
        return dilated_prompt(q, k, v)

    def b_sample(i, q, k, v):
        return moba_sample(q, k, v, cache_b_k, cache_b_v, page_table, i)

    def c_sample(i, q, k, v):
        return dilated_sample(q, k, v, cache_c_k, cache_c_v, i)

    y_prompt, a_v_prompt, b_k_prompt, b_v_prompt, c_k_prompt, c_v_prompt = run_group(
        x_prompt, b_prompt, c_prompt, norm_mix, norm_ffn, norm_final, w_in_even, w_out_even, a_ln_g, a_ln_b,
        a_ws, a_bs, ffn_gate, ffn_up, ffn_down, w_in_odd, w_out_odd, router, moe_gate, moe_up, moe_down)
    y_sample, a_v_sample, b_k_sample, b_v_sample, c_k_sample, c_v_sample = run_group(
        x_sample, b_sample, c_sample, norm_mix, norm_ffn, norm_final, w_in_even, w_out_even, a_ln_g, a_ln_b,
        a_ws, a_bs, ffn_gate, ffn_up, ffn_down, w_in_odd, w_out_odd, router, moe_gate, moe_up, moe_down)
    return (y_prompt, y_sample, a_v_prompt, a_v_sample, b_k_prompt, b_v_prompt, b_k_sample, b_v_sample,
            c_k_prompt, c_v_prompt, c_k_sample, c_v_sample)
```

```python
import functools

import jax
import jax.numpy as jnp
from jax import lax
from jax.experimental import pallas as pl
from jax.experimental.pallas import tpu as pltpu

F32 = jnp.float32
BF16 = jnp.bfloat16
HIGHEST = lax.Precision.HIGHEST

HEAD_DIM = 64
LANES = 128
A_CHUNK = 128
B_HEADS = 8
B_BLOCK = 256
B_TOPK = 3
C_PATTERNS = ((128, 1), (512, 4), (2048, 16))
N_EXPERTS = 8
RMS_EPS = 1e-6
LN_EPS = 1e-5
NEG = -1e30
QK_SCALE = HEAD_DIM ** -0.5
VMEM_LIMIT = 56 * 1024 * 1024

NT = (((1,), (1,)), ((), ()))


def _cparams(sem):
    return pltpu.CompilerParams(dimension_semantics=sem, vmem_limit_bytes=VMEM_LIMIT)


def _rms(x, g):
    return x * lax.rsqrt(jnp.mean(x * x, axis=-1, keepdims=True) + RMS_EPS) * g


def _gelu(x):
    return 0.5 * x * (1.0 + lax.erf(x * 0.7071067811865476))


def _iota(shape, dim):
    return lax.broadcasted_iota(jnp.int32, shape, dim)


def _even_in_body(x_ref, g_ref, w_ref, lng_ref, lnb_ref, ws_ref, bias_ref,
                  a_ref, v_ref, q_ref, k_ref, vv_ref, *, n_prompt_tiles, aw, bw):
    i = pl.program_id(0)
    tm = x_ref.shape[0]
    h = _rms(x_ref[...], g_ref[...]).astype(BF16)
    y = jnp.dot(h, w_ref[...], preferred_element_type=F32)
    u = _gelu(y[:, :aw])
    v = _gelu(y[:, aw:2 * aw])
    mu = jnp.mean(v, axis=-1, keepdims=True)
    vc = v - mu
    var = jnp.mean(vc * vc, axis=-1, keepdims=True)
    v = vc * lax.rsqrt(var + LN_EPS) * lng_ref[...] + lnb_ref[...]
    v_ref[...] = v
    q_ref[...] = y[:, 2 * aw:2 * aw + bw]
    k_ref[...] = y[:, 2 * aw + bw:2 * aw + 2 * bw]
    vv_ref[...] = y[:, 2 * aw + 2 * bw:]

    row = _iota((A_CHUNK, A_CHUNK), 0)
    col = _iota((A_CHUNK, A_CHUNK), 1)
    is_prompt = i < n_prompt_tiles
    ok = (col <= row) & (((row // 8) == (col // 8)) | is_prompt)
    lo = _iota((A_CHUNK, LANES), 1) < HEAD_DIM
    v16 = v.astype(BF16)
    n_groups = aw // HEAD_DIM
    wm = [jnp.where(ok, ws_ref[g], 0.0).astype(BF16) for g in range(n_groups)]
    for c in range(tm // A_CHUNK):
        rs = slice(c * A_CHUNK, (c + 1) * A_CHUNK)
        for j in range(aw // LANES):
            cs = slice(j * LANES, (j + 1) * LANES)
            vj = v16[rs, cs]
            ma = jnp.dot(wm[2 * j], vj, preferred_element_type=F32)
            mb = jnp.dot(wm[2 * j + 1], vj, preferred_element_type=F32)
            mixed = jnp.where(lo, ma, mb) + bias_ref[:, cs]
            a_ref[rs, cs] = (u[rs, cs] * mixed).astype(BF16)


def _even_in(x, n_prompt_rows, g, w16, lng, lnb, ws2, bias2, *, tm=256):
    rows, dm = x.shape
    aw = lng.shape[-1]
    n_in = w16.shape[1]
    bw = (n_in - 2 * aw) // 3
    n_groups = aw // HEAD_DIM
    npt = n_prompt_rows // tm
    sel = lambda i: jnp.minimum(i // npt, 1)
    row_spec = lambda width: pl.BlockSpec((tm, width), lambda i: (i, 0))
    return pl.pallas_call(
        functools.partial(_even_in_body, n_prompt_tiles=npt, aw=aw, bw=bw),
        grid=(rows // tm,),
        in_specs=[
            row_spec(dm),
            pl.BlockSpec((1, dm), lambda i: (0, 0)),
            pl.BlockSpec((dm, n_in), lambda i: (0, 0)),
            pl.BlockSpec((1, aw), lambda i: (0, 0)),
            pl.BlockSpec((1, aw), lambda i: (0, 0)),
            pl.BlockSpec((None, n_groups, A_CHUNK, A_CHUNK), lambda i: (sel(i), 0, 0, 0)),
            pl.BlockSpec((None, A_CHUNK, aw), lambda i: (sel(i), 0, 0)),
        ],
        out_specs=[row_spec(aw), row_spec(aw), row_spec(bw), row_spec(bw), row_spec(bw)],
        out_shape=[
            jax.ShapeDtypeStruct((rows, aw), BF16),
            jax.ShapeDtypeStruct((rows, aw), F32),
            jax.ShapeDtypeStruct((rows, bw), F32),
            jax.ShapeDtypeStruct((rows, bw), F32),
            jax.ShapeDtypeStruct((rows, bw), F32),
        ],
        compiler_params=_cparams(("arbitrary",)),
        name="even_in",
    )(x, g, w16, lng, lnb, ws2, bias2)


def _moba_prompt_body(q_ref, k_ref, v_ref, o_ref, k16, v16, mt_ref, *, nb):
    i = pl.program_id(1)
    t_len, width = k_ref.shape
    n_gate = B_HEADS * nb

    @pl.when(i == 0)
    def _():
        k = k_ref[...]
        k16[...] = k.astype(BF16)
        v16[...] = v_ref[...].astype(BF16)
        r = _iota((LANES, t_len), 0)
        key = _iota((LANES, t_len), 1)
        avg = jnp.where(((key // B_BLOCK) == (r % nb)) & (r < n_gate), 1.0 / B_BLOCK, 0.0).astype(F32)
        mt = jnp.dot(avg, k, precision=HIGHEST, preferred_element_type=F32)
        rr = _iota((LANES, width), 0)
        cc = _iota((LANES, width), 1)
        mt_ref[...] = jnp.where(((rr // nb) == (cc // HEAD_DIM)) & (rr < n_gate), mt, 0.0)

    q = q_ref[...]
    tq = q.shape[0]
    gate = lax.dot_general(q, mt_ref[...], NT, precision=HIGHEST, preferred_element_type=F32)
    lane = _iota((tq, LANES), 1)
    n_of = lane % nb
    valid = (n_of < i) & (lane < n_gate)
    gm = jnp.where(valid, gate, -jnp.inf)
    rank = jnp.zeros((tq, LANES), jnp.int32)
    for s in range(1, nb):
        fwd = pltpu.roll(gm, LANES - s, 1)
        bwd = pltpu.roll(gm, nb - s, 1)
        wrap = (n_of + s) >= nb
        other = jnp.where(wrap, bwd, fwd)
        m_idx = jnp.where(wrap, n_of + s - nb, n_of + s)
        beats = (other > gm) | ((other == gm) & (m_idx < n_of))
        rank = rank + beats.astype(jnp.int32)
    selbias = jnp.where(valid & (rank < B_TOPK), 0.0, NEG)

    row = _iota((tq, B_BLOCK), 0)
    col = _iota((tq, B_BLOCK), 1)
    causal = col <= row
    lo = lane < HEAD_DIM
    own = pl.multiple_of(i * B_BLOCK, B_BLOCK)
    for j in range(width // LANES):
        cs = slice(j * LANES, (j + 1) * LANES)
        qj = q[:, cs] * QK_SCALE
        kcur = k16[pl.ds(own, B_BLOCK), cs]
        vcur = v16[pl.ds(own, B_BLOCK), cs]
        pair = []
        for a in range(2):
            head = 2 * j + a
            qh = jnp.where(lo if a == 0 else jnp.logical_not(lo), qj, 0.0).astype(BF16)
            s = lax.dot_general(qh, kcur, NT, preferred_element_type=F32)
            s = jnp.where(causal, s, NEG)
            m = jnp.max(s, axis=1, keepdims=True)
            p = jnp.exp(s - m)
            l = jnp.sum(p, axis=1, keepdims=True)
            acc = jnp.dot(p.astype(BF16), vcur, preferred_element_type=F32)

            def body(n, carry, qh=qh, head=head, cs=cs):
                m, l, acc = carry
                start = pl.multiple_of(n * B_BLOCK, B_BLOCK)
                kn = k16[pl.ds(start, B_BLOCK), cs]
                vn = v16[pl.ds(start, B_BLOCK), cs]
                bias = jnp.sum(jnp.where(lane == head * nb + n, selbias, 0.0), axis=1, keepdims=True)
                s = lax.dot_general(qh, kn, NT, preferred_element_type=F32) + bias
                m2 = jnp.maximum(m, jnp.max(s, axis=1, keepdims=True))
                alpha = jnp.exp(m - m2)
                p = jnp.exp(s - m2)
                l = alpha * l + jnp.sum(p, axis=1, keepdims=True)
                acc = alpha * acc + jnp.dot(p.astype(BF16), vn, preferred_element_type=F32)
                return m2, l, acc

            m, l, acc = lax.fori_loop(0, i, body, (m, l, acc))
            pair.append(acc / l)
        o_ref[:, cs] = jnp.where(lo, pair[0], pair[1]).astype(BF16)


def _moba_prompt(q, k, v, bsz, t_len):
    width = q.shape[1]
    nb = t_len // B_BLOCK
    return pl.pallas_call(
        functools.partial(_moba_prompt_body, nb=nb),
        grid=(bsz, nb),
        in_specs=[
            pl.BlockSpec((B_BLOCK, width), lambda b, i: (b * nb + i, 0)),
            pl.BlockSpec((t_len, width), lambda b, i: (b, 0)),
            pl.BlockSpec((t_len, width), lambda b, i: (b, 0)),
        ],
        out_specs=pl.BlockSpec((B_BLOCK, width), lambda b, i: (b * nb + i, 0)),
        out_shape=jax.ShapeDtypeStruct((bsz * t_len, width), BF16),
        scratch_shapes=[
            pltpu.VMEM((t_len, width), BF16),
            pltpu.VMEM((t_len, width), BF16),
            pltpu.VMEM((LANES, width), F32),
        ],
        compiler_params=_cparams(("arbitrary", "arbitrary")),
        name="moba_prompt",
    )(q, k, v)


def _moba_select_body(pt_ref, q_ref, *refs, npg, ppb, n_blocks):
    pages = refs[:npg]
    idx_ref = refs[npg]
    mt_ref = refs[npg + 1]
    s = pl.program_id(1)
    t_new, width = q_ref.shape
    bps = npg // ppb

    @pl.when(s == 0)
    def _():
        mt_ref[...] = jnp.zeros(mt_ref.shape, F32)

    lane_w = _iota((width, LANES), 1)
    cur = mt_ref[...]
    for blk in range(bps):
        tot = pages[blk * ppb][...]
        for r in range(1, ppb):
            tot = tot + pages[blk * ppb + r][...]
        col = jnp.sum(tot, axis=1, keepdims=True) * (1.0 / B_BLOCK)
        cur = jnp.where(lane_w == s * bps + blk, col, cur)
    mt_ref[...] = cur

    @pl.when(s == pl.num_programs(1) - 1)
    def _():
        ht = B_HEADS * t_new
        rr = _iota((ht, width), 0)
        cc = _iota((ht, width), 1)
        q = q_ref[...]
        qst = jnp.where((rr // t_new) == (cc // HEAD_DIM), jnp.concatenate([q] * B_HEADS, axis=0), 0.0)
        gate = jnp.dot(qst, cur, precision=HIGHEST, preferred_element_type=F32)
        lane = _iota((ht, LANES), 1)
        g = jnp.where(lane < n_blocks, gate, -jnp.inf)
        out = jnp.full((ht, LANES), -1, jnp.int32)
        for r in range(B_TOPK):
            mx = jnp.max(g, axis=1, keepdims=True)
            idx = jnp.min(jnp.where(g == mx, lane, LANES), axis=1, keepdims=True)
            out = jnp.where(lane == r, jnp.where(mx > -jnp.inf, idx, -1), out)
            g = jnp.where(lane == idx, -jnp.inf, g)
        idx_ref[...] = out


def _moba_select(q, row_block0, pool_kt, page_table, *, npg=8):
    bsz, n_pages = page_table.shape
    _, width, page = pool_kt.shape
    ppb = B_BLOCK // page
    t_new = 8
    ht = B_HEADS * t_new
    page_specs = [
        pl.BlockSpec((None, width, page), lambda b, s, pt, r=r: (pt[b, s * npg + r], 0, 0))
        for r in range(npg)
    ]
    return pl.pallas_call(
        functools.partial(_moba_select_body, npg=npg, ppb=ppb, n_blocks=n_pages // ppb),
        grid_spec=pltpu.PrefetchScalarGridSpec(
            num_scalar_prefetch=1,
            grid=(bsz, n_pages // npg),
            in_specs=[pl.BlockSpec((t_new, width), lambda b, s, pt: (row_block0 + b, 0))] + page_specs,
            out_specs=pl.BlockSpec((ht, LANES), lambda b, s, pt: (b, 0)),
            scratch_shapes=[pltpu.VMEM((width, LANES), F32)],
        ),
        out_shape=jax.ShapeDtypeStruct((bsz * ht, LANES), jnp.int32),
        compiler_params=_cparams(("arbitrary", "arbitrary")),
        name="moba_select",
    )(page_table, q, *([pool_kt] * npg))


def _moba_gather_body(pt_ref, sel_ref, qt_ref, knt_ref, vnt_ref, *refs, ppb):
    b = pl.program_id(0)
    h = pl.program_id(1)
    t_new = qt_ref.shape[1]
    n_chunk = t_new * B_TOPK * ppb
    kch = refs[:n_chunk]
    vch = refs[n_chunk:2 * n_chunk]
    o_ref = refs[2 * n_chunk]
    qt = qt_ref[...] * QK_SCALE
    knt = knt_ref[...]
    vnt = vnt_ref[...]
    lane_n = _iota((1, t_new), 1)
    lane_o = _iota((HEAD_DIM, t_new), 1)
    out = jnp.zeros((HEAD_DIM, t_new), F32)
    for t in range(t_new):
        qcol = qt[:, t:t + 1]
        s_own = jnp.where(lane_n <= t, jnp.sum(qcol * knt, axis=0, keepdims=True), NEG)
        m = jnp.max(s_own, axis=1, keepdims=True)
        scores = []
        for r in range(B_TOPK):
            picked = sel_ref[b, (h * t_new + t) * B_TOPK + r] >= 0
            for pg in range(ppb):
                kc = kch[(t * B_TOPK + r) * ppb + pg][...]
                s = jnp.where(picked, jnp.sum(qcol * kc, axis=0, keepdims=True), NEG)
                m = jnp.maximum(m, jnp.max(s, axis=1, keepdims=True))
                scores.append(s)
        p_own = jnp.exp(s_own - m)
        l = jnp.sum(p_own, axis=1, keepdims=True)
        acc = jnp.sum(p_own * vnt, axis=1, keepdims=True)
        accv = jnp.zeros((HEAD_DIM, LANES), F32)
        for c, s in enumerate(scores):
            p = jnp.exp(s - m)
            l = l + jnp.sum(p, axis=1, keepdims=True)
            accv = accv + p * vch[t * B_TOPK * ppb + c][...]
        acc = acc + jnp.sum(accv, axis=1, keepdims=True)
        out = jnp.where(lane_o == t, acc / l, out)
    o_ref[...] = out


def _moba_gather(qt, knt, vnt, pool_kt, pool_vt, page_table, sel):
    bsz, width, t_new = qt.shape
    _, n_heads, hd, page = pool_kt.shape
    ppb = B_BLOCK // page
    new_spec = pl.BlockSpec((None, hd, t_new), lambda b, h, pt, sl: (b, h, 0))

    def chunk_spec(t, r, pg):
        def imap(b, h, pt, sl):
            blk = jnp.maximum(sl[b, (h * t_new + t) * B_TOPK + r], 0)
            return (pt[b, blk * ppb + pg], h, 0, 0)
        return pl.BlockSpec((None, None, hd, page), imap)

    chunk_specs = [chunk_spec(t, r, pg) for t in range(t_new) for r in range(B_TOPK) for pg in range(ppb)]
    n_chunk = len(chunk_specs)
    return pl.pallas_call(
        functools.partial(_moba_gather_body, ppb=ppb),
        grid_spec=pltpu.PrefetchScalarGridSpec(
            num_scalar_prefetch=2,
            grid=(bsz, n_heads),
            in_specs=[new_spec, new_spec, new_spec] + chunk_specs + chunk_specs,
            out_specs=pl.BlockSpec((None, hd, t_new), lambda b, h, pt, sl: (b, h, 0)),
        ),
        out_shape=jax.ShapeDtypeStruct((bsz, width, t_new), F32),
        compiler_params=_cparams(("arbitrary", "arbitrary")),
        name="moba_gather",
    )(page_table, sel, qt, knt, vnt, *([pool_kt] * n_chunk), *([pool_vt] * n_chunk))


def _silu(x):
    return x / (1.0 + jnp.exp(-x))


def _ffn_even_body(x_ref, a_ref, b_ref, wo_ref, g_ref, wg_ref, wu_ref, wd_ref, o_ref, hn_ref, acc_ref):
    f = pl.program_id(1)
    aw = a_ref.shape[1]

    @pl.when(f == 0)
    def _():
        x1 = (x_ref[...]
              + jnp.dot(a_ref[...], wo_ref[:aw, :], preferred_element_type=F32)
              + jnp.dot(b_ref[...], wo_ref[aw:, :], preferred_element_type=F32))
        acc_ref[...] = x1
        hn_ref[...] = _rms(x1, g_ref[...]).astype(BF16)

    hn = hn_ref[...]
    gate = jnp.dot(hn, wg_ref[...], preferred_element_type=F32)
    up = jnp.dot(hn, wu_ref[...], preferred_element_type=F32)
    act = (_silu(gate) * up).astype(BF16)
    acc_ref[...] += jnp.dot(act, wd_ref[...], preferred_element_type=F32)

    @pl.when(f == pl.num_programs(1) - 1)
    def _():
        o_ref[...] = acc_ref[...]


def _ffn_even(x, a, b, wo16, g, wg16, wu16, wd16, *, tm=640, tf=1408):
    rows, dm = x.shape
    aw = a.shape[1]
    bw = b.shape[1]
    dff = wg16.shape[1]
    return pl.pallas_call(
        _ffn_even_body,
        grid=(rows // tm, dff // tf),
        in_specs=[
            pl.BlockSpec((tm, dm), lambda i, f: (i, 0)),
            pl.BlockSpec((tm, aw), lambda i, f: (i, 0)),
            pl.BlockSpec((tm, bw), lambda i, f: (i, 0)),
            pl.BlockSpec((aw + bw, dm), lambda i, f: (0, 0)),
            pl.BlockSpec((1, dm), lambda i, f: (0, 0)),
            pl.BlockSpec((dm, tf), lambda i, f: (0, f)),
            pl.BlockSpec((dm, tf), lambda i, f: (0, f)),
            pl.BlockSpec((tf, dm), lambda i, f: (f, 0)),
        ],
        out_specs=pl.BlockSpec((tm, dm), lambda i, f: (i, 0)),
        out_shape=jax.ShapeDtypeStruct((rows, dm), F32),
        scratch_shapes=[pltpu.VMEM((tm, dm), BF16), pltpu.VMEM((tm, dm), F32)],
        compiler_params=_cparams(("arbitrary", "arbitrary")),
        name="ffn_even",
    )(x, a, b, wo16, g, wg16, wu16, wd16)


def _odd_in_body(x_ref, g_ref, w_ref, q_ref, k_ref, v_ref):
    cw = q_ref.shape[1]
    h = _rms(x_ref[...], g_ref[...]).astype(BF16)
    y = jnp.dot(h, w_ref[...], preferred_element_type=F32)
    q_ref[...] = y[:, :cw]
    k_ref[...] = y[:, cw:2 * cw]
    v_ref[...] = y[:, 2 * cw:]


def _odd_in(x, g, w16, *, tm=256):
    rows, dm = x.shape
    cw = w16.shape[1] // 3
    out = jax.ShapeDtypeStruct((rows, cw), F32)
    spec = pl.BlockSpec((tm, cw), lambda i: (i, 0))
    return pl.pallas_call(
        _odd_in_body,
        grid=(rows // tm,),
        in_specs=[
            pl.BlockSpec((tm, dm), lambda i: (i, 0)),
            pl.BlockSpec((1, dm), lambda i: (0, 0)),
            pl.BlockSpec((dm, 3 * cw), lambda i: (0, 0)),
        ],
        out_specs=[spec, spec, spec],
        out_shape=[out, out, out],
        compiler_params=_cparams(("arbitrary",)),
        name="odd_in",
    )(x, g, w16)


def _dilated_prompt_body(q_ref, k_ref, v_ref, o_ref, m_s, l_s, acc_s, *, patterns):
    t_len = q_ref.shape[0]
    blk = 128
    lane = _iota((blk, LANES), 1)
    lo = lane < HEAD_DIM
    row = _iota((blk, blk), 0)
    col = _iota((blk, blk), 1)
    cur_ok = col <= row
    prev_ok = col >= row
    n_br = len(patterns)
    for bi, (window, dil) in enumerate(patterns):
        assert window // dil == blk
        nblk = t_len // (dil * blk)

        def body(it, _, bi=bi, dil=dil, nblk=nblk):
            cls = it // nblk
            ib = it % nblk

            def rows_at(block):
                start = cls + dil * blk * block
                if dil == 1:
                    return pl.ds(pl.multiple_of(start, blk), blk)
                return pl.ds(start, blk, stride=dil)

            rows = rows_at(ib)
            q = q_ref[rows, :] * QK_SCALE
            kc = k_ref[rows, :].astype(BF16)
            vc = v_ref[rows, :].astype(BF16)
            has_prev = nblk > 1
            if has_prev:
                prows = rows_at(jnp.maximum(ib - 1, 0))
                kp = k_ref[prows, :].astype(BF16)
                vp = v_ref[prows, :].astype(BF16)
                p_ok = prev_ok & (ib > 0)
            sc, sp, mb = [], [], []
            for a in range(2):
                qh = jnp.where(lo if a == 0 else jnp.logical_not(lo), q, 0.0).astype(BF16)
                s_c = jnp.where(cur_ok, lax.dot_general(qh, kc, NT, preferred_element_type=F32), NEG)
                mx = jnp.max(s_c, axis=1, keepdims=True)
                sc.append(s_c)
                if has_prev:
                    s_p = jnp.where(p_ok, lax.dot_general(qh, kp, NT, preferred_element_type=F32), NEG)
                    mx = jnp.maximum(mx, jnp.max(s_p, axis=1, keepdims=True))
                    sp.append(s_p)
                mb.append(mx)
            m_blk = jnp.where(lo, mb[0], mb[1])
            if bi == 0:
                m_new = m_blk
            else:
                m_old = m_s[rows, :]
                m_new = jnp.maximum(m_old, m_blk)
                alpha = jnp.exp(m_old - m_new)
            ls, pv = [], []
            for a in range(2):
                m_a = m_new[:, a * HEAD_DIM:a * HEAD_DIM + 1]
                p_c = jnp.exp(sc[a] - m_a)
                l_a = jnp.sum(p_c, axis=1, keepdims=True)
                pv_a = jnp.dot(p_c.astype(BF16), vc, preferred_element_type=F32)
                if has_prev:
                    p_p = jnp.exp(sp[a] - m_a)
                    l_a = l_a + jnp.sum(p_p, axis=1, keepdims=True)
                    pv_a = pv_a + jnp.dot(p_p.astype(BF16), vp, preferred_element_type=F32)
                ls.append(l_a)
                pv.append(pv_a)
            l_new = jnp.where(lo, ls[0], ls[1])
            acc_new = jnp.where(lo, pv[0], pv[1])
            if bi > 0:
                l_new = alpha * l_s[rows, :] + l_new
                acc_new = alpha * acc_s[rows, :] + acc_new
            if bi == n_br - 1:
                acc_s[rows, :] = acc_new / l_new
            else:
                m_s[rows, :] = m_new
                l_s[rows, :] = l_new
                acc_s[rows, :] = acc_new
            return 0

        lax.fori_loop(0, dil * nblk, body, 0)
    o_ref[...] = acc_s[...].astype(BF16)


def _dilated_prompt(q, k, v, bsz, t_len):
    width = q.shape[1]
    spec = pl.BlockSpec((t_len, LANES), lambda b, j: (b, j))
    return pl.pallas_call(
        functools.partial(_dilated_prompt_body, patterns=C_PATTERNS),
        grid=(bsz, width // LANES),
        in_specs=[spec, spec, spec],
        out_specs=spec,
        out_shape=jax.ShapeDtypeStruct((bsz * t_len, width), BF16),
        scratch_shapes=[pltpu.VMEM((t_len, LANES), F32)] * 3,
        compiler_params=_cparams(("arbitrary", "arbitrary")),
        name="dilated_prompt",
    )(q, k, v)


def _dilated_sample_body(q_ref, kn_ref, vn_ref, kc_ref, vc_ref, o_ref, *, patterns):
    t_new = q_ref.shape[0]
    w_len = kc_ref.shape[1]
    nrow = 2 * t_new
    lane = _iota((t_new, LANES), 1)
    q = q_ref[...] * QK_SCALE
    qst = jnp.concatenate([jnp.where(lane < HEAD_DIM, q, 0.0), jnp.where(lane >= HEAD_DIM, q, 0.0)], axis=0)
    qst = qst.astype(BF16)

    def mult(delta):
        cnt = jnp.zeros(delta.shape, F32)
        for window, dil in patterns:
            hit = (delta >= 0) & (delta <= window) & ((delta % dil) == 0)
            cnt = cnt + hit.astype(F32)
        return cnt

    t_c = _iota((nrow, w_len), 0) % t_new
    w_c = mult(w_len + t_c - _iota((nrow, w_len), 1))
    t_n = _iota((nrow, LANES), 0) % t_new
    c_n = _iota((nrow, LANES), 1)
    w_n = jnp.where(c_n < t_new, mult(t_n - c_n), 0.0)

    pad = jnp.zeros((LANES - t_new, LANES), F32)
    kn = jnp.concatenate([kn_ref[...], pad], axis=0).astype(BF16)
    vn = jnp.concatenate([vn_ref[...], pad], axis=0).astype(BF16)
    s_c = jnp.dot(qst, kc_ref[...].astype(BF16), preferred_element_type=F32)
    s_n = lax.dot_general(qst, kn, NT, preferred_element_type=F32)
    s_c = jnp.where(w_c > 0, s_c, NEG)
    s_n = jnp.where(w_n > 0, s_n, NEG)
    m = jnp.maximum(jnp.max(s_c, axis=1, keepdims=True), jnp.max(s_n, axis=1, keepdims=True))
    p_c = w_c * jnp.exp(s_c - m)
    p_n = w_n * jnp.exp(s_n - m)
    l = jnp.sum(p_c, axis=1, keepdims=True) + jnp.sum(p_n, axis=1, keepdims=True)
    acc = (lax.dot_general(p_c.astype(BF16), vc_ref[...].astype(BF16), NT, preferred_element_type=F32)
           + jnp.dot(p_n.astype(BF16), vn, preferred_element_type=F32))
    out = acc / l
    o_ref[...] = jnp.where(lane < HEAD_DIM, out[:t_new], out[t_new:])


def _dilated_sample(q, k, v, row_block0, cache_kt, cache_vt):
    bsz, width, w_len = cache_kt.shape
    t_new = 8
    new_spec = pl.BlockSpec((t_new, LANES), lambda b, j: (row_block0 + b, j))
    cache_spec = pl.BlockSpec((None, LANES, w_len), lambda b, j: (b, j, 0))
    return pl.pallas_call(
        functools.partial(_dilated_sample_body, patterns=C_PATTERNS),
        grid=(bsz, width // LANES),
        in_specs=[new_spec, new_spec, new_spec, cache_spec, cache_spec],
        out_specs=pl.BlockSpec((t_new, LANES), lambda b, j: (b, j)),
        out_shape=jax.ShapeDtypeStruct((bsz * t_new, width), F32),
        compiler_params=_cparams(("arbitrary", "arbitrary")),
        name="dilated_sample",
    )(q, k, v, cache_kt, cache_vt)


def _moe_body(x_ref, c_ref, wo_ref, g_ref, r_ref, gf_ref, wg_ref, wu_ref, wd_ref,
              o_ref, hn_ref, acc_ref, comb_ref):
    e = pl.program_id(1)
    f = pl.program_id(2)
    tm = x_ref.shape[0]
    lane = _iota((tm, LANES), 1)

    @pl.when((e == 0) & (f == 0))
    def _():
        x3 = x_ref[...] + jnp.dot(c_ref[...], wo_ref[...], preferred_element_type=F32)
        acc_ref[...] = x3
        hf = _rms(x3, g_ref[...])
        hn_ref[...] = hf.astype(BF16)
        logits = jnp.dot(hf, r_ref[...], precision=HIGHEST, preferred_element_type=F32)
        lg = jnp.where(lane < N_EXPERTS, logits, -jnp.inf)
        m1 = jnp.max(lg, axis=1, keepdims=True)
        p1 = lane == jnp.min(jnp.where(lg == m1, lane, LANES), axis=1, keepdims=True)
        lg2 = jnp.where(p1, -jnp.inf, lg)
        m2 = jnp.max(lg2, axis=1, keepdims=True)
        p2 = lane == jnp.min(jnp.where(lg2 == m2, lane, LANES), axis=1, keepdims=True)
        e2 = jnp.exp(m2 - m1)
        den = 1.0 + e2
        comb_ref[...] = jnp.where(p1, 1.0 / den, 0.0) + jnp.where(p2, e2 / den, 0.0)

    hn = hn_ref[...]
    gate = jnp.dot(hn, wg_ref[...], preferred_element_type=F32)
    up = jnp.dot(hn, wu_ref[...], preferred_element_type=F32)
    act = (_silu(gate) * up).astype(BF16)
    y = jnp.dot(act, wd_ref[...], preferred_element_type=F32)
    w_e = jnp.sum(jnp.where(lane == e, comb_ref[...], 0.0), axis=1, keepdims=True)
    acc_ref[...] += w_e * y

    @pl.when((e == pl.num_programs(1) - 1) & (f == pl.num_programs(2) - 1))
    def _():
        o_ref[...] = _rms(acc_ref[...], gf_ref[...])


def _moe(x, c, wo16, g, router_pad, g_final, wg16, wu16, wd16, *, tm=640, tf=1408):
    rows, dm = x.shape
    n_exp, _, dff = wg16.shape
    return pl.pallas_call(
        _moe_body,
        grid=(rows // tm, n_exp, dff // tf),
        in_specs=[
            pl.BlockSpec((tm, dm), lambda i, e, f: (i, 0)),
            pl.BlockSpec((tm, dm), lambda i, e, f: (i, 0)),
            pl.BlockSpec((dm, dm), lambda i, e, f: (0, 0)),
            pl.BlockSpec((1, dm), lambda i, e, f: (0, 0)),
            pl.BlockSpec((dm, LANES), lambda i, e, f: (0, 0)),
            pl.BlockSpec((1, dm), lambda i, e, f: (0, 0)),
            pl.BlockSpec((None, dm, tf), lambda i, e, f: (e, 0, f)),
            pl.BlockSpec((None, dm, tf), lambda i, e, f: (e, 0, f)),
            pl.BlockSpec((None, tf, dm), lambda i, e, f: (e, f, 0)),
        ],
        out_specs=pl.BlockSpec((tm, dm), lambda i, e, f: (i, 0)),
        out_shape=jax.ShapeDtypeStruct((rows, dm), F32),
        scratch_shapes=[
            pltpu.VMEM((tm, dm), BF16),
            pltpu.VMEM((tm, dm), F32),
            pltpu.VMEM((tm, LANES), F32),
        ],
        compiler_params=_cparams(("arbitrary", "arbitrary", "arbitrary")),
        name="moe",
    )(x, c, wo16, g, router_pad, g_final, wg16, wu16, wd16)


def kernel(x_prompt, x_sample, cache_b_k, cache_b_v, cache_c_k, cache_c_v, page_table, norm_mix, norm_ffn,
           norm_final, w_in_even, w_out_even, a_ln_g, a_ln_b, a_ws, a_bs, ffn_gate, ffn_up, ffn_down,
           w_in_odd, w_out_odd, router, moe_gate, moe_up, moe_down):
    bsz, t_len, dm = x_prompt.shape
    dbsz, t_new, _ = x_sample.shape
    assert norm_mix.shape[0] == 2 and w_in_even.shape[0] == 1 and w_in_odd.shape[0] == 1
    assert cache_b_k.shape[0] == 1 and cache_c_k.shape[0] == 1
    assert t_new == 8 and A_CHUNK % t_new == 0
    n_p = bsz * t_len
    n_s = dbsz * t_new
    aw = a_ln_g.shape[-1]
    n_groups = aw // HEAD_DIM
    seq_per_chunk = A_CHUNK // t_new

    x = jnp.concatenate([x_prompt.reshape(n_p, dm), x_sample.reshape(n_s, dm)], axis=0)

    ws = a_ws[0]
    ws_sample = jnp.tile(ws[:, :t_new, :t_new], (1, seq_per_chunk, seq_per_chunk))
    ws2 = jnp.stack([ws, ws_sample])
    bias_p = jnp.repeat(a_bs[0].T, HEAD_DIM, axis=1)
    bias_s = jnp.repeat(jnp.tile(a_bs[0][:, :t_new].T, (seq_per_chunk, 1)), HEAD_DIM, axis=1)
    bias2 = jnp.stack([bias_p, bias_s])
    row = lambda vec: vec.reshape(1, -1)
    router_pad = jnp.pad(router[0], ((0, 0), (0, LANES - router.shape[-1])))

    a_out, v_a, q_b, k_b, v_b = _even_in(
        x, n_p, row(norm_mix[0]), w_in_even[0].astype(BF16), row(a_ln_g[0]), row(a_ln_b[0]), ws2, bias2)
    b_prompt = _moba_prompt(q_b, k_b, v_b, bsz, t_len)
    page = cache_b_k.shape[2]
    bw = q_b.shape[1]
    feature_major = lambda c: jnp.transpose(c, (0, 1, 3, 4, 2))
    pool_kt = feature_major(cache_b_k).reshape(-1, B_HEADS, HEAD_DIM, page)
    pool_vt = feature_major(cache_b_v).reshape(-1, B_HEADS, HEAD_DIM, page)
    sel = _moba_select(q_b, n_p // t_new, pool_kt.reshape(-1, bw, page), page_table)
    sel = sel.reshape(dbsz, B_HEADS * t_new, LANES)[:, :, :B_TOPK].reshape(dbsz, -1)
    new_t = lambda a: jnp.transpose(a[n_p:].reshape(dbsz, t_new, -1), (0, 2, 1))
    b_sample_t = _moba_gather(new_t(q_b), new_t(k_b), new_t(v_b), pool_kt, pool_vt, page_table, sel)
    b_sample = jnp.transpose(b_sample_t, (0, 2, 1)).reshape(n_s, bw)
    b_out = jnp.concatenate([b_prompt, b_sample.astype(BF16)], axis=0)
    x2 = _ffn_even(x, a_out, b_out, w_out_even[0].astype(BF16), row(norm_ffn[0]),
                   ffn_gate[0].astype(BF16), ffn_up[0].astype(BF16), ffn_down[0].astype(BF16))

    q_c, k_c, v_c = _odd_in(x2, row(norm_mix[1]), w_in_odd[0].astype(BF16))
    c_prompt = _dilated_prompt(q_c, k_c, v_c, bsz, t_len)
    cw = q_c.shape[1]
    w_len = cache_c_k.shape[2]
    c_sample = _dilated_sample(q_c, k_c, v_c, n_p // t_new,
                               feature_major(cache_c_k).reshape(-1, cw, w_len),
                               feature_major(cache_c_v).reshape(-1, cw, w_len))
    c_out = jnp.concatenate([c_prompt, c_sample.astype(BF16)], axis=0)
    y = _moe(x2, c_out, w_out_odd[0].astype(BF16), row(norm_ffn[1]), router_pad, row(norm_final),
             moe_gate[0].astype(BF16), moe_up[0].astype(BF16), moe_down[0].astype(BF16))

    bh = (B_HEADS, HEAD_DIM)
    ch = (cw // HEAD_DIM, HEAD_DIM)
    a_keep = ((t_len - 1) // A_CHUNK) * A_CHUNK
    c_keep_p = min(C_PATTERNS[-1][0], t_len)
    return (
        y[:n_p].reshape(bsz, t_len, dm),
        y[n_p:].reshape(dbsz, t_new, dm),
        v_a[:n_p].reshape(1, bsz, t_len, aw)[:, :, a_keep:],
        v_a[n_p:].reshape(1, dbsz, t_new, aw),
        k_b[:n_p].reshape(1, bsz, t_len, *bh),
        v_b[:n_p].reshape(1, bsz, t_len, *bh),
        k_b[n_p:].reshape(1, dbsz, t_new, *bh),
        v_b[n_p:].reshape(1, dbsz, t_new, *bh),
        k_c[:n_p].reshape(1, bsz, t_len, *ch)[:, :, t_len - c_keep_p:],
        v_c[:n_p].reshape(1, bsz, t_len, *ch)[:, :, t_len - c_keep_p:],
        k_c[n_p:].reshape(1, dbsz, t_new, *ch),
        v_c[n_p:].reshape(1, dbsz, t_new, *ch),
    )
```

```python
import functools

import jax
import jax.numpy as jnp
from jax import lax
from jax.experimental import pallas as pl
from jax.experimental.pallas import tpu as pltpu

F32 = jnp.float32
BF16 = jnp.bfloat16
HIGHEST = lax.Precision.HIGHEST

HEAD_DIM = 64
LANES = 128
A_CHUNK = 128
B_HEADS = 8
B_BLOCK = 256
B_TOPK = 3
C_PATTERNS = ((128, 1), (512, 4), (2048, 16))
N_EXPERTS = 8
RMS_EPS = 1e-6
LN_EPS = 1e-5
NEG = -1e30
QK_SCALE = HEAD_DIM ** -0.5
VMEM_LIMIT = 56 * 1024 * 1024

NT = (((1,), (1,)), ((), ()))


def _cparams(sem):
    return pltpu.CompilerParams(dimension_semantics=sem, vmem_limit_bytes=VMEM_LIMIT)


def _rms(x, g):
    return x * lax.rsqrt(jnp.mean(x * x, axis=-1, keepdims=True) + RMS_EPS) * g


def _gelu(x):
    return 0.5 * x * (1.0 + lax.erf(x * 0.7071067811865476))


def _iota(shape, dim):
    return lax.broadcasted_iota(jnp.int32, shape, dim)


def _feature_major_kv(i, n_prompt_tiles, wt_ref, h, kt_ref, vt_ref):
    width = kt_ref.shape[0]
    kvt = lax.dot_general(wt_ref[...], h, NT, preferred_element_type=F32)

    @pl.when(i < n_prompt_tiles)
    def _():
        kt_ref[...] = kvt[:width]
        vt_ref[...] = kvt[width:]


def _feature_major_spec(width, tm, n_prompt_tiles, tiles_per_seq):
    def imap(i):
        ic = jnp.minimum(i, n_prompt_tiles - 1)
        return (ic // tiles_per_seq, 0, ic % tiles_per_seq)
    return pl.BlockSpec((None, width, tm), imap)


def _even_in_body(x_ref, g_ref, w_ref, wt_ref, lng_ref, lnb_ref, ws_ref, bias_ref,
                  a_ref, v_ref, q_ref, k_ref, vv_ref, kt_ref, vt_ref, vlast_ref,
                  *, n_prompt_tiles, tiles_per_seq, aw, bw):
    i = pl.program_id(0)
    tm = x_ref.shape[0]
    h = _rms(x_ref[...], g_ref[...]).astype(BF16)
    y = jnp.dot(h, w_ref[...], preferred_element_type=F32)
    _feature_major_kv(i, n_prompt_tiles, wt_ref, h, kt_ref, vt_ref)
    u = _gelu(y[:, :aw])
    v = _gelu(y[:, aw:2 * aw])
    mu = jnp.mean(v, axis=-1, keepdims=True)
    vc = v - mu
    var = jnp.mean(vc * vc, axis=-1, keepdims=True)
    v = vc * lax.rsqrt(var + LN_EPS) * lng_ref[...] + lnb_ref[...]
    v_ref[...] = v

    @pl.when((i < n_prompt_tiles) & (i % tiles_per_seq == tiles_per_seq - 1))
    def _():
        vlast_ref[...] = v[tm - A_CHUNK:, :]
    q_ref[...] = y[:, 2 * aw:2 * aw + bw]
    k_ref[...] = y[:, 2 * aw + bw:2 * aw + 2 * bw]
    vv_ref[...] = y[:, 2 * aw + 2 * bw:]

    row = _iota((A_CHUNK, A_CHUNK), 0)
    col = _iota((A_CHUNK, A_CHUNK), 1)
    is_prompt = i < n_prompt_tiles
    ok = (col <= row) & (((row // 8) == (col // 8)) | is_prompt)
    lo = _iota((A_CHUNK, LANES), 1) < HEAD_DIM
    v16 = v.astype(BF16)
    n_groups = aw // HEAD_DIM
    wm = [jnp.where(ok, ws_ref[g], 0.0).astype(BF16) for g in range(n_groups)]
    for c in range(tm // A_CHUNK):
        rs = slice(c * A_CHUNK, (c + 1) * A_CHUNK)
        for j in range(aw // LANES):
            cs = slice(j * LANES, (j + 1) * LANES)
            vj = v16[rs, cs]
            ma = jnp.dot(wm[2 * j], vj, preferred_element_type=F32)
            mb = jnp.dot(wm[2 * j + 1], vj, preferred_element_type=F32)
            mixed = jnp.where(lo, ma, mb) + bias_ref[:, cs]
            a_ref[rs, cs] = (u[rs, cs] * mixed).astype(BF16)


def _even_in(x, bsz, t_len, g, w16, wt16, lng, lnb, ws2, bias2, *, tm=256):
    rows, dm = x.shape
    aw = lng.shape[-1]
    n_in = w16.shape[1]
    bw = (n_in - 2 * aw) // 3
    n_groups = aw // HEAD_DIM
    npt = bsz * t_len // tm
    sel = lambda i: jnp.minimum(i // npt, 1)
    row_spec = lambda width: pl.BlockSpec((tm, width), lambda i: (i, 0))
    tps = t_len // tm
    fm_spec = _feature_major_spec(bw, tm, npt, tps)
    return pl.pallas_call(
        functools.partial(_even_in_body, n_prompt_tiles=npt, tiles_per_seq=tps, aw=aw, bw=bw),
        grid=(rows // tm,),
        in_specs=[
            row_spec(dm),
            pl.BlockSpec((1, dm), lambda i: (0, 0)),
            pl.BlockSpec((dm, n_in), lambda i: (0, 0)),
            pl.BlockSpec((2 * bw, dm), lambda i: (0, 0)),
            pl.BlockSpec((1, aw), lambda i: (0, 0)),
            pl.BlockSpec((1, aw), lambda i: (0, 0)),
            pl.BlockSpec((None, n_groups, A_CHUNK, A_CHUNK), lambda i: (sel(i), 0, 0, 0)),
            pl.BlockSpec((None, A_CHUNK, aw), lambda i: (sel(i), 0, 0)),
        ],
        out_specs=[row_spec(aw), row_spec(aw), row_spec(bw), row_spec(bw), row_spec(bw), fm_spec, fm_spec,
                   pl.BlockSpec((None, A_CHUNK, aw), lambda i: (jnp.minimum(i, npt - 1) // tps, 0, 0))],
        out_shape=[
            jax.ShapeDtypeStruct((rows, aw), BF16),
            jax.ShapeDtypeStruct((rows, aw), F32),
            jax.ShapeDtypeStruct((rows, bw), F32),
            jax.ShapeDtypeStruct((rows, bw), F32),
            jax.ShapeDtypeStruct((rows, bw), F32),
            jax.ShapeDtypeStruct((bsz, bw, t_len), F32),
            jax.ShapeDtypeStruct((bsz, bw, t_len), F32),
            jax.ShapeDtypeStruct((bsz, A_CHUNK, aw), F32),
        ],
        compiler_params=_cparams(("arbitrary",)),
        name="even_in",
    )(x, g, w16, wt16, lng, lnb, ws2, bias2)


def _moba_prompt_body(q_ref, k_ref, v_ref, o_ref, k16, v16, mt_ref, s_scr, p_scr, *, nb):
    i = pl.program_id(1)
    t_len, width = k_ref.shape
    n_gate = B_HEADS * nb

    @pl.when(i == 0)
    def _():
        k = k_ref[...]
        k16[...] = k.astype(BF16)
        v16[...] = v_ref[...].astype(BF16)
        r = _iota((LANES, t_len), 0)
        key = _iota((LANES, t_len), 1)
        avg = jnp.where(((key // B_BLOCK) == (r % nb)) & (r < n_gate), 1.0 / B_BLOCK, 0.0).astype(F32)
        mt = jnp.dot(avg, k, precision=HIGHEST, preferred_element_type=F32)
        rr = _iota((LANES, width), 0)
        cc = _iota((LANES, width), 1)
        mt_ref[...] = jnp.where(((rr // nb) == (cc // HEAD_DIM)) & (rr < n_gate), mt, 0.0)

    q = q_ref[...]
    tq = q.shape[0]
    gate = lax.dot_general(q, mt_ref[...], NT, precision=HIGHEST, preferred_element_type=F32)
    lane = _iota((tq, LANES), 1)
    n_of = lane % nb
    valid = (n_of < i) & (lane < n_gate)
    gm = jnp.where(valid, gate, -jnp.inf)
    rank = jnp.zeros((tq, LANES), jnp.int32)
    for s in range(1, nb):
        fwd = pltpu.roll(gm, LANES - s, 1)
        bwd = pltpu.roll(gm, nb - s, 1)
        wrap = (n_of + s) >= nb
        other = jnp.where(wrap, bwd, fwd)
        m_idx = jnp.where(wrap, n_of + s - nb, n_of + s)
        beats = (other > gm) | ((other == gm) & (m_idx < n_of))
        rank = rank + beats.astype(jnp.int32)
    selbias = jnp.where(valid & (rank < B_TOPK), 0.0, NEG)

    row2 = _iota((2 * tq, B_BLOCK), 0) % tq
    col2 = _iota((2 * tq, B_BLOCK), 1)
    causal_bias = jnp.where(col2 <= row2, 0.0, NEG)
    lo = lane < HEAD_DIM
    own = pl.multiple_of(i * B_BLOCK, B_BLOCK)
    half = B_BLOCK // 2

    key_lane = _iota((B_BLOCK, LANES), 1)

    def zero_body(n, carry):
        p_scr[n] = jnp.zeros((2 * tq, B_BLOCK), BF16)
        return carry

    lax.fori_loop(i + 1, nb, zero_body, 0)
    for j in range(width // LANES):
        cs = slice(j * LANES, (j + 1) * LANES)
        qj = q[:, cs] * QK_SCALE
        qst = jnp.concatenate([jnp.where(lo, qj, 0.0), jnp.where(lo, 0.0, qj)], axis=0).astype(BF16)
        s = lax.dot_general(qst, k16[pl.ds(own, B_BLOCK), cs], NT, preferred_element_type=F32) + causal_bias
        s_scr[i] = s
        m_tile = jnp.maximum(s[:, :half], s[:, half:])
        sb = []
        for a in range(2):
            off = (2 * j + a) * nb
            moved = selbias if off == 0 else pltpu.roll(selbias, LANES - off, 1)
            sb.append(jnp.where(lane < nb, moved, 0.0))
        q_aug = jnp.concatenate([qst, jnp.concatenate(sb, axis=0).astype(BF16)], axis=1)

        def score_body(n, m_tile, q_aug=q_aug, cs=cs):
            start = pl.multiple_of(n * B_BLOCK, B_BLOCK)
            k_aug = jnp.concatenate([k16[pl.ds(start, B_BLOCK), cs],
                                     jnp.where(key_lane == n, 1.0, 0.0).astype(BF16)], axis=1)
            s = lax.dot_general(q_aug, k_aug, NT, preferred_element_type=F32)
            s_scr[n] = s
            return jnp.maximum(m_tile, jnp.maximum(s[:, :half], s[:, half:]))

        m_tile = lax.fori_loop(0, i, score_body, m_tile)
        m = jnp.max(m_tile, axis=1, keepdims=True)

        def prob_body(n, l_tile, m=m):
            p = jnp.exp(s_scr[n] - m)
            p_scr[n] = p.astype(BF16)
            return l_tile + p[:, :half] + p[:, half:]

        l_tile = lax.fori_loop(0, i + 1, prob_body, jnp.zeros((2 * tq, half), F32))
        l = jnp.sum(l_tile, axis=1, keepdims=True)
        acc = jnp.dot(p_scr[0], v16[0:B_BLOCK, cs], preferred_element_type=F32)
        for n in range(1, nb):
            acc = acc + jnp.dot(p_scr[n], v16[n * B_BLOCK:(n + 1) * B_BLOCK, cs], preferred_element_type=F32)
        o = acc / l
        o_ref[:, cs] = jnp.where(lo, o[:tq], o[tq:]).astype(BF16)


def _moba_prompt(q, k, v, bsz, t_len):
    width = q.shape[1]
    nb = t_len // B_BLOCK
    return pl.pallas_call(
        functools.partial(_moba_prompt_body, nb=nb),
        grid=(bsz, nb),
        in_specs=[
            pl.BlockSpec((B_BLOCK, width), lambda b, i: (b * nb + i, 0)),
            pl.BlockSpec((t_len, width), lambda b, i: (b, 0)),
            pl.BlockSpec((t_len, width), lambda b, i: (b, 0)),
        ],
        out_specs=pl.BlockSpec((B_BLOCK, width), lambda b, i: (b * nb + i, 0)),
        out_shape=jax.ShapeDtypeStruct((bsz * t_len, width), BF16),
        scratch_shapes=[
            pltpu.VMEM((t_len, width), BF16),
            pltpu.VMEM((t_len, width), BF16),
            pltpu.VMEM((LANES, width), F32),
            pltpu.VMEM((nb, 2 * B_BLOCK, B_BLOCK), F32),
            pltpu.VMEM((nb, 2 * B_BLOCK, B_BLOCK), BF16),
        ],
        compiler_params=_cparams(("arbitrary", "arbitrary")),
        name="moba_prompt",
    )(q, k, v)


def _moba_select_body(pt_ref, q_ref, *refs, npg, ppb, n_blocks):
    pages = refs[:npg]
    idx_ref = refs[npg]
    mt_ref = refs[npg + 1]
    s = pl.program_id(1)
    t_new, width = q_ref.shape
    bps = npg // ppb

    @pl.when(s == 0)
    def _():
        mt_ref[...] = jnp.zeros(mt_ref.shape, F32)

    lane_w = _iota((width, LANES), 1)
    cur = mt_ref[...]
    for blk in range(bps):
        tot = pages[blk * ppb][...]
        for r in range(1, ppb):
            tot = tot + pages[blk * ppb + r][...]
        col = jnp.sum(tot, axis=1, keepdims=True) * (1.0 / B_BLOCK)
        cur = jnp.where(lane_w == s * bps + blk, col, cur)
    mt_ref[...] = cur

    @pl.when(s == pl.num_programs(1) - 1)
    def _():
        ht = B_HEADS * t_new
        rr = _iota((ht, width), 0)
        cc = _iota((ht, width), 1)
        q = q_ref[...]
        qst = jnp.where((rr // t_new) == (cc // HEAD_DIM), jnp.concatenate([q] * B_HEADS, axis=0), 0.0)
        gate = jnp.dot(qst, cur, precision=HIGHEST, preferred_element_type=F32)
        lane = _iota((ht, LANES), 1)
        g = jnp.where(lane < n_blocks, gate, -jnp.inf)
        out = jnp.full((ht, LANES), -1, jnp.int32)
        for r in range(B_TOPK):
            mx = jnp.max(g, axis=1, keepdims=True)
            idx = jnp.min(jnp.where(g == mx, lane, LANES), axis=1, keepdims=True)
            out = jnp.where(lane == r, jnp.where(mx > -jnp.inf, idx, -1), out)
            g = jnp.where(lane == idx, -jnp.inf, g)
        idx_ref[...] = out


def _moba_select(q, row_block0, pool_kt, page_table, *, npg=16):
    bsz, n_pages = page_table.shape
    _, width, page = pool_kt.shape
    ppb = B_BLOCK // page
    t_new = 8
    ht = B_HEADS * t_new
    page_specs = [
        pl.BlockSpec((None, width, page), lambda b, s, pt, r=r: (pt[b, s * npg + r], 0, 0))
        for r in range(npg)
    ]
    return pl.pallas_call(
        functools.partial(_moba_select_body, npg=npg, ppb=ppb, n_blocks=n_pages // ppb),
        grid_spec=pltpu.PrefetchScalarGridSpec(
            num_scalar_prefetch=1,
            grid=(bsz, n_pages // npg),
            in_specs=[pl.BlockSpec((t_new, width), lambda b, s, pt: (row_block0 + b, 0))] + page_specs,
            out_specs=pl.BlockSpec((ht, LANES), lambda b, s, pt: (b, 0)),
            scratch_shapes=[pltpu.VMEM((width, LANES), F32)],
        ),
        out_shape=jax.ShapeDtypeStruct((bsz * ht, LANES), jnp.int32),
        compiler_params=_cparams(("arbitrary", "arbitrary")),
        name="moba_select",
    )(page_table, q, *([pool_kt] * npg))


def _moba_gather_body(pt_ref, sel_ref, qt_ref, knt_ref, vnt_ref, *refs, ppb):
    b = pl.program_id(0)
    h = pl.program_id(1)
    t_new = qt_ref.shape[1]
    n_chunk = t_new * B_TOPK * ppb
    kch = refs[:n_chunk]
    vch = refs[n_chunk:2 * n_chunk]
    o_ref = refs[2 * n_chunk]
    qt = qt_ref[...] * QK_SCALE
    knt = knt_ref[...]
    vnt = vnt_ref[...]
    lane_n = _iota((1, t_new), 1)
    lane_o = _iota((HEAD_DIM, t_new), 1)
    out = jnp.zeros((HEAD_DIM, t_new), F32)
    for t in range(t_new):
        qcol = qt[:, t:t + 1]
        s_own = jnp.where(lane_n <= t, jnp.sum(qcol * knt, axis=0, keepdims=True), NEG)
        m = jnp.max(s_own, axis=1, keepdims=True)
        scores = []
        for r in range(B_TOPK):
            picked = sel_ref[b, (h * t_new + t) * B_TOPK + r] >= 0
            for pg in range(ppb):
                kc = kch[(t * B_TOPK + r) * ppb + pg][...]
                s = jnp.where(picked, jnp.sum(qcol * kc, axis=0, keepdims=True), NEG)
                m = jnp.maximum(m, jnp.max(s, axis=1, keepdims=True))
                scores.append(s)
        p_own = jnp.exp(s_own - m)
        l = jnp.sum(p_own, axis=1, keepdims=True)
        acc = jnp.sum(p_own * vnt, axis=1, keepdims=True)
        accv = jnp.zeros((HEAD_DIM, LANES), F32)
        for c, s in enumerate(scores):
            p = jnp.exp(s - m)
            l = l + jnp.sum(p, axis=1, keepdims=True)
            accv = accv + p * vch[t * B_TOPK * ppb + c][...]
        acc = acc + jnp.sum(accv, axis=1, keepdims=True)
        out = jnp.where(lane_o == t, acc / l, out)
    o_ref[...] = out


def _moba_gather(qt, knt, vnt, pool_kt, pool_vt, page_table, sel):
    bsz, width, t_new = qt.shape
    _, n_heads, hd, page = pool_kt.shape
    ppb = B_BLOCK // page
    new_spec = pl.BlockSpec((None, hd, t_new), lambda b, h, pt, sl: (b, h, 0))

    def chunk_spec(t, r, pg):
        def imap(b, h, pages, sl):
            return (pages[b, ((h * t_new + t) * B_TOPK + r) * ppb + pg], h, 0, 0)
        return pl.BlockSpec((None, None, hd, page), imap)

    chunk_specs = [chunk_spec(t, r, pg) for t in range(t_new) for r in range(B_TOPK) for pg in range(ppb)]
    n_chunk = len(chunk_specs)
    return pl.pallas_call(
        functools.partial(_moba_gather_body, ppb=ppb),
        grid_spec=pltpu.PrefetchScalarGridSpec(
            num_scalar_prefetch=2,
            grid=(bsz, n_heads),
            in_specs=[new_spec, new_spec, new_spec] + chunk_specs + chunk_specs,
            out_specs=pl.BlockSpec((None, hd, t_new), lambda b, h, pt, sl: (b, h, 0)),
        ),
        out_shape=jax.ShapeDtypeStruct((bsz, width, t_new), F32),
        compiler_params=_cparams(("arbitrary", "arbitrary")),
        name="moba_gather",
    )(page_table, sel, qt, knt, vnt, *([pool_kt] * n_chunk), *([pool_vt] * n_chunk))


def _silu(x):
    return x / (1.0 + jnp.exp(-x))


def _ffn_even_body(x_ref, a_ref, b_ref, wo_ref, g_ref, wg_ref, wu_ref, wd_ref, o_ref, hn_ref, acc_ref):
    f = pl.program_id(1)
    aw = a_ref.shape[1]

    @pl.when(f == 0)
    def _():
        x1 = (x_ref[...]
              + jnp.dot(a_ref[...], wo_ref[:aw, :], preferred_element_type=F32)
              + jnp.dot(b_ref[...], wo_ref[aw:, :], preferred_element_type=F32))
        acc_ref[...] = x1
        hn_ref[...] = _rms(x1, g_ref[...]).astype(BF16)

    hn = hn_ref[...]
    gate = jnp.dot(hn, wg_ref[...], preferred_element_type=F32)
    up = jnp.dot(hn, wu_ref[...], preferred_element_type=F32)
    act = (_silu(gate) * up).astype(BF16)
    acc_ref[...] += jnp.dot(act, wd_ref[...], preferred_element_type=F32)

    @pl.when(f == pl.num_programs(1) - 1)
    def _():
        o_ref[...] = acc_ref[...]


def _ffn_even(x, a, b, wo16, g, wg16, wu16, wd16, *, tm=640, tf=1408):
    rows, dm = x.shape
    aw = a.shape[1]
    bw = b.shape[1]
    dff = wg16.shape[1]
    return pl.pallas_call(
        _ffn_even_body,
        grid=(rows // tm, dff // tf),
        in_specs=[
            pl.BlockSpec((tm, dm), lambda i, f: (i, 0)),
            pl.BlockSpec((tm, aw), lambda i, f: (i, 0)),
            pl.BlockSpec((tm, bw), lambda i, f: (i, 0)),
            pl.BlockSpec((aw + bw, dm), lambda i, f: (0, 0)),
            pl.BlockSpec((1, dm), lambda i, f: (0, 0)),
            pl.BlockSpec((dm, tf), lambda i, f: (0, f)),
            pl.BlockSpec((dm, tf), lambda i, f: (0, f)),
            pl.BlockSpec((tf, dm), lambda i, f: (f, 0)),
        ],
        out_specs=pl.BlockSpec((tm, dm), lambda i, f: (i, 0)),
        out_shape=jax.ShapeDtypeStruct((rows, dm), F32),
        scratch_shapes=[pltpu.VMEM((tm, dm), BF16), pltpu.VMEM((tm, dm), F32)],
        compiler_params=_cparams(("arbitrary", "arbitrary")),
        name="ffn_even",
    )(x, a, b, wo16, g, wg16, wu16, wd16)


def _odd_in_body(x_ref, g_ref, w_ref, wt_ref, q_ref, k_ref, v_ref, kt_ref, vt_ref, *, n_prompt_tiles):
    cw = q_ref.shape[1]
    h = _rms(x_ref[...], g_ref[...]).astype(BF16)
    y = jnp.dot(h, w_ref[...], preferred_element_type=F32)
    q_ref[...] = y[:, :cw]
    k_ref[...] = y[:, cw:2 * cw]
    v_ref[...] = y[:, 2 * cw:]
    _feature_major_kv(pl.program_id(0), n_prompt_tiles, wt_ref, h, kt_ref, vt_ref)


def _odd_in(x, bsz, t_len, g, w16, wt16, *, tm=256):
    rows, dm = x.shape
    cw = w16.shape[1] // 3
    npt = bsz * t_len // tm
    out = jax.ShapeDtypeStruct((rows, cw), F32)
    out_t = jax.ShapeDtypeStruct((bsz, cw, t_len), F32)
    spec = pl.BlockSpec((tm, cw), lambda i: (i, 0))
    fm_spec = _feature_major_spec(cw, tm, npt, t_len // tm)
    return pl.pallas_call(
        functools.partial(_odd_in_body, n_prompt_tiles=npt),
        grid=(rows // tm,),
        in_specs=[
            pl.BlockSpec((tm, dm), lambda i: (i, 0)),
            pl.BlockSpec((1, dm), lambda i: (0, 0)),
            pl.BlockSpec((dm, 3 * cw), lambda i: (0, 0)),
            pl.BlockSpec((2 * cw, dm), lambda i: (0, 0)),
        ],
        out_specs=[spec, spec, spec, fm_spec, fm_spec],
        out_shape=[out, out, out, out_t, out_t],
        compiler_params=_cparams(("arbitrary",)),
        name="odd_in",
    )(x, g, w16, wt16)


def _dilated_prompt_body(q_ref, k_ref, v_ref, o_ref, *state, patterns):
    t_len = q_ref.shape[0]
    blk = 128
    lo = _iota((blk, LANES), 1) < HEAD_DIM
    n_br = len(patterns)
    for bi, (window, dil) in enumerate(patterns):
        assert window // dil == blk
        nblk = t_len // (dil * blk)
        has_prev = nblk > 1
        n_keys = 2 * blk if has_prev else blk
        qrow = _iota((2 * blk, n_keys), 0) % blk
        kcol = _iota((2 * blk, n_keys), 1)
        if has_prev:
            in_prev = kcol < blk
            window_ok = (in_prev & (kcol >= qrow)) | ((kcol >= blk) & ((kcol - blk) <= qrow))
        else:
            window_ok = kcol <= qrow
        m_s, l_s, acc_s = state[3 * bi:3 * bi + 3]

        def body(it, carry, dil=dil, nblk=nblk, has_prev=has_prev, window_ok=window_ok,
                 m_s=m_s, l_s=l_s, acc_s=acc_s):
            cls = it // nblk
            ib = it % nblk

            def rows_at(block):
                start = cls + dil * blk * block
                if dil == 1:
                    return pl.ds(pl.multiple_of(start, blk), blk)
                return pl.ds(start, blk, stride=dil)

            rows = rows_at(ib)
            q = q_ref[rows, :] * QK_SCALE
            qst = jnp.concatenate([jnp.where(lo, q, 0.0), jnp.where(lo, 0.0, q)], axis=0).astype(BF16)
            kk = k_ref[rows, :].astype(BF16)
            vv = v_ref[rows, :].astype(BF16)
            ok = window_ok
            if has_prev:
                prows = rows_at(jnp.maximum(ib - 1, 0))
                kk = jnp.concatenate([k_ref[prows, :].astype(BF16), kk], axis=0)
                vv = jnp.concatenate([v_ref[prows, :].astype(BF16), vv], axis=0)
                ok = (window_ok & (kcol >= blk)) | ((window_ok & in_prev) & (ib > 0))
            s = jnp.where(ok, lax.dot_general(qst, kk, NT, preferred_element_type=F32), NEG)
            m = jnp.max(s, axis=1, keepdims=True)
            p = jnp.exp(s - m)
            l = jnp.sum(p, axis=1, keepdims=True)
            pv = jnp.dot(p.astype(BF16), vv, preferred_element_type=F32)
            m_s[rows, :] = jnp.where(lo, m[:blk], m[blk:])
            l_s[rows, :] = jnp.where(lo, l[:blk], l[blk:])
            acc_s[rows, :] = jnp.where(lo, pv[:blk], pv[blk:])
            return carry

        lax.fori_loop(0, dil * nblk, body, 0, unroll=2)

    chunk = 256

    def merge(c, carry):
        rows = pl.ds(pl.multiple_of(c * chunk, chunk), chunk)
        ms = [state[3 * r][rows, :] for r in range(n_br)]
        m = functools.reduce(jnp.maximum, ms)
        num = jnp.zeros((chunk, LANES), F32)
        den = jnp.zeros((chunk, LANES), F32)
        for r in range(n_br):
            w = jnp.exp(ms[r] - m)
            num = num + w * state[3 * r + 2][rows, :]
            den = den + w * state[3 * r + 1][rows, :]
        o_ref[rows, :] = (num / den).astype(BF16)
        return carry

    lax.fori_loop(0, t_len // chunk, merge, 0)


def _dilated_prompt(q, k, v, bsz, t_len):
    width = q.shape[1]
    spec = pl.BlockSpec((t_len, LANES), lambda b, j: (b, j))
    return pl.pallas_call(
        functools.partial(_dilated_prompt_body, patterns=C_PATTERNS),
        grid=(bsz, width // LANES),
        in_specs=[spec, spec, spec],
        out_specs=spec,
        out_shape=jax.ShapeDtypeStruct((bsz * t_len, width), BF16),
        scratch_shapes=[pltpu.VMEM((t_len, LANES), F32)] * (3 * len(C_PATTERNS)),
        compiler_params=_cparams(("arbitrary", "arbitrary")),
        name="dilated_prompt",
    )(q, k, v)


def _dilated_sample_body(q_ref, kn_ref, vn_ref, kc_ref, vc_ref, o_ref, *, patterns):
    t_new = q_ref.shape[0]
    w_len = kc_ref.shape[1]
    nrow = 2 * t_new
    lane = _iota((t_new, LANES), 1)
    q = q_ref[...] * QK_SCALE
    qst = jnp.concatenate([jnp.where(lane < HEAD_DIM, q, 0.0), jnp.where(lane >= HEAD_DIM, q, 0.0)], axis=0)
    qst = qst.astype(BF16)

    def mult(delta):
        cnt = jnp.zeros(delta.shape, F32)
        for window, dil in patterns:
            hit = (delta >= 0) & (delta <= window) & ((delta % dil) == 0)
            cnt = cnt + hit.astype(F32)
        return cnt

    t_c = _iota((nrow, w_len), 0) % t_new
    w_c = mult(w_len + t_c - _iota((nrow, w_len), 1))
    t_n = _iota((nrow, LANES), 0) % t_new
    c_n = _iota((nrow, LANES), 1)
    w_n = jnp.where(c_n < t_new, mult(t_n - c_n), 0.0)

    pad = jnp.zeros((LANES - t_new, LANES), F32)
    kn = jnp.concatenate([kn_ref[...], pad], axis=0).astype(BF16)
    vn = jnp.concatenate([vn_ref[...], pad], axis=0).astype(BF16)
    s_c = jnp.dot(qst, kc_ref[...].astype(BF16), preferred_element_type=F32)
    s_n = lax.dot_general(qst, kn, NT, preferred_element_type=F32)
    s_c = jnp.where(w_c > 0, s_c, NEG)
    s_n = jnp.where(w_n > 0, s_n, NEG)
    m = jnp.maximum(jnp.max(s_c, axis=1, keepdims=True), jnp.max(s_n, axis=1, keepdims=True))
    p_c = w_c * jnp.exp(s_c - m)
    p_n = w_n * jnp.exp(s_n - m)
    l = jnp.sum(p_c, axis=1, keepdims=True) + jnp.sum(p_n, axis=1, keepdims=True)
    acc = (lax.dot_general(p_c.astype(BF16), vc_ref[...].astype(BF16), NT, preferred_element_type=F32)
           + jnp.dot(p_n.astype(BF16), vn, preferred_element_type=F32))
    out = acc / l
    o_ref[...] = jnp.where(lane < HEAD_DIM, out[:t_new], out[t_new:])


def _dilated_sample(q, k, v, row_block0, cache_kt, cache_vt):
    bsz, width, w_len = cache_kt.shape
    t_new = 8
    new_spec = pl.BlockSpec((t_new, LANES), lambda b, j: (row_block0 + b, j))
    cache_spec = pl.BlockSpec((None, LANES, w_len), lambda b, j: (b, j, 0))
    return pl.pallas_call(
        functools.partial(_dilated_sample_body, patterns=C_PATTERNS),
        grid=(bsz, width // LANES),
        in_specs=[new_spec, new_spec, new_spec, cache_spec, cache_spec],
        out_specs=pl.BlockSpec((t_new, LANES), lambda b, j: (b, j)),
        out_shape=jax.ShapeDtypeStruct((bsz * t_new, width), F32),
        compiler_params=_cparams(("arbitrary", "arbitrary")),
        name="dilated_sample",
    )(q, k, v, cache_kt, cache_vt)


def _moe_route_body(x_ref, c_ref, wo_ref, g_ref, r_ref,
                    x3_ref, hn_ref, comb_ref, pos_ref, post_ref, cnt_ref):
    tb = x_ref.shape[0]
    x3 = x_ref[...] + jnp.dot(c_ref[...], wo_ref[...], preferred_element_type=F32)
    x3_ref[...] = x3
    hf = _rms(x3, g_ref[...])
    hn_ref[...] = hf.astype(BF16)
    lane = _iota((tb, LANES), 1)
    logits = jnp.dot(hf, r_ref[...], precision=HIGHEST, preferred_element_type=F32)
    lg = jnp.where(lane < N_EXPERTS, logits, -jnp.inf)
    m1 = jnp.max(lg, axis=1, keepdims=True)
    p1 = lane == jnp.min(jnp.where(lg == m1, lane, LANES), axis=1, keepdims=True)
    lg2 = jnp.where(p1, -jnp.inf, lg)
    m2 = jnp.max(lg2, axis=1, keepdims=True)
    p2 = lane == jnp.min(jnp.where(lg2 == m2, lane, LANES), axis=1, keepdims=True)
    e2 = jnp.exp(m2 - m1)
    den = 1.0 + e2
    comb_ref[...] = jnp.where(p1, 1.0 / den, 0.0) + jnp.where(p2, e2 / den, 0.0)
    routed = p1 | p2
    ind = jnp.where(routed, 1.0, 0.0)
    before = jnp.where(_iota((tb, tb), 1) < _iota((tb, tb), 0), 1.0, 0.0).astype(BF16)
    slot = jnp.where(routed, jnp.dot(before, ind.astype(BF16), preferred_element_type=F32), -1.0)
    pos_ref[...] = slot
    eye = jnp.where(_iota((LANES, LANES), 0) == _iota((LANES, LANES), 1), 1.0, 0.0)
    post_ref[...] = lax.dot_general(eye, slot, NT, precision=HIGHEST, preferred_element_type=F32)
    cnt = jnp.sum(ind, axis=0, keepdims=True)
    cnt_ref[...] = jnp.broadcast_to(cnt, cnt_ref.shape).astype(jnp.int32)


def _moe_route(x, c, wo16, g, router_pad, *, tb):
    rows, dm = x.shape
    nblk = rows // tb
    row_spec = lambda width: pl.BlockSpec((tb, width), lambda i: (i, 0))
    return pl.pallas_call(
        _moe_route_body,
        grid=(nblk,),
        in_specs=[
            row_spec(dm), row_spec(dm),
            pl.BlockSpec((dm, dm), lambda i: (0, 0)),
            pl.BlockSpec((1, dm), lambda i: (0, 0)),
            pl.BlockSpec((dm, LANES), lambda i: (0, 0)),
        ],
        out_specs=[
            row_spec(dm), row_spec(dm), row_spec(LANES), row_spec(LANES),
            pl.BlockSpec((None, LANES, tb), lambda i: (i, 0, 0)),
            pl.BlockSpec((None, 8, LANES), lambda i: (i, 0, 0)),
        ],
        out_shape=[
            jax.ShapeDtypeStruct((rows, dm), F32),
            jax.ShapeDtypeStruct((rows, dm), BF16),
            jax.ShapeDtypeStruct((rows, LANES), F32),
            jax.ShapeDtypeStruct((rows, LANES), F32),
            jax.ShapeDtypeStruct((nblk, LANES, tb), F32),
            jax.ShapeDtypeStruct((nblk, 8, LANES), jnp.int32),
        ],
        compiler_params=_cparams(("arbitrary",)),
        name="moe_route",
    )(x, c, wo16, g, router_pad)


def _moe_expert_body(cnt_ref, acc_ref, hn_ref, comb_ref, pos_ref, post_ref, gf_ref,
                     wg_ref, wu_ref, wd_ref, o_ref, *, cap):
    e = pl.program_id(0)
    i = pl.program_id(1)
    tb = hn_ref.shape[0]
    n_sub = (cnt_ref[i, e] + cap - 1) // cap
    lane = _iota((tb, LANES), 1)
    w_col = jnp.sum(jnp.where(lane == e, comb_ref[...], 0.0), axis=1, keepdims=True)
    slot_col = jnp.sum(jnp.where(lane == e, pos_ref[...], 0.0), axis=1, keepdims=True)
    slot_row = post_ref[pl.ds(e, 1), :]
    sub_of_row = _iota((cap, tb), 0).astype(F32)
    sub_of_col = _iota((tb, cap), 1).astype(F32)
    o_ref[...] = acc_ref[...]

    def body(s, carry):
        base = (s * cap).astype(F32)
        gather = jnp.where(slot_row - base == sub_of_row, 1.0, 0.0).astype(BF16)
        xe = jnp.dot(gather, hn_ref[...], preferred_element_type=F32).astype(BF16)
        gate = jnp.dot(xe, wg_ref[...], preferred_element_type=F32)
        up = jnp.dot(xe, wu_ref[...], preferred_element_type=F32)
        act = (_silu(gate) * up).astype(BF16)
        y = jnp.dot(act, wd_ref[...], preferred_element_type=F32).astype(BF16)
        scatter = jnp.where(slot_col - base == sub_of_col, 1.0, 0.0).astype(BF16)
        o_ref[...] += w_col * jnp.dot(scatter, y, preferred_element_type=F32)
        return carry

    lax.fori_loop(0, n_sub, body, 0)

    @pl.when(e == pl.num_programs(0) - 1)
    def _():
        o_ref[...] = _rms(o_ref[...], gf_ref[...])


def _moe_experts(counts, x3, hn, comb, pos, post, g_final, wg16, wu16, wd16, *, tb, cap):
    rows, dm = x3.shape
    n_exp, _, dff = wg16.shape
    once = pl.Buffered(1)
    return pl.pallas_call(
        functools.partial(_moe_expert_body, cap=cap),
        grid_spec=pltpu.PrefetchScalarGridSpec(
            num_scalar_prefetch=1,
            grid=(n_exp, rows // tb),
            in_specs=[
                pl.BlockSpec((tb, dm), lambda e, i, cnt: (i, 0)),
                pl.BlockSpec((tb, dm), lambda e, i, cnt: (i, 0)),
                pl.BlockSpec((tb, LANES), lambda e, i, cnt: (i, 0)),
                pl.BlockSpec((tb, LANES), lambda e, i, cnt: (i, 0)),
                pl.BlockSpec((None, LANES, tb), lambda e, i, cnt: (i, 0, 0)),
                pl.BlockSpec((1, dm), lambda e, i, cnt: (0, 0)),
                pl.BlockSpec((None, dm, dff), lambda e, i, cnt: (e, 0, 0), pipeline_mode=once),
                pl.BlockSpec((None, dm, dff), lambda e, i, cnt: (e, 0, 0), pipeline_mode=once),
                pl.BlockSpec((None, dff, dm), lambda e, i, cnt: (e, 0, 0), pipeline_mode=once),
            ],
            out_specs=pl.BlockSpec((tb, dm), lambda e, i, cnt: (i, 0)),
        ),
        out_shape=jax.ShapeDtypeStruct((rows, dm), F32),
        input_output_aliases={1: 0},
        compiler_params=_cparams(("arbitrary", "arbitrary")),
        name="moe_experts",
    )(counts, x3, hn, comb, pos, post, g_final, wg16, wu16, wd16)


def _moe(x, c, wo16, g, router_pad, g_final, wg16, wu16, wd16, *, tb=832, cap=256):
    x3, hn, comb, pos, post, cnt = _moe_route(x, c, wo16, g, router_pad, tb=tb)
    counts = cnt[:, 0, :N_EXPERTS]
    return _moe_experts(counts, x3, hn, comb, pos, post, g_final, wg16, wu16, wd16, tb=tb, cap=cap)


def kernel(x_prompt, x_sample, cache_b_k, cache_b_v, cache_c_k, cache_c_v, page_table, norm_mix, norm_ffn,
           norm_final, w_in_even, w_out_even, a_ln_g, a_ln_b, a_ws, a_bs, ffn_gate, ffn_up, ffn_down,
           w_in_odd, w_out_odd, router, moe_gate, moe_up, moe_down):
    bsz, t_len, dm = x_prompt.shape
    dbsz, t_new, _ = x_sample.shape
    assert norm_mix.shape[0] == 2 and w_in_even.shape[0] == 1 and w_in_odd.shape[0] == 1
    assert cache_b_k.shape[0] == 1 and cache_c_k.shape[0] == 1
    assert t_new == 8 and A_CHUNK % t_new == 0
    n_p = bsz * t_len
    n_s = dbsz * t_new
    aw = a_ln_g.shape[-1]
    n_groups = aw // HEAD_DIM
    seq_per_chunk = A_CHUNK // t_new

    x = jnp.concatenate([x_prompt.reshape(n_p, dm), x_sample.reshape(n_s, dm)], axis=0)

    ws = a_ws[0]
    ws_sample = jnp.tile(ws[:, :t_new, :t_new], (1, seq_per_chunk, seq_per_chunk))
    ws2 = jnp.stack([ws, ws_sample])
    bias_p = jnp.repeat(a_bs[0].T, HEAD_DIM, axis=1)
    bias_s = jnp.repeat(jnp.tile(a_bs[0][:, :t_new].T, (seq_per_chunk, 1)), HEAD_DIM, axis=1)
    bias2 = jnp.stack([bias_p, bias_s])
    row = lambda vec: vec.reshape(1, -1)
    router_pad = jnp.pad(router[0], ((0, 0), (0, LANES - router.shape[-1])))

    w_even = w_in_even[0].astype(BF16)
    kv_even_t = w_even[:, w_even.shape[1] - 2 * (B_HEADS * HEAD_DIM):].T
    a_out, v_a, q_b, k_b, v_b, kt_b, vt_b, v_a_last = _even_in(
        x, bsz, t_len, row(norm_mix[0]), w_even, kv_even_t, row(a_ln_g[0]), row(a_ln_b[0]), ws2, bias2)
    b_prompt = _moba_prompt(q_b, k_b, v_b, bsz, t_len)
    page = cache_b_k.shape[2]
    bw = q_b.shape[1]
    feature_major = lambda c: jnp.transpose(c, (0, 1, 3, 4, 2))
    pool_kt = feature_major(cache_b_k).reshape(-1, B_HEADS, HEAD_DIM, page)
    pool_vt = feature_major(cache_b_v).reshape(-1, B_HEADS, HEAD_DIM, page)
    sel = _moba_select(q_b, n_p // t_new, pool_kt.reshape(-1, bw, page), page_table)
    sel = sel.reshape(dbsz, B_HEADS * t_new, LANES)[:, :, :B_TOPK].reshape(dbsz, -1)
    new_t = lambda a: jnp.transpose(a[n_p:].reshape(dbsz, t_new, -1), (0, 2, 1))
    ppb = B_BLOCK // page
    logical = (jnp.maximum(sel, 0)[:, :, None] * ppb + jnp.arange(ppb, dtype=jnp.int32)).reshape(dbsz, -1)
    sel_pages = jnp.take_along_axis(page_table, logical, axis=1)
    b_sample_t = _moba_gather(new_t(q_b), new_t(k_b), new_t(v_b), pool_kt, pool_vt, sel_pages, sel)
    b_sample = jnp.transpose(b_sample_t, (0, 2, 1)).reshape(n_s, bw)
    b_out = jnp.concatenate([b_prompt, b_sample.astype(BF16)], axis=0)
    x2 = _ffn_even(x, a_out, b_out, w_out_even[0].astype(BF16), row(norm_ffn[0]),
                   ffn_gate[0].astype(BF16), ffn_up[0].astype(BF16), ffn_down[0].astype(BF16))

    w_odd = w_in_odd[0].astype(BF16)
    q_c, k_c, v_c, kt_c, vt_c = _odd_in(x2, bsz, t_len, row(norm_mix[1]), w_odd, w_odd[:, w_odd.shape[1] // 3:].T)
    c_prompt = _dilated_prompt(q_c, k_c, v_c, bsz, t_len)
    cw = q_c.shape[1]
    w_len = cache_c_k.shape[2]
    c_sample = _dilated_sample(q_c, k_c, v_c, n_p // t_new,
                               feature_major(cache_c_k).reshape(-1, cw, w_len),
                               feature_major(cache_c_v).reshape(-1, cw, w_len))
    c_out = jnp.concatenate([c_prompt, c_sample.astype(BF16)], axis=0)
    y = _moe(x2, c_out, w_out_odd[0].astype(BF16), row(norm_ffn[1]), router_pad, row(norm_final),
             moe_gate[0].astype(BF16), moe_up[0].astype(BF16), moe_down[0].astype(BF16))

    bh = (B_HEADS, HEAD_DIM)
    ch = (cw // HEAD_DIM, HEAD_DIM)
    assert t_len % 256 == 0
    c_keep_p = min(C_PATTERNS[-1][0], t_len)
    leaf = lambda a_t, hd: jnp.transpose(a_t.reshape(1, bsz, *hd, a_t.shape[-1]), (0, 1, 4, 2, 3))
    return (
        y[:n_p].reshape(bsz, t_len, dm),
        y[n_p:].reshape(dbsz, t_new, dm),
        v_a_last.reshape(1, bsz, A_CHUNK, aw),
        v_a[n_p:].reshape(1, dbsz, t_new, aw),
        leaf(kt_b, bh),
        leaf(vt_b, bh),
        k_b[n_p:].reshape(1, dbsz, t_new, *bh),
        v_b[n_p:].reshape(1, dbsz, t_new, *bh),
        leaf(kt_c[:, :, t_len - c_keep_p:], ch),
        leaf(vt_c[:, :, t_len - c_keep_p:], ch),
        k_c[n_p:].reshape(1, dbsz, t_new, *ch),
        v_c[n_p:].reshape(1, dbsz, t_new, *ch),
    )
```

```python
import functools

import jax
import jax.numpy as jnp
from jax import lax
from jax.experimental import pallas as pl
from jax.experimental.pallas import tpu as pltpu

F32 = jnp.float32
BF16 = jnp.bfloat16
HIGHEST = lax.Precision.HIGHEST

HEAD_DIM = 64
LANES = 128
A_CHUNK = 128
B_HEADS = 8
B_BLOCK = 256
B_TOPK = 3
C_PATTERNS = ((128, 1), (512, 4), (2048, 16))
N_EXPERTS = 8
RMS_EPS = 1e-6
LN_EPS = 1e-5
NEG = -1e30
QK_SCALE = HEAD_DIM ** -0.5
VMEM_LIMIT = 56 * 1024 * 1024

NT = (((1,), (1,)), ((), ()))


def _cparams(sem):
    return pltpu.CompilerParams(dimension_semantics=sem, vmem_limit_bytes=VMEM_LIMIT)


def _rms(x, g):
    return x * lax.rsqrt(jnp.mean(x * x, axis=-1, keepdims=True) + RMS_EPS) * g


def _gelu(x):
    return 0.5 * x * (1.0 + lax.erf(x * 0.7071067811865476))


def _iota(shape, dim):
    return lax.broadcasted_iota(jnp.int32, shape, dim)


def _feature_major_kv(i, n_prompt_tiles, wt_ref, h, kt_ref, vt_ref):
    width = kt_ref.shape[0]
    kvt = lax.dot_general(wt_ref[...], h, NT, preferred_element_type=F32)

    @pl.when(i < n_prompt_tiles)
    def _():
        kt_ref[...] = kvt[:width]
        vt_ref[...] = kvt[width:]


def _feature_major_spec(width, tm, n_prompt_tiles, tiles_per_seq):
    def imap(i):
        ic = jnp.minimum(i, n_prompt_tiles - 1)
        return (ic // tiles_per_seq, 0, ic % tiles_per_seq)
    return pl.BlockSpec((None, width, tm), imap)


def _even_in_body(x_ref, g_ref, w_ref, wt_ref, lng_ref, lnb_ref, ws_ref, bias_ref,
                  a_ref, v_ref, q_ref, k_ref, vv_ref, kt_ref, vt_ref, vlast_ref,
                  *, n_prompt_tiles, tiles_per_seq, aw, bw):
    i = pl.program_id(0)
    tm = x_ref.shape[0]
    h = _rms(x_ref[...], g_ref[...]).astype(BF16)
    y = jnp.dot(h, w_ref[...], preferred_element_type=F32)
    _feature_major_kv(i, n_prompt_tiles, wt_ref, h, kt_ref, vt_ref)
    u = _gelu(y[:, :aw])
    v = _gelu(y[:, aw:2 * aw])
    mu = jnp.mean(v, axis=-1, keepdims=True)
    vc = v - mu
    var = jnp.mean(vc * vc, axis=-1, keepdims=True)
    v = vc * lax.rsqrt(var + LN_EPS) * lng_ref[...] + lnb_ref[...]
    v_ref[...] = v

    @pl.when((i < n_prompt_tiles) & (i % tiles_per_seq == tiles_per_seq - 1))
    def _():
        vlast_ref[...] = v[tm - A_CHUNK:, :]
    q_ref[...] = y[:, 2 * aw:2 * aw + bw]
    k_ref[...] = y[:, 2 * aw + bw:2 * aw + 2 * bw]
    vv_ref[...] = y[:, 2 * aw + 2 * bw:]

    row = _iota((A_CHUNK, A_CHUNK), 0)
    col = _iota((A_CHUNK, A_CHUNK), 1)
    is_prompt = i < n_prompt_tiles
    ok = (col <= row) & (((row // 8) == (col // 8)) | is_prompt)
    lo = _iota((A_CHUNK, LANES), 1) < HEAD_DIM
    v16 = v.astype(BF16)
    n_groups = aw // HEAD_DIM
    wm = [jnp.where(ok, ws_ref[g], 0.0).astype(BF16) for g in range(n_groups)]
    for c in range(tm // A_CHUNK):
        rs = slice(c * A_CHUNK, (c + 1) * A_CHUNK)
        for j in range(aw // LANES):
            cs = slice(j * LANES, (j + 1) * LANES)
            vj = v16[rs, cs]
            ma = jnp.dot(wm[2 * j], vj, preferred_element_type=F32)
            mb = jnp.dot(wm[2 * j + 1], vj, preferred_element_type=F32)
            mixed = jnp.where(lo, ma, mb) + bias_ref[:, cs]
            a_ref[rs, cs] = (u[rs, cs] * mixed).astype(BF16)


def _even_in(x, bsz, t_len, g, w16, wt16, lng, lnb, ws2, bias2, *, tm=256):
    rows, dm = x.shape
    aw = lng.shape[-1]
    n_in = w16.shape[1]
    bw = (n_in - 2 * aw) // 3
    n_groups = aw // HEAD_DIM
    npt = bsz * t_len // tm
    sel = lambda i: jnp.minimum(i // npt, 1)
    row_spec = lambda width: pl.BlockSpec((tm, width), lambda i: (i, 0))
    tps = t_len // tm
    fm_spec = _feature_major_spec(bw, tm, npt, tps)
    return pl.pallas_call(
        functools.partial(_even_in_body, n_prompt_tiles=npt, tiles_per_seq=tps, aw=aw, bw=bw),
        grid=(rows // tm,),
        in_specs=[
            row_spec(dm),
            pl.BlockSpec((1, dm), lambda i: (0, 0)),
            pl.BlockSpec((dm, n_in), lambda i: (0, 0)),
            pl.BlockSpec((2 * bw, dm), lambda i: (0, 0)),
            pl.BlockSpec((1, aw), lambda i: (0, 0)),
            pl.BlockSpec((1, aw), lambda i: (0, 0)),
            pl.BlockSpec((None, n_groups, A_CHUNK, A_CHUNK), lambda i: (sel(i), 0, 0, 0)),
            pl.BlockSpec((None, A_CHUNK, aw), lambda i: (sel(i), 0, 0)),
        ],
        out_specs=[row_spec(aw), row_spec(aw), row_spec(bw), row_spec(bw), row_spec(bw), fm_spec, fm_spec,
                   pl.BlockSpec((None, A_CHUNK, aw), lambda i: (jnp.minimum(i, npt - 1) // tps, 0, 0))],
        out_shape=[
            jax.ShapeDtypeStruct((rows, aw), BF16),
            jax.ShapeDtypeStruct((rows, aw), F32),
            jax.ShapeDtypeStruct((rows, bw), F32),
            jax.ShapeDtypeStruct((rows, bw), F32),
            jax.ShapeDtypeStruct((rows, bw), F32),
            jax.ShapeDtypeStruct((bsz, bw, t_len), F32),
            jax.ShapeDtypeStruct((bsz, bw, t_len), F32),
            jax.ShapeDtypeStruct((bsz, A_CHUNK, aw), F32),
        ],
        compiler_params=_cparams(("arbitrary",)),
        name="even_in",
    )(x, g, w16, wt16, lng, lnb, ws2, bias2)


def _moba_prompt_body(q_ref, k_ref, v_ref, o_ref, k16, v16, mt_ref, s_scr, p_scr, *, nb):
    i = pl.program_id(1)
    t_len, width = k_ref.shape
    n_gate = B_HEADS * nb

    @pl.when(i == 0)
    def _():
        k = k_ref[...]
        k16[...] = k.astype(BF16)
        v16[...] = v_ref[...].astype(BF16)
        r = _iota((LANES, t_len), 0)
        key = _iota((LANES, t_len), 1)
        avg = jnp.where(((key // B_BLOCK) == (r % nb)) & (r < n_gate), 1.0 / B_BLOCK, 0.0).astype(F32)
        mt = jnp.dot(avg, k, precision=HIGHEST, preferred_element_type=F32)
        rr = _iota((LANES, width), 0)
        cc = _iota((LANES, width), 1)
        mt_ref[...] = jnp.where(((rr // nb) == (cc // HEAD_DIM)) & (rr < n_gate), mt, 0.0)

    q = q_ref[...]
    tq = q.shape[0]
    gate = lax.dot_general(q, mt_ref[...], NT, precision=HIGHEST, preferred_element_type=F32)
    lane = _iota((tq, LANES), 1)
    n_of = lane % nb
    valid = (n_of < i) & (lane < n_gate)
    gm = jnp.where(valid, gate, -jnp.inf)
    rank = jnp.zeros((tq, LANES), jnp.int32)
    for s in range(1, nb):
        fwd = pltpu.roll(gm, LANES - s, 1)
        bwd = pltpu.roll(gm, nb - s, 1)
        wrap = (n_of + s) >= nb
        other = jnp.where(wrap, bwd, fwd)
        m_idx = jnp.where(wrap, n_of + s - nb, n_of + s)
        beats = (other > gm) | ((other == gm) & (m_idx < n_of))
        rank = rank + beats.astype(jnp.int32)
    selbias = jnp.where(valid & (rank < B_TOPK), 0.0, NEG)

    row2 = _iota((2 * tq, B_BLOCK), 0) % tq
    col2 = _iota((2 * tq, B_BLOCK), 1)
    causal_bias = jnp.where(col2 <= row2, 0.0, NEG)
    lo = lane < HEAD_DIM
    own = pl.multiple_of(i * B_BLOCK, B_BLOCK)
    half = B_BLOCK // 2

    key_lane = _iota((B_BLOCK, LANES), 1)

    group = s_scr.shape[0]

    def zero_body(n, carry):
        for g in range(group):
            p_scr[g, n] = jnp.zeros((2 * tq, B_BLOCK), BF16)
        return carry

    lax.fori_loop(i + 1, nb, zero_body, 0)
    for j0 in range(0, width // LANES, group):
        cols = [slice((j0 + g) * LANES, (j0 + g + 1) * LANES) for g in range(group)]
        q_aug, m_tiles = [], []
        for g in range(group):
            j = j0 + g
            qj = q[:, cols[g]] * QK_SCALE
            qst = jnp.concatenate([jnp.where(lo, qj, 0.0), jnp.where(lo, 0.0, qj)], axis=0).astype(BF16)
            s = (lax.dot_general(qst, k16[pl.ds(own, B_BLOCK), cols[g]], NT, preferred_element_type=F32)
                 + causal_bias)
            s_scr[g, i] = s
            m_tiles.append(jnp.maximum(s[:, :half], s[:, half:]))
            sb = []
            for a in range(2):
                off = (2 * j + a) * nb
                moved = selbias if off == 0 else pltpu.roll(selbias, LANES - off, 1)
                sb.append(jnp.where(lane < nb, moved, 0.0))
            q_aug.append(jnp.concatenate([qst, jnp.concatenate(sb, axis=0).astype(BF16)], axis=1))

        def score_body(n, m_tiles, q_aug=q_aug, cols=cols):
            start = pl.multiple_of(n * B_BLOCK, B_BLOCK)
            onehot = jnp.where(key_lane == n, 1.0, 0.0).astype(BF16)
            out = []
            for g in range(group):
                k_aug = jnp.concatenate([k16[pl.ds(start, B_BLOCK), cols[g]], onehot], axis=1)
                s = lax.dot_general(q_aug[g], k_aug, NT, preferred_element_type=F32)
                s_scr[g, n] = s
                out.append(jnp.maximum(m_tiles[g], jnp.maximum(s[:, :half], s[:, half:])))
            return tuple(out)

        m_tiles = lax.fori_loop(0, i, score_body, tuple(m_tiles))
        ms = [jnp.max(m_tiles[g], axis=1, keepdims=True) for g in range(group)]

        def prob_body(n, l_tiles, ms=ms):
            out = []
            for g in range(group):
                p = jnp.exp(s_scr[g, n] - ms[g])
                p_scr[g, n] = p.astype(BF16)
                out.append(l_tiles[g] + p[:, :half] + p[:, half:])
            return tuple(out)

        l_tiles = lax.fori_loop(0, i + 1, prob_body,
                                tuple(jnp.zeros((2 * tq, half), F32) for _ in range(group)))
        for g in range(group):
            l = jnp.sum(l_tiles[g], axis=1, keepdims=True)
            acc = jnp.dot(p_scr[g, 0], v16[0:B_BLOCK, cols[g]], preferred_element_type=F32)
            for n in range(1, nb):
                acc = acc + jnp.dot(p_scr[g, n], v16[n * B_BLOCK:(n + 1) * B_BLOCK, cols[g]],
                                    preferred_element_type=F32)
            o = acc / l
            o_ref[:, cols[g]] = jnp.where(lo, o[:tq], o[tq:]).astype(BF16)


def _moba_prompt(q, k, v, bsz, t_len):
    width = q.shape[1]
    nb = t_len // B_BLOCK
    return pl.pallas_call(
        functools.partial(_moba_prompt_body, nb=nb),
        grid=(bsz, nb),
        in_specs=[
            pl.BlockSpec((B_BLOCK, width), lambda b, i: (b * nb + i, 0)),
            pl.BlockSpec((t_len, width), lambda b, i: (b, 0)),
            pl.BlockSpec((t_len, width), lambda b, i: (b, 0)),
        ],
        out_specs=pl.BlockSpec((B_BLOCK, width), lambda b, i: (b * nb + i, 0)),
        out_shape=jax.ShapeDtypeStruct((bsz * t_len, width), BF16),
        scratch_shapes=[
            pltpu.VMEM((t_len, width), BF16),
            pltpu.VMEM((t_len, width), BF16),
            pltpu.VMEM((LANES, width), F32),
            pltpu.VMEM((2, nb, 2 * B_BLOCK, B_BLOCK), F32),
            pltpu.VMEM((2, nb, 2 * B_BLOCK, B_BLOCK), BF16),
        ],
        compiler_params=_cparams(("arbitrary", "arbitrary")),
        name="moba_prompt",
    )(q, k, v)


def _moba_select_body(pt_ref, q_ref, *refs, npg, ppb, n_blocks):
    pages = refs[:npg]
    idx_ref = refs[npg]
    mt_ref = refs[npg + 1]
    s = pl.program_id(1)
    t_new, width = q_ref.shape
    bps = npg // ppb

    @pl.when(s == 0)
    def _():
        mt_ref[...] = jnp.zeros(mt_ref.shape, F32)

    lane_w = _iota((width, LANES), 1)
    cur = mt_ref[...]
    for blk in range(bps):
        tot = pages[blk * ppb][...]
        for r in range(1, ppb):
            tot = tot + pages[blk * ppb + r][...]
        col = jnp.sum(tot, axis=1, keepdims=True) * (1.0 / B_BLOCK)
        cur = jnp.where(lane_w == s * bps + blk, col, cur)
    mt_ref[...] = cur

    @pl.when(s == pl.num_programs(1) - 1)
    def _():
        ht = B_HEADS * t_new
        rr = _iota((ht, width), 0)
        cc = _iota((ht, width), 1)
        q = q_ref[...]
        qst = jnp.where((rr // t_new) == (cc // HEAD_DIM), jnp.concatenate([q] * B_HEADS, axis=0), 0.0)
        gate = jnp.dot(qst, cur, precision=HIGHEST, preferred_element_type=F32)
        lane = _iota((ht, LANES), 1)
        g = jnp.where(lane < n_blocks, gate, -jnp.inf)
        out = jnp.full((ht, LANES), -1, jnp.int32)
        for r in range(B_TOPK):
            mx = jnp.max(g, axis=1, keepdims=True)
            idx = jnp.min(jnp.where(g == mx, lane, LANES), axis=1, keepdims=True)
            out = jnp.where(lane == r, jnp.where(mx > -jnp.inf, idx, -1), out)
            g = jnp.where(lane == idx, -jnp.inf, g)
        idx_ref[...] = out


def _moba_select(q, row_block0, pool_kt, page_table, *, npg=16):
    bsz, n_pages = page_table.shape
    _, width, page = pool_kt.shape
    ppb = B_BLOCK // page
    t_new = 8
    ht = B_HEADS * t_new
    page_specs = [
        pl.BlockSpec((None, width, page), lambda b, s, pt, r=r: (pt[b, s * npg + r], 0, 0))
        for r in range(npg)
    ]
    return pl.pallas_call(
        functools.partial(_moba_select_body, npg=npg, ppb=ppb, n_blocks=n_pages // ppb),
        grid_spec=pltpu.PrefetchScalarGridSpec(
            num_scalar_prefetch=1,
            grid=(bsz, n_pages // npg),
            in_specs=[pl.BlockSpec((t_new, width), lambda b, s, pt: (row_block0 + b, 0))] + page_specs,
            out_specs=pl.BlockSpec((ht, LANES), lambda b, s, pt: (b, 0)),
            scratch_shapes=[pltpu.VMEM((width, LANES), F32)],
        ),
        out_shape=jax.ShapeDtypeStruct((bsz * ht, LANES), jnp.int32),
        compiler_params=_cparams(("arbitrary", "arbitrary")),
        name="moba_select",
    )(page_table, q, *([pool_kt] * npg))


def _moba_gather_body(pages_ref, sel_ref, qt_ref, knt_ref, vnt_ref, pool_k, pool_v, o_ref,
                      kbuf, vbuf, sem, *, ppb):
    b = pl.program_id(0)
    h = pl.program_id(1)
    n_heads = pl.num_programs(1)
    t_new = qt_ref.shape[1]
    n_chunk = t_new * B_TOPK * ppb
    step = b * n_heads + h
    slot = step % 2

    def chunk_copies(bb, hh, c, to_slot):
        page = pages_ref[bb, hh * n_chunk + c]
        return (pltpu.make_async_copy(pool_k.at[page, hh], kbuf.at[to_slot, c], sem.at[to_slot, 0]),
                pltpu.make_async_copy(pool_v.at[page, hh], vbuf.at[to_slot, c], sem.at[to_slot, 1]))

    def start_all(bb, hh, to_slot):
        def body(c, carry):
            for cp in chunk_copies(bb, hh, c, to_slot):
                cp.start()
            return carry
        lax.fori_loop(0, n_chunk, body, 0)

    @pl.when(step == 0)
    def _():
        start_all(b, h, slot)

    nxt = step + 1

    @pl.when(nxt < pl.num_programs(0) * n_heads)
    def _():
        start_all(nxt // n_heads, nxt % n_heads, 1 - slot)

    def wait_body(c, carry):
        for cp in chunk_copies(b, h, c, slot):
            cp.wait()
        return carry

    lax.fori_loop(0, n_chunk, wait_body, 0)
    kch = [kbuf.at[slot, c] for c in range(n_chunk)]
    vch = [vbuf.at[slot, c] for c in range(n_chunk)]
    qt = qt_ref[...] * QK_SCALE
    knt = knt_ref[...]
    vnt = vnt_ref[...]
    lane_n = _iota((1, t_new), 1)
    lane_o = _iota((HEAD_DIM, t_new), 1)
    out = jnp.zeros((HEAD_DIM, t_new), F32)
    for t in range(t_new):
        qcol = qt[:, t:t + 1]
        s_own = jnp.where(lane_n <= t, jnp.sum(qcol * knt, axis=0, keepdims=True), NEG)
        m = jnp.max(s_own, axis=1, keepdims=True)
        scores = []
        for r in range(B_TOPK):
            picked = sel_ref[b, (h * t_new + t) * B_TOPK + r] >= 0
            for pg in range(ppb):
                kc = kch[(t * B_TOPK + r) * ppb + pg][...]
                s = jnp.where(picked, jnp.sum(qcol * kc, axis=0, keepdims=True), NEG)
                m = jnp.maximum(m, jnp.max(s, axis=1, keepdims=True))
                scores.append(s)
        p_own = jnp.exp(s_own - m)
        l = jnp.sum(p_own, axis=1, keepdims=True)
        acc = jnp.sum(p_own * vnt, axis=1, keepdims=True)
        accv = jnp.zeros((HEAD_DIM, LANES), F32)
        for c, s in enumerate(scores):
            p = jnp.exp(s - m)
            l = l + jnp.sum(p, axis=1, keepdims=True)
            accv = accv + p * vch[t * B_TOPK * ppb + c][...]
        acc = acc + jnp.sum(accv, axis=1, keepdims=True)
        out = jnp.where(lane_o == t, acc / l, out)
    o_ref[...] = out


def _moba_gather(qt, knt, vnt, pool_kt, pool_vt, page_table, sel):
    bsz, width, t_new = qt.shape
    _, n_heads, hd, page = pool_kt.shape
    ppb = B_BLOCK // page
    n_chunk = t_new * B_TOPK * ppb
    new_spec = pl.BlockSpec((None, hd, t_new), lambda b, h, pt, sl: (b, h, 0))
    hbm = pl.BlockSpec(memory_space=pl.ANY)
    return pl.pallas_call(
        functools.partial(_moba_gather_body, ppb=ppb),
        grid_spec=pltpu.PrefetchScalarGridSpec(
            num_scalar_prefetch=2,
            grid=(bsz, n_heads),
            in_specs=[new_spec, new_spec, new_spec, hbm, hbm],
            out_specs=pl.BlockSpec((None, hd, t_new), lambda b, h, pt, sl: (b, h, 0)),
            scratch_shapes=[
                pltpu.VMEM((2, n_chunk, hd, page), F32),
                pltpu.VMEM((2, n_chunk, hd, page), F32),
                pltpu.SemaphoreType.DMA((2, 2)),
            ],
        ),
        out_shape=jax.ShapeDtypeStruct((bsz, width, t_new), F32),
        compiler_params=_cparams(("arbitrary", "arbitrary")),
        name="moba_gather",
    )(page_table, sel, qt, knt, vnt, pool_kt, pool_vt)


def _silu(x):
    return x / (1.0 + jnp.exp(-x))


def _ffn_even_body(x_ref, a_ref, b_ref, wo_ref, g_ref, wg_ref, wu_ref, wd_ref, o_ref, hn_ref, acc_ref):
    f = pl.program_id(1)
    aw = a_ref.shape[1]

    @pl.when(f == 0)
    def _():
        x1 = (x_ref[...]
              + jnp.dot(a_ref[...], wo_ref[:aw, :], preferred_element_type=F32)
              + jnp.dot(b_ref[...], wo_ref[aw:, :], preferred_element_type=F32))
        acc_ref[...] = x1
        hn_ref[...] = _rms(x1, g_ref[...]).astype(BF16)

    hn = hn_ref[...]
    gate = jnp.dot(hn, wg_ref[...], preferred_element_type=F32)
    up = jnp.dot(hn, wu_ref[...], preferred_element_type=F32)
    act = (_silu(gate) * up).astype(BF16)
    acc_ref[...] += jnp.dot(act, wd_ref[...], preferred_element_type=F32)

    @pl.when(f == pl.num_programs(1) - 1)
    def _():
        o_ref[...] = acc_ref[...]


def _ffn_even(x, a, b, wo16, g, wg16, wu16, wd16, *, tm=640, tf=1408):
    rows, dm = x.shape
    aw = a.shape[1]
    bw = b.shape[1]
    dff = wg16.shape[1]
    return pl.pallas_call(
        _ffn_even_body,
        grid=(rows // tm, dff // tf),
        in_specs=[
            pl.BlockSpec((tm, dm), lambda i, f: (i, 0)),
            pl.BlockSpec((tm, aw), lambda i, f: (i, 0)),
            pl.BlockSpec((tm, bw), lambda i, f: (i, 0)),
            pl.BlockSpec((aw + bw, dm), lambda i, f: (0, 0)),
            pl.BlockSpec((1, dm), lambda i, f: (0, 0)),
            pl.BlockSpec((dm, tf), lambda i, f: (0, f)),
            pl.BlockSpec((dm, tf), lambda i, f: (0, f)),
            pl.BlockSpec((tf, dm), lambda i, f: (f, 0)),
        ],
        out_specs=pl.BlockSpec((tm, dm), lambda i, f: (i, 0)),
        out_shape=jax.ShapeDtypeStruct((rows, dm), F32),
        scratch_shapes=[pltpu.VMEM((tm, dm), BF16), pltpu.VMEM((tm, dm), F32)],
        compiler_params=_cparams(("arbitrary", "arbitrary")),
        name="ffn_even",
    )(x, a, b, wo16, g, wg16, wu16, wd16)


def _odd_in_body(x_ref, g_ref, w_ref, wt_ref, q_ref, k_ref, v_ref, kt_ref, vt_ref, *, n_prompt_tiles):
    cw = q_ref.shape[1]
    h = _rms(x_ref[...], g_ref[...]).astype(BF16)
    y = jnp.dot(h, w_ref[...], preferred_element_type=F32)
    q_ref[...] = y[:, :cw]
    k_ref[...] = y[:, cw:2 * cw]
    v_ref[...] = y[:, 2 * cw:]
    _feature_major_kv(pl.program_id(0), n_prompt_tiles, wt_ref, h, kt_ref, vt_ref)


def _odd_in(x, bsz, t_len, g, w16, wt16, *, tm=256):
    rows, dm = x.shape
    cw = w16.shape[1] // 3
    npt = bsz * t_len // tm
    out = jax.ShapeDtypeStruct((rows, cw), F32)
    out_t = jax.ShapeDtypeStruct((bsz, cw, t_len), F32)
    spec = pl.BlockSpec((tm, cw), lambda i: (i, 0))
    fm_spec = _feature_major_spec(cw, tm, npt, t_len // tm)
    return pl.pallas_call(
        functools.partial(_odd_in_body, n_prompt_tiles=npt),
        grid=(rows // tm,),
        in_specs=[
            pl.BlockSpec((tm, dm), lambda i: (i, 0)),
            pl.BlockSpec((1, dm), lambda i: (0, 0)),
            pl.BlockSpec((dm, 3 * cw), lambda i: (0, 0)),
            pl.BlockSpec((2 * cw, dm), lambda i: (0, 0)),
        ],
        out_specs=[spec, spec, spec, fm_spec, fm_spec],
        out_shape=[out, out, out, out_t, out_t],
        compiler_params=_cparams(("arbitrary",)),
        name="odd_in",
    )(x, g, w16, wt16)


def _dilated_prompt_body(*refs, patterns, tiles):
    q_refs, k_refs, v_refs = (refs[g * tiles:(g + 1) * tiles] for g in range(3))
    o_ref = refs[3 * tiles]
    state = refs[3 * tiles + 1:]
    t_len = q_refs[0].shape[0]
    blk = 128
    lo = _iota((blk, LANES), 1) < HEAD_DIM
    n_br = len(patterns)
    n_iter = t_len // blk
    masks = []
    for window, dil in patterns:
        assert window // dil == blk
        has_prev = t_len // (dil * blk) > 1
        n_keys = 2 * blk if has_prev else blk
        qrow = _iota((2 * blk, n_keys), 0) % blk
        kcol = _iota((2 * blk, n_keys), 1)
        if has_prev:
            masks.append(((kcol >= blk) & ((kcol - blk) <= qrow), (kcol < blk) & (kcol >= qrow)))
        else:
            masks.append((kcol <= qrow, None))

    def body(it, carry):
        for bi, (window, dil) in enumerate(patterns):
            nblk = t_len // (dil * blk)
            cur_ok, prev_ok = masks[bi]
            cls = it // nblk
            ib = it % nblk

            def rows_at(block, cls=cls, dil=dil):
                start = cls + dil * blk * block
                if dil == 1:
                    return pl.ds(pl.multiple_of(start, blk), blk)
                return pl.ds(start, blk, stride=dil)

            rows = rows_at(ib)
            prows = rows_at(jnp.maximum(ib - 1, 0))
            for tile in range(tiles):
                q_ref, k_ref, v_ref = q_refs[tile], k_refs[tile], v_refs[tile]
                m_s, l_s, acc_s = state[3 * (tile * n_br + bi):3 * (tile * n_br + bi) + 3]
                q = q_ref[rows, :] * QK_SCALE
                qst = jnp.concatenate([jnp.where(lo, q, 0.0), jnp.where(lo, 0.0, q)], axis=0).astype(BF16)
                kk = k_ref[rows, :].astype(BF16)
                vv = v_ref[rows, :].astype(BF16)
                ok = cur_ok
                if prev_ok is not None:
                    kk = jnp.concatenate([k_ref[prows, :].astype(BF16), kk], axis=0)
                    vv = jnp.concatenate([v_ref[prows, :].astype(BF16), vv], axis=0)
                    ok = cur_ok | (prev_ok & (ib > 0))
                s = jnp.where(ok, lax.dot_general(qst, kk, NT, preferred_element_type=F32), NEG)
                m = jnp.max(s, axis=1, keepdims=True)
                p = jnp.exp(s - m)
                l = jnp.sum(p, axis=1, keepdims=True)
                pv = jnp.dot(p.astype(BF16), vv, preferred_element_type=F32)
                m_s[rows, :] = jnp.where(lo, m[:blk], m[blk:])
                l_s[rows, :] = jnp.where(lo, l[:blk], l[blk:])
                acc_s[rows, :] = jnp.where(lo, pv[:blk], pv[blk:])
        return carry

    lax.fori_loop(0, n_iter, body, 0)

    chunk = 128

    def merge(c, carry):
        rows = pl.ds(pl.multiple_of(c * chunk, chunk), chunk)
        for tile in range(tiles):
            st = state[3 * tile * n_br:3 * (tile + 1) * n_br]
            ms = [st[3 * r][rows, :] for r in range(n_br)]
            m = functools.reduce(jnp.maximum, ms)
            num = jnp.zeros((chunk, LANES), F32)
            den = jnp.zeros((chunk, LANES), F32)
            for r in range(n_br):
                w = jnp.exp(ms[r] - m)
                num = num + w * st[3 * r + 2][rows, :]
                den = den + w * st[3 * r + 1][rows, :]
            o_ref[rows, tile * LANES:(tile + 1) * LANES] = (num / den).astype(BF16)
        return carry

    lax.fori_loop(0, t_len // chunk, merge, 0)


def _dilated_prompt(q, k, v, bsz, t_len, *, tiles=2):
    width = q.shape[1]
    specs = [pl.BlockSpec((t_len, LANES), lambda b, j, t=t: (b, j * tiles + t)) for t in range(tiles)]
    return pl.pallas_call(
        functools.partial(_dilated_prompt_body, patterns=C_PATTERNS, tiles=tiles),
        grid=(bsz, width // (tiles * LANES)),
        in_specs=specs * 3,
        out_specs=pl.BlockSpec((t_len, tiles * LANES), lambda b, j: (b, j)),
        out_shape=jax.ShapeDtypeStruct((bsz * t_len, width), BF16),
        scratch_shapes=[pltpu.VMEM((t_len, LANES), F32)] * (3 * len(C_PATTERNS) * tiles),
        compiler_params=_cparams(("arbitrary", "arbitrary")),
        name="dilated_prompt",
    )(*([q] * tiles), *([k] * tiles), *([v] * tiles))


def _dilated_sample_body(q_ref, kn_ref, vn_ref, kc_ref, vc_ref, o_ref, *, patterns):
    t_new = q_ref.shape[0]
    w_len = kc_ref.shape[1]
    nrow = 2 * t_new
    lane = _iota((t_new, LANES), 1)
    q = q_ref[...] * QK_SCALE
    qst = jnp.concatenate([jnp.where(lane < HEAD_DIM, q, 0.0), jnp.where(lane >= HEAD_DIM, q, 0.0)], axis=0)
    qst = qst.astype(BF16)

    def mult(delta):
        cnt = jnp.zeros(delta.shape, F32)
        for window, dil in patterns:
            hit = (delta >= 0) & (delta <= window) & ((delta % dil) == 0)
            cnt = cnt + hit.astype(F32)
        return cnt

    t_c = _iota((nrow, w_len), 0) % t_new
    w_c = mult(w_len + t_c - _iota((nrow, w_len), 1))
    t_n = _iota((nrow, LANES), 0) % t_new
    c_n = _iota((nrow, LANES), 1)
    w_n = jnp.where(c_n < t_new, mult(t_n - c_n), 0.0)

    pad = jnp.zeros((LANES - t_new, LANES), F32)
    kn = jnp.concatenate([kn_ref[...], pad], axis=0).astype(BF16)
    vn = jnp.concatenate([vn_ref[...], pad], axis=0).astype(BF16)
    s_c = jnp.dot(qst, kc_ref[...].astype(BF16), preferred_element_type=F32)
    s_n = lax.dot_general(qst, kn, NT, preferred_element_type=F32)
    s_c = jnp.where(w_c > 0, s_c, NEG)
    s_n = jnp.where(w_n > 0, s_n, NEG)
    m = jnp.maximum(jnp.max(s_c, axis=1, keepdims=True), jnp.max(s_n, axis=1, keepdims=True))
    p_c = w_c * jnp.exp(s_c - m)
    p_n = w_n * jnp.exp(s_n - m)
    l = jnp.sum(p_c, axis=1, keepdims=True) + jnp.sum(p_n, axis=1, keepdims=True)
    acc = (lax.dot_general(p_c.astype(BF16), vc_ref[...].astype(BF16), NT, preferred_element_type=F32)
           + jnp.dot(p_n.astype(BF16), vn, preferred_element_type=F32))
    out = acc / l
    o_ref[...] = jnp.where(lane < HEAD_DIM, out[:t_new], out[t_new:])


def _dilated_sample(q, k, v, row_block0, cache_kt, cache_vt):
    bsz, width, w_len = cache_kt.shape
    t_new = 8
    new_spec = pl.BlockSpec((t_new, LANES), lambda b, j: (row_block0 + b, j))
    cache_spec = pl.BlockSpec((None, LANES, w_len), lambda b, j: (b, j, 0))
    return pl.pallas_call(
        functools.partial(_dilated_sample_body, patterns=C_PATTERNS),
        grid=(bsz, width // LANES),
        in_specs=[new_spec, new_spec, new_spec, cache_spec, cache_spec],
        out_specs=pl.BlockSpec((t_new, LANES), lambda b, j: (b, j)),
        out_shape=jax.ShapeDtypeStruct((bsz * t_new, width), F32),
        compiler_params=_cparams(("arbitrary", "arbitrary")),
        name="dilated_sample",
    )(q, k, v, cache_kt, cache_vt)


def _moe_route_body(x_ref, c_ref, wo_ref, g_ref, r_ref,
                    x3_ref, hn_ref, comb_ref, pos_ref, post_ref, cnt_ref, before_ref):
    tb = x_ref.shape[0]

    @pl.when(pl.program_id(0) == 0)
    def _():
        before_ref[...] = jnp.where(_iota((tb, tb), 1) < _iota((tb, tb), 0), 1.0, 0.0).astype(BF16)

    x3 = x_ref[...] + jnp.dot(c_ref[...], wo_ref[...], preferred_element_type=F32)
    x3_ref[...] = x3
    hf = _rms(x3, g_ref[...])
    hn_ref[...] = hf.astype(BF16)
    lane = _iota((tb, LANES), 1)
    logits = jnp.dot(hf, r_ref[...], precision=HIGHEST, preferred_element_type=F32)
    lg = jnp.where(lane < N_EXPERTS, logits, -jnp.inf)
    m1 = jnp.max(lg, axis=1, keepdims=True)
    p1 = lane == jnp.min(jnp.where(lg == m1, lane, LANES), axis=1, keepdims=True)
    lg2 = jnp.where(p1, -jnp.inf, lg)
    m2 = jnp.max(lg2, axis=1, keepdims=True)
    p2 = lane == jnp.min(jnp.where(lg2 == m2, lane, LANES), axis=1, keepdims=True)
    e2 = jnp.exp(m2 - m1)
    den = 1.0 + e2
    comb_ref[...] = jnp.where(p1, 1.0 / den, 0.0) + jnp.where(p2, e2 / den, 0.0)
    routed = p1 | p2
    ind = jnp.where(routed, 1.0, 0.0)
    slot = jnp.where(routed, jnp.dot(before_ref[...], ind.astype(BF16), preferred_element_type=F32), -1.0)
    pos_ref[...] = slot
    eye = jnp.where(_iota((LANES, LANES), 0) == _iota((LANES, LANES), 1), 1.0, 0.0)
    post_ref[...] = lax.dot_general(eye, slot, NT, precision=HIGHEST, preferred_element_type=F32)
    cnt = jnp.sum(ind, axis=0, keepdims=True)
    cnt_ref[...] = jnp.broadcast_to(cnt, cnt_ref.shape).astype(jnp.int32)


def _moe_route(x, c, wo16, g, router_pad, *, tb):
    rows, dm = x.shape
    nblk = rows // tb
    row_spec = lambda width: pl.BlockSpec((tb, width), lambda i: (i, 0))
    return pl.pallas_call(
        _moe_route_body,
        grid=(nblk,),
        in_specs=[
            row_spec(dm), row_spec(dm),
            pl.BlockSpec((dm, dm), lambda i: (0, 0)),
            pl.BlockSpec((1, dm), lambda i: (0, 0)),
            pl.BlockSpec((dm, LANES), lambda i: (0, 0)),
        ],
        out_specs=[
            row_spec(dm), row_spec(dm), row_spec(LANES), row_spec(LANES),
            pl.BlockSpec((None, LANES, tb), lambda i: (i, 0, 0)),
            pl.BlockSpec((None, 8, LANES), lambda i: (i, 0, 0)),
        ],
        out_shape=[
            jax.ShapeDtypeStruct((rows, dm), F32),
            jax.ShapeDtypeStruct((rows, dm), BF16),
            jax.ShapeDtypeStruct((rows, LANES), F32),
            jax.ShapeDtypeStruct((rows, LANES), F32),
            jax.ShapeDtypeStruct((nblk, LANES, tb), F32),
            jax.ShapeDtypeStruct((nblk, 8, LANES), jnp.int32),
        ],
        scratch_shapes=[pltpu.VMEM((tb, tb), BF16)],
        compiler_params=_cparams(("arbitrary",)),
        name="moe_route",
    )(x, c, wo16, g, router_pad)


def _moe_expert_body(cnt_ref, acc_ref, hn_ref, comb_ref, pos_ref, post_ref, gf_ref,
                     wg_ref, wu_ref, wd_ref, o_ref, *, e, last, cap):
    i = pl.program_id(0)
    tb = hn_ref.shape[0]
    n_sub = (cnt_ref[i, e] + cap - 1) // cap
    lane = _iota((tb, LANES), 1)
    w_col = jnp.sum(jnp.where(lane == e, comb_ref[...], 0.0), axis=1, keepdims=True)
    slot_col = jnp.sum(jnp.where(lane == e, pos_ref[...], 0.0), axis=1, keepdims=True)
    slot_row = post_ref[pl.ds(e, 1), :]
    sub_of_row = _iota((cap, tb), 0).astype(F32)
    sub_of_col = _iota((tb, cap), 1).astype(F32)
    o_ref[...] = acc_ref[...]

    def body(s, carry):
        base = (s * cap).astype(F32)
        gather = jnp.where(slot_row - base == sub_of_row, 1.0, 0.0).astype(BF16)
        xe = jnp.dot(gather, hn_ref[...], preferred_element_type=F32).astype(BF16)
        gate = jnp.dot(xe, wg_ref[...], preferred_element_type=F32)
        up = jnp.dot(xe, wu_ref[...], preferred_element_type=F32)
        act = (_silu(gate) * up).astype(BF16)
        y = jnp.dot(act, wd_ref[...], preferred_element_type=F32).astype(BF16)
        scatter = jnp.where(slot_col - base == sub_of_col, 1.0, 0.0).astype(BF16)
        o_ref[...] += w_col * jnp.dot(scatter, y, preferred_element_type=F32)
        return carry

    lax.fori_loop(0, n_sub, body, 0)
    if last:
        o_ref[...] = _rms(o_ref[...], gf_ref[...])


def _moe_experts(counts, x3, hn, comb, pos, post, g_final, wg16, wu16, wd16, *, tb, cap):
    rows, dm = x3.shape
    n_exp, _, dff = wg16.shape
    once = pl.Buffered(1)
    acc = x3
    for e in range(n_exp):
        acc = pl.pallas_call(
            functools.partial(_moe_expert_body, e=e, last=(e == n_exp - 1), cap=cap),
            grid_spec=pltpu.PrefetchScalarGridSpec(
                num_scalar_prefetch=1,
                grid=(rows // tb,),
                in_specs=[
                    pl.BlockSpec((tb, dm), lambda i, cnt: (i, 0)),
                    pl.BlockSpec((tb, dm), lambda i, cnt: (i, 0)),
                    pl.BlockSpec((tb, LANES), lambda i, cnt: (i, 0)),
                    pl.BlockSpec((tb, LANES), lambda i, cnt: (i, 0)),
                    pl.BlockSpec((None, LANES, tb), lambda i, cnt: (i, 0, 0)),
                    pl.BlockSpec((1, dm), lambda i, cnt: (0, 0)),
                    pl.BlockSpec((None, dm, dff), lambda i, cnt, e=e: (e, 0, 0), pipeline_mode=once),
                    pl.BlockSpec((None, dm, dff), lambda i, cnt, e=e: (e, 0, 0), pipeline_mode=once),
                    pl.BlockSpec((None, dff, dm), lambda i, cnt, e=e: (e, 0, 0), pipeline_mode=once),
                ],
                out_specs=pl.BlockSpec((tb, dm), lambda i, cnt: (i, 0)),
            ),
            out_shape=jax.ShapeDtypeStruct((rows, dm), F32),
            compiler_params=_cparams(("arbitrary",)),
            name=f"moe_expert{e}",
        )(counts, acc, hn, comb, pos, post, g_final, wg16, wu16, wd16)
    return acc


def _moe(x, c, wo16, g, router_pad, g_final, wg16, wu16, wd16, *, tb=832, cap=256):
    x3, hn, comb, pos, post, cnt = _moe_route(x, c, wo16, g, router_pad, tb=tb)
    counts = cnt[:, 0, :N_EXPERTS]
    return _moe_experts(counts, x3, hn, comb, pos, post, g_final, wg16, wu16, wd16, tb=tb, cap=cap)


def kernel(x_prompt, x_sample, cache_b_k, cache_b_v, cache_c_k, cache_c_v, page_table, norm_mix, norm_ffn,
           norm_final, w_in_even, w_out_even, a_ln_g, a_ln_b, a_ws, a_bs, ffn_gate, ffn_up, ffn_down,
           w_in_odd, w_out_odd, router, moe_gate, moe_up, moe_down):
    bsz, t_len, dm = x_prompt.shape
    dbsz, t_new, _ = x_sample.shape
    assert norm_mix.shape[0] == 2 and w_in_even.shape[0] == 1 and w_in_odd.shape[0] == 1
    assert cache_b_k.shape[0] == 1 and cache_c_k.shape[0] == 1
    assert t_new == 8 and A_CHUNK % t_new == 0
    n_p = bsz * t_len
    n_s = dbsz * t_new
    aw = a_ln_g.shape[-1]
    n_groups = aw // HEAD_DIM
    seq_per_chunk = A_CHUNK // t_new

    x = jnp.concatenate([x_prompt.reshape(n_p, dm), x_sample.reshape(n_s, dm)], axis=0)

    ws = a_ws[0]
    ws_sample = jnp.tile(ws[:, :t_new, :t_new], (1, seq_per_chunk, seq_per_chunk))
    ws2 = jnp.stack([ws, ws_sample])
    bias_p = jnp.repeat(a_bs[0].T, HEAD_DIM, axis=1)
    bias_s = jnp.repeat(jnp.tile(a_bs[0][:, :t_new].T, (seq_per_chunk, 1)), HEAD_DIM, axis=1)
    bias2 = jnp.stack([bias_p, bias_s])
    row = lambda vec: vec.reshape(1, -1)
    router_pad = jnp.pad(router[0], ((0, 0), (0, LANES - router.shape[-1])))

    w_even = w_in_even[0].astype(BF16)
    kv_even_t = w_even[:, w_even.shape[1] - 2 * (B_HEADS * HEAD_DIM):].T
    a_out, v_a, q_b, k_b, v_b, kt_b, vt_b, v_a_last = _even_in(
        x, bsz, t_len, row(norm_mix[0]), w_even, kv_even_t, row(a_ln_g[0]), row(a_ln_b[0]), ws2, bias2)
    b_prompt = _moba_prompt(q_b, k_b, v_b, bsz, t_len)
    page = cache_b_k.shape[2]
    bw = q_b.shape[1]
    feature_major = lambda c: jnp.transpose(c, (0, 1, 3, 4, 2))
    pool_kt = feature_major(cache_b_k).reshape(-1, B_HEADS, HEAD_DIM, page)
    pool_vt = feature_major(cache_b_v).reshape(-1, B_HEADS, HEAD_DIM, page)
    sel = _moba_select(q_b, n_p // t_new, pool_kt.reshape(-1, bw, page), page_table)
    sel = sel.reshape(dbsz, B_HEADS * t_new, LANES)[:, :, :B_TOPK].reshape(dbsz, -1)
    new_t = lambda a: jnp.transpose(a[n_p:].reshape(dbsz, t_new, -1), (0, 2, 1))
    ppb = B_BLOCK // page
    logical = (jnp.maximum(sel, 0)[:, :, None] * ppb + jnp.arange(ppb, dtype=jnp.int32)).reshape(dbsz, -1)
    hit = logical[:, :, None] == jnp.arange(page_table.shape[1], dtype=jnp.int32)
    sel_pages = jnp.sum(jnp.where(hit, page_table[:, None, :], 0), axis=-1)
    b_sample_t = _moba_gather(new_t(q_b), new_t(k_b), new_t(v_b), pool_kt, pool_vt, sel_pages, sel)
    b_sample = jnp.transpose(b_sample_t, (0, 2, 1)).reshape(n_s, bw)
    b_out = jnp.concatenate([b_prompt, b_sample.astype(BF16)], axis=0)
    x2 = _ffn_even(x, a_out, b_out, w_out_even[0].astype(BF16), row(norm_ffn[0]),
                   ffn_gate[0].astype(BF16), ffn_up[0].astype(BF16), ffn_down[0].astype(BF16))

    w_odd = w_in_odd[0].astype(BF16)
    q_c, k_c, v_c, kt_c, vt_c = _odd_in(x2, bsz, t_len, row(norm_mix[1]), w_odd, w_odd[:, w_odd.shape[1] // 3:].T)
    c_prompt = _dilated_prompt(q_c, k_c, v_c, bsz, t_len)
    cw = q_c.shape[1]
    w_len = cache_c_k.shape[2]
    c_sample = _dilated_sample(q_c, k_c, v_c, n_p // t_new,
                               feature_major(cache_c_k).reshape(-1, cw, w_len),
                               feature_major(cache_c_v).reshape(-1, cw, w_len))
    c_out = jnp.concatenate([c_prompt, c_sample.astype(BF16)], axis=0)
    y = _moe(x2, c_out, w_out_odd[0].astype(BF16), row(norm_ffn[1]), router_pad, row(norm_final),
             moe_gate[0].astype(BF16), moe_up[0].astype(BF16), moe_down[0].astype(BF16))

    bh = (B_HEADS, HEAD_DIM)
    ch = (cw // HEAD_DIM, HEAD_DIM)
    assert t_len % 256 == 0
    c_keep_p = min(C_PATTERNS[-1][0], t_len)
    leaf = lambda a_t, hd: jnp.transpose(a_t.reshape(1, bsz, *hd, a_t.shape[-1]), (0, 1, 4, 2, 3))
    return (
        y[:n_p].reshape(bsz, t_len, dm),
        y[n_p:].reshape(dbsz, t_new, dm),
        v_a_last.reshape(1, bsz, A_CHUNK, aw),
        v_a[n_p:].reshape(1, dbsz, t_new, aw),
        leaf(kt_b, bh),
        leaf(vt_b, bh),
        k_b[n_p:].reshape(1, dbsz, t_new, *bh),
        v_b[n_p:].reshape(1, dbsz, t_new, *bh),
        leaf(kt_c[:, :, t_len - c_keep_p:], ch),
        leaf(vt_c[:, :, t_len - c_keep_p:], ch),
        k_c[n_p:].reshape(1, dbsz, t_new, *ch),
        v_c[n_p:].reshape(1, dbsz, t_new, *ch),
    )
```

```python
import functools

import jax
import jax.numpy as jnp
from jax import lax
from jax.experimental import pallas as pl
from jax.experimental.pallas import tpu as pltpu

F32 = jnp.float32
BF16 = jnp.bfloat16
HIGHEST = lax.Precision.HIGHEST

HEAD_DIM = 64
LANES = 128
A_CHUNK = 128
B_HEADS = 8
B_BLOCK = 256
B_TOPK = 3
C_PATTERNS = ((128, 1), (512, 4), (2048, 16))
N_EXPERTS = 8
RMS_EPS = 1e-6
LN_EPS = 1e-5
NEG = -1e30
QK_SCALE = HEAD_DIM ** -0.5
VMEM_LIMIT = 56 * 1024 * 1024

NT = (((1,), (1,)), ((), ()))


def _cparams(sem):
    return pltpu.CompilerParams(dimension_semantics=sem, vmem_limit_bytes=VMEM_LIMIT)


def _rms(x, g):
    return x * lax.rsqrt(jnp.mean(x * x, axis=-1, keepdims=True) + RMS_EPS) * g


def _gelu(x):
    return 0.5 * x * (1.0 + lax.erf(x * 0.7071067811865476))


def _iota(shape, dim):
    return lax.broadcasted_iota(jnp.int32, shape, dim)


def _feature_major_kv(i, n_prompt_tiles, k_tile, v_tile, kt_ref, vt_ref):
    @pl.when(i < n_prompt_tiles)
    def _():
        kt_ref[...] = k_tile.T
        vt_ref[...] = v_tile.T


def _feature_major_spec(width, tm, n_prompt_tiles, tiles_per_seq):
    def imap(i):
        ic = jnp.minimum(i, n_prompt_tiles - 1)
        return (ic // tiles_per_seq, 0, ic % tiles_per_seq)
    return pl.BlockSpec((None, width, tm), imap)


def _even_in_body(x_ref, g_ref, w_ref, lng_ref, lnb_ref, ws_ref, bias_ref,
                  a_ref, v_ref, q_ref, k_ref, vv_ref, kt_ref, vt_ref, vlast_ref,
                  *, n_prompt_tiles, tiles_per_seq, aw, bw):
    i = pl.program_id(0)
    tm = x_ref.shape[0]
    h = _rms(x_ref[...], g_ref[...]).astype(BF16)
    y = jnp.dot(h, w_ref[...], preferred_element_type=F32)
    _feature_major_kv(i, n_prompt_tiles, y[:, 2 * aw + bw:2 * aw + 2 * bw], y[:, 2 * aw + 2 * bw:], kt_ref, vt_ref)
    u = _gelu(y[:, :aw])
    v = _gelu(y[:, aw:2 * aw])
    mu = jnp.mean(v, axis=-1, keepdims=True)
    vc = v - mu
    var = jnp.mean(vc * vc, axis=-1, keepdims=True)
    v = vc * lax.rsqrt(var + LN_EPS) * lng_ref[...] + lnb_ref[...]
    v_ref[...] = v

    @pl.when((i < n_prompt_tiles) & (i % tiles_per_seq == tiles_per_seq - 1))
    def _():
        vlast_ref[...] = v[tm - A_CHUNK:, :]
    q_ref[...] = y[:, 2 * aw:2 * aw + bw]
    k_ref[...] = y[:, 2 * aw + bw:2 * aw + 2 * bw]
    vv_ref[...] = y[:, 2 * aw + 2 * bw:]

    row = _iota((A_CHUNK, A_CHUNK), 0)
    col = _iota((A_CHUNK, A_CHUNK), 1)
    is_prompt = i < n_prompt_tiles
    ok = (col <= row) & (((row // 8) == (col // 8)) | is_prompt)
    lo = _iota((A_CHUNK, LANES), 1) < HEAD_DIM
    v16 = v.astype(BF16)
    n_groups = aw // HEAD_DIM
    wm = [jnp.where(ok, ws_ref[g], 0.0).astype(BF16) for g in range(n_groups)]
    for c in range(tm // A_CHUNK):
        rs = slice(c * A_CHUNK, (c + 1) * A_CHUNK)
        for j in range(aw // LANES):
            cs = slice(j * LANES, (j + 1) * LANES)
            vj = v16[rs, cs]
            ma = jnp.dot(wm[2 * j], vj, preferred_element_type=F32)
            mb = jnp.dot(wm[2 * j + 1], vj, preferred_element_type=F32)
            mixed = jnp.where(lo, ma, mb) + bias_ref[:, cs]
            a_ref[rs, cs] = (u[rs, cs] * mixed).astype(BF16)


def _even_in(x, bsz, t_len, g, w16, lng, lnb, ws2, bias2, *, tm=256):
    rows, dm = x.shape
    aw = lng.shape[-1]
    n_in = w16.shape[1]
    bw = (n_in - 2 * aw) // 3
    n_groups = aw // HEAD_DIM
    npt = bsz * t_len // tm
    sel = lambda i: jnp.minimum(i // npt, 1)
    row_spec = lambda width: pl.BlockSpec((tm, width), lambda i: (i, 0))
    tps = t_len // tm
    fm_spec = _feature_major_spec(bw, tm, npt, tps)
    return pl.pallas_call(
        functools.partial(_even_in_body, n_prompt_tiles=npt, tiles_per_seq=tps, aw=aw, bw=bw),
        grid=(rows // tm,),
        in_specs=[
            row_spec(dm),
            pl.BlockSpec((1, dm), lambda i: (0, 0)),
            pl.BlockSpec((dm, n_in), lambda i: (0, 0)),
            pl.BlockSpec((1, aw), lambda i: (0, 0)),
            pl.BlockSpec((1, aw), lambda i: (0, 0)),
            pl.BlockSpec((None, n_groups, A_CHUNK, A_CHUNK), lambda i: (sel(i), 0, 0, 0)),
            pl.BlockSpec((None, A_CHUNK, aw), lambda i: (sel(i), 0, 0)),
        ],
        out_specs=[row_spec(aw), row_spec(aw), row_spec(bw), row_spec(bw), row_spec(bw), fm_spec, fm_spec,
                   pl.BlockSpec((None, A_CHUNK, aw), lambda i: (jnp.minimum(i, npt - 1) // tps, 0, 0))],
        out_shape=[
            jax.ShapeDtypeStruct((rows, aw), BF16),
            jax.ShapeDtypeStruct((rows, aw), F32),
            jax.ShapeDtypeStruct((rows, bw), F32),
            jax.ShapeDtypeStruct((rows, bw), F32),
            jax.ShapeDtypeStruct((rows, bw), F32),
            jax.ShapeDtypeStruct((bsz, bw, t_len), F32),
            jax.ShapeDtypeStruct((bsz, bw, t_len), F32),
            jax.ShapeDtypeStruct((bsz, A_CHUNK, aw), F32),
        ],
        compiler_params=_cparams(("arbitrary",)),
        name="even_in",
    )(x, g, w16, lng, lnb, ws2, bias2)


def _moba_prompt_body(q_ref, k_ref, v_ref, o_ref, k16, v16, mt_ref, s_scr, p_scr, *, nb):
    i = pl.program_id(1)
    t_len, width = k_ref.shape
    n_gate = B_HEADS * nb

    @pl.when(i == 0)
    def _():
        k = k_ref[...]
        k16[...] = k.astype(BF16)
        v16[...] = v_ref[...].astype(BF16)
        r = _iota((LANES, t_len), 0)
        key = _iota((LANES, t_len), 1)
        avg = jnp.where(((key // B_BLOCK) == (r % nb)) & (r < n_gate), 1.0 / B_BLOCK, 0.0).astype(F32)
        mt = jnp.dot(avg, k, precision=HIGHEST, preferred_element_type=F32)
        rr = _iota((LANES, width), 0)
        cc = _iota((LANES, width), 1)
        mt_ref[...] = jnp.where(((rr // nb) == (cc // HEAD_DIM)) & (rr < n_gate), mt, 0.0)

    q = q_ref[...]
    tq = q.shape[0]
    gate = lax.dot_general(q, mt_ref[...], NT, precision=HIGHEST, preferred_element_type=F32)
    lane = _iota((tq, LANES), 1)
    n_of = lane % nb
    valid = (n_of < i) & (lane < n_gate)
    gm = jnp.where(valid, gate, -jnp.inf)
    rank = jnp.zeros((tq, LANES), jnp.int32)
    for s in range(1, nb):
        fwd = pltpu.roll(gm, LANES - s, 1)
        bwd = pltpu.roll(gm, nb - s, 1)
        wrap = (n_of + s) >= nb
        other = jnp.where(wrap, bwd, fwd)
        m_idx = jnp.where(wrap, n_of + s - nb, n_of + s)
        beats = (other > gm) | ((other == gm) & (m_idx < n_of))
        rank = rank + beats.astype(jnp.int32)
    selbias = jnp.where(valid & (rank < B_TOPK), 0.0, NEG)

    row2 = _iota((2 * tq, B_BLOCK), 0) % tq
    col2 = _iota((2 * tq, B_BLOCK), 1)
    causal_bias = jnp.where(col2 <= row2, 0.0, NEG)
    lo = lane < HEAD_DIM
    own = pl.multiple_of(i * B_BLOCK, B_BLOCK)
    half = B_BLOCK // 2

    key_lane = _iota((B_BLOCK, LANES), 1)

    group = s_scr.shape[0]

    def zero_body(n, carry):
        for g in range(group):
            p_scr[g, n] = jnp.zeros((2 * tq, B_BLOCK), BF16)
        return carry

    lax.fori_loop(i + 1, nb, zero_body, 0)
    for j0 in range(0, width // LANES, group):
        cols = [slice((j0 + g) * LANES, (j0 + g + 1) * LANES) for g in range(group)]
        q_aug, m_tiles = [], []
        for g in range(group):
            j = j0 + g
            qj = q[:, cols[g]] * QK_SCALE
            qst = jnp.concatenate([jnp.where(lo, qj, 0.0), jnp.where(lo, 0.0, qj)], axis=0).astype(BF16)
            s = (lax.dot_general(qst, k16[pl.ds(own, B_BLOCK), cols[g]], NT, preferred_element_type=F32)
                 + causal_bias)
            s_scr[g, i] = s
            m_tiles.append(jnp.maximum(s[:, :half], s[:, half:]))
            sb = []
            for a in range(2):
                off = (2 * j + a) * nb
                moved = selbias if off == 0 else pltpu.roll(selbias, LANES - off, 1)
                sb.append(jnp.where(lane < nb, moved, 0.0))
            q_aug.append(jnp.concatenate([qst, jnp.concatenate(sb, axis=0).astype(BF16)], axis=1))

        def score_body(n, m_tiles, q_aug=q_aug, cols=cols):
            start = pl.multiple_of(n * B_BLOCK, B_BLOCK)
            onehot = jnp.where(key_lane == n, 1.0, 0.0).astype(BF16)
            out = []
            for g in range(group):
                k_aug = jnp.concatenate([k16[pl.ds(start, B_BLOCK), cols[g]], onehot], axis=1)
                s = lax.dot_general(q_aug[g], k_aug, NT, preferred_element_type=F32)
                s_scr[g, n] = s
                out.append(jnp.maximum(m_tiles[g], jnp.maximum(s[:, :half], s[:, half:])))
            return tuple(out)

        m_tiles = lax.fori_loop(0, i, score_body, tuple(m_tiles))
        ms = [jnp.max(m_tiles[g], axis=1, keepdims=True) for g in range(group)]

        def prob_body(n, l_tiles, ms=ms):
            out = []
            for g in range(group):
                p = jnp.exp(s_scr[g, n] - ms[g])
                p_scr[g, n] = p.astype(BF16)
                out.append(l_tiles[g] + p[:, :half] + p[:, half:])
            return tuple(out)

        l_tiles = lax.fori_loop(0, i + 1, prob_body,
                                tuple(jnp.zeros((2 * tq, half), F32) for _ in range(group)))
        for g in range(group):
            l = jnp.sum(l_tiles[g], axis=1, keepdims=True)
            acc = jnp.dot(p_scr[g, 0], v16[0:B_BLOCK, cols[g]], preferred_element_type=F32)
            for n in range(1, nb):
                acc = acc + jnp.dot(p_scr[g, n], v16[n * B_BLOCK:(n + 1) * B_BLOCK, cols[g]],
                                    preferred_element_type=F32)
            o = acc / l
            o_ref[:, cols[g]] = jnp.where(lo, o[:tq], o[tq:]).astype(BF16)


def _moba_prompt(q, k, v, bsz, t_len):
    width = q.shape[1]
    nb = t_len // B_BLOCK
    return pl.pallas_call(
        functools.partial(_moba_prompt_body, nb=nb),
        grid=(bsz, nb),
        in_specs=[
            pl.BlockSpec((B_BLOCK, width), lambda b, i: (b * nb + i, 0)),
            pl.BlockSpec((t_len, width), lambda b, i: (b, 0)),
            pl.BlockSpec((t_len, width), lambda b, i: (b, 0)),
        ],
        out_specs=pl.BlockSpec((B_BLOCK, width), lambda b, i: (b * nb + i, 0)),
        out_shape=jax.ShapeDtypeStruct((bsz * t_len, width), BF16),
        scratch_shapes=[
            pltpu.VMEM((t_len, width), BF16),
            pltpu.VMEM((t_len, width), BF16),
            pltpu.VMEM((LANES, width), F32),
            pltpu.VMEM((2, nb, 2 * B_BLOCK, B_BLOCK), F32),
            pltpu.VMEM((2, nb, 2 * B_BLOCK, B_BLOCK), BF16),
        ],
        compiler_params=_cparams(("arbitrary", "arbitrary")),
        name="moba_prompt",
    )(q, k, v)


def _moba_select_body(pt_ref, q_ref, *refs, npg, ppb, n_blocks):
    pages = refs[:npg]
    idx_ref = refs[npg]
    mt_ref = refs[npg + 1]
    s = pl.program_id(1)
    t_new, width = q_ref.shape
    bps = npg // ppb

    @pl.when(s == 0)
    def _():
        mt_ref[...] = jnp.zeros(mt_ref.shape, F32)

    lane_w = _iota((width, LANES), 1)
    cur = mt_ref[...]
    for blk in range(bps):
        tot = pages[blk * ppb][...]
        for r in range(1, ppb):
            tot = tot + pages[blk * ppb + r][...]
        col = jnp.sum(tot, axis=1, keepdims=True) * (1.0 / B_BLOCK)
        cur = jnp.where(lane_w == s * bps + blk, col, cur)
    mt_ref[...] = cur

    @pl.when(s == pl.num_programs(1) - 1)
    def _():
        ht = B_HEADS * t_new
        rr = _iota((ht, width), 0)
        cc = _iota((ht, width), 1)
        q = q_ref[...]
        qst = jnp.where((rr // t_new) == (cc // HEAD_DIM), jnp.concatenate([q] * B_HEADS, axis=0), 0.0)
        gate = jnp.dot(qst, cur, precision=HIGHEST, preferred_element_type=F32)
        lane = _iota((ht, LANES), 1)
        g = jnp.where(lane < n_blocks, gate, -jnp.inf)
        out = jnp.full((ht, LANES), -1, jnp.int32)
        for r in range(B_TOPK):
            mx = jnp.max(g, axis=1, keepdims=True)
            idx = jnp.min(jnp.where(g == mx, lane, LANES), axis=1, keepdims=True)
            out = jnp.where(lane == r, jnp.where(mx > -jnp.inf, idx, -1), out)
            g = jnp.where(lane == idx, -jnp.inf, g)
        idx_ref[...] = out


def _moba_select(q, row_block0, pool_kt, page_table, *, npg=32):
    bsz, n_pages = page_table.shape
    _, width, page = pool_kt.shape
    ppb = B_BLOCK // page
    t_new = 8
    ht = B_HEADS * t_new
    page_specs = [
        pl.BlockSpec((None, width, page), lambda b, s, pt, r=r: (pt[b, s * npg + r], 0, 0))
        for r in range(npg)
    ]
    return pl.pallas_call(
        functools.partial(_moba_select_body, npg=npg, ppb=ppb, n_blocks=n_pages // ppb),
        grid_spec=pltpu.PrefetchScalarGridSpec(
            num_scalar_prefetch=1,
            grid=(bsz, n_pages // npg),
            in_specs=[pl.BlockSpec((t_new, width), lambda b, s, pt: (row_block0 + b, 0))] + page_specs,
            out_specs=pl.BlockSpec((ht, LANES), lambda b, s, pt: (b, 0)),
            scratch_shapes=[pltpu.VMEM((width, LANES), F32)],
        ),
        out_shape=jax.ShapeDtypeStruct((bsz * ht, LANES), jnp.int32),
        compiler_params=_cparams(("arbitrary", "arbitrary")),
        name="moba_select",
    )(page_table, q, *([pool_kt] * npg))


def _moba_gather_body(pages_ref, sel_ref, qt_ref, knt_ref, vnt_ref, pool_k, pool_v, o_ref,
                      kbuf, vbuf, sem, *, ppb):
    b = pl.program_id(0)
    h = pl.program_id(1)
    n_heads = pl.num_programs(1)
    t_new = qt_ref.shape[1]
    n_chunk = t_new * B_TOPK * ppb
    step = b * n_heads + h
    slot = step % 2

    def chunk_copies(bb, hh, c, to_slot):
        page = pages_ref[bb, hh * n_chunk + c]
        return (pltpu.make_async_copy(pool_k.at[page, hh], kbuf.at[to_slot, c], sem.at[to_slot, 0]),
                pltpu.make_async_copy(pool_v.at[page, hh], vbuf.at[to_slot, c], sem.at[to_slot, 1]))

    def start_all(bb, hh, to_slot):
        def body(c, carry):
            for cp in chunk_copies(bb, hh, c, to_slot):
                cp.start()
            return carry
        lax.fori_loop(0, n_chunk, body, 0)

    @pl.when(step == 0)
    def _():
        start_all(b, h, slot)

    nxt = step + 1

    @pl.when(nxt < pl.num_programs(0) * n_heads)
    def _():
        start_all(nxt // n_heads, nxt % n_heads, 1 - slot)

    def wait_body(c, carry):
        for cp in chunk_copies(b, h, c, slot):
            cp.wait()
        return carry

    lax.fori_loop(0, n_chunk, wait_body, 0)
    kch = [kbuf.at[slot, c] for c in range(n_chunk)]
    vch = [vbuf.at[slot, c] for c in range(n_chunk)]
    qt = qt_ref[...] * QK_SCALE
    knt = knt_ref[...]
    vnt = vnt_ref[...]
    lane_n = _iota((1, t_new), 1)
    lane_o = _iota((HEAD_DIM, t_new), 1)
    out = jnp.zeros((HEAD_DIM, t_new), F32)
    for t in range(t_new):
        qcol = qt[:, t:t + 1]
        s_own = jnp.where(lane_n <= t, jnp.sum(qcol * knt, axis=0, keepdims=True), NEG)
        m = jnp.max(s_own, axis=1, keepdims=True)
        scores = []
        for r in range(B_TOPK):
            picked = sel_ref[b, (h * t_new + t) * B_TOPK + r] >= 0
            for pg in range(ppb):
                kc = kch[(t * B_TOPK + r) * ppb + pg][...]
                s = jnp.where(picked, jnp.sum(qcol * kc, axis=0, keepdims=True), NEG)
                m = jnp.maximum(m, jnp.max(s, axis=1, keepdims=True))
                scores.append(s)
        p_own = jnp.exp(s_own - m)
        l = jnp.sum(p_own, axis=1, keepdims=True)
        acc = jnp.sum(p_own * vnt, axis=1, keepdims=True)
        accv = jnp.zeros((HEAD_DIM, LANES), F32)
        for c, s in enumerate(scores):
            p = jnp.exp(s - m)
            l = l + jnp.sum(p, axis=1, keepdims=True)
            accv = accv + p * vch[t * B_TOPK * ppb + c][...]
        acc = acc + jnp.sum(accv, axis=1, keepdims=True)
        out = jnp.where(lane_o == t, acc / l, out)
    o_ref[...] = out


def _moba_gather(qt, knt, vnt, pool_kt, pool_vt, page_table, sel):
    bsz, width, t_new = qt.shape
    _, n_heads, hd, page = pool_kt.shape
    ppb = B_BLOCK // page
    n_chunk = t_new * B_TOPK * ppb
    new_spec = pl.BlockSpec((None, hd, t_new), lambda b, h, pt, sl: (b, h, 0))
    hbm = pl.BlockSpec(memory_space=pl.ANY)
    return pl.pallas_call(
        functools.partial(_moba_gather_body, ppb=ppb),
        grid_spec=pltpu.PrefetchScalarGridSpec(
            num_scalar_prefetch=2,
            grid=(bsz, n_heads),
            in_specs=[new_spec, new_spec, new_spec, hbm, hbm],
            out_specs=pl.BlockSpec((None, hd, t_new), lambda b, h, pt, sl: (b, h, 0)),
            scratch_shapes=[
                pltpu.VMEM((2, n_chunk, hd, page), F32),
                pltpu.VMEM((2, n_chunk, hd, page), F32),
                pltpu.SemaphoreType.DMA((2, 2)),
            ],
        ),
        out_shape=jax.ShapeDtypeStruct((bsz, width, t_new), F32),
        compiler_params=_cparams(("arbitrary", "arbitrary")),
        name="moba_gather",
    )(page_table, sel, qt, knt, vnt, pool_kt, pool_vt)


def _silu(x):
    return x / (1.0 + jnp.exp(-x))


def _ffn_even_body(x_ref, a_ref, b_ref, wo_ref, g_ref, wg_ref, wu_ref, wd_ref, o_ref, hn_ref, acc_ref):
    f = pl.program_id(1)
    aw = a_ref.shape[1]

    @pl.when(f == 0)
    def _():
        x1 = (x_ref[...]
              + jnp.dot(a_ref[...], wo_ref[:aw, :], preferred_element_type=F32)
              + jnp.dot(b_ref[...], wo_ref[aw:, :], preferred_element_type=F32))
        acc_ref[...] = x1
        hn_ref[...] = _rms(x1, g_ref[...]).astype(BF16)

    hn = hn_ref[...]
    gate = jnp.dot(hn, wg_ref[...], preferred_element_type=F32)
    up = jnp.dot(hn, wu_ref[...], preferred_element_type=F32)
    act = (_silu(gate) * up).astype(BF16)
    acc_ref[...] += jnp.dot(act, wd_ref[...], preferred_element_type=F32)

    @pl.when(f == pl.num_programs(1) - 1)
    def _():
        o_ref[...] = acc_ref[...]


def _ffn_even(x, a, b, wo16, g, wg16, wu16, wd16, *, tm=640, tf=1408):
    rows, dm = x.shape
    aw = a.shape[1]
    bw = b.shape[1]
    dff = wg16.shape[1]
    return pl.pallas_call(
        _ffn_even_body,
        grid=(rows // tm, dff // tf),
        in_specs=[
            pl.BlockSpec((tm, dm), lambda i, f: (i, 0)),
            pl.BlockSpec((tm, aw), lambda i, f: (i, 0)),
            pl.BlockSpec((tm, bw), lambda i, f: (i, 0)),
            pl.BlockSpec((aw + bw, dm), lambda i, f: (0, 0)),
            pl.BlockSpec((1, dm), lambda i, f: (0, 0)),
            pl.BlockSpec((dm, tf), lambda i, f: (0, f)),
            pl.BlockSpec((dm, tf), lambda i, f: (0, f)),
            pl.BlockSpec((tf, dm), lambda i, f: (f, 0)),
        ],
        out_specs=pl.BlockSpec((tm, dm), lambda i, f: (i, 0)),
        out_shape=jax.ShapeDtypeStruct((rows, dm), F32),
        scratch_shapes=[pltpu.VMEM((tm, dm), BF16), pltpu.VMEM((tm, dm), F32)],
        compiler_params=_cparams(("arbitrary", "arbitrary")),
        name="ffn_even",
    )(x, a, b, wo16, g, wg16, wu16, wd16)


def _odd_in_body(x_ref, g_ref, w_ref, q_ref, k_ref, v_ref, kt_ref, vt_ref, *, n_prompt_tiles):
    cw = q_ref.shape[1]
    h = _rms(x_ref[...], g_ref[...]).astype(BF16)
    y = jnp.dot(h, w_ref[...], preferred_element_type=F32)
    q_ref[...] = y[:, :cw]
    k_ref[...] = y[:, cw:2 * cw]
    v_ref[...] = y[:, 2 * cw:]
    _feature_major_kv(pl.program_id(0), n_prompt_tiles, y[:, cw:2 * cw], y[:, 2 * cw:], kt_ref, vt_ref)


def _odd_in(x, bsz, t_len, g, w16, *, tm=256):
    rows, dm = x.shape
    cw = w16.shape[1] // 3
    npt = bsz * t_len // tm
    out = jax.ShapeDtypeStruct((rows, cw), F32)
    out_t = jax.ShapeDtypeStruct((bsz, cw, t_len), F32)
    spec = pl.BlockSpec((tm, cw), lambda i: (i, 0))
    fm_spec = _feature_major_spec(cw, tm, npt, t_len // tm)
    return pl.pallas_call(
        functools.partial(_odd_in_body, n_prompt_tiles=npt),
        grid=(rows // tm,),
        in_specs=[
            pl.BlockSpec((tm, dm), lambda i: (i, 0)),
            pl.BlockSpec((1, dm), lambda i: (0, 0)),
            pl.BlockSpec((dm, 3 * cw), lambda i: (0, 0)),
        ],
        out_specs=[spec, spec, spec, fm_spec, fm_spec],
        out_shape=[out, out, out, out_t, out_t],
        compiler_params=_cparams(("arbitrary",)),
        name="odd_in",
    )(x, g, w16)


def _dilated_prompt_body(*refs, patterns, tiles):
    q_refs, k_refs, v_refs = (refs[g * tiles:(g + 1) * tiles] for g in range(3))
    o_ref = refs[3 * tiles]
    state = refs[3 * tiles + 1:]
    t_len = q_refs[0].shape[0]
    blk = 128
    lo = _iota((blk, LANES), 1) < HEAD_DIM
    n_br = len(patterns)
    n_iter = t_len // blk
    masks = []
    for window, dil in patterns:
        assert window // dil == blk
        has_prev = t_len // (dil * blk) > 1
        n_keys = 2 * blk if has_prev else blk
        qrow = _iota((2 * blk, n_keys), 0) % blk
        kcol = _iota((2 * blk, n_keys), 1)
        if has_prev:
            masks.append(((kcol >= blk) & ((kcol - blk) <= qrow), (kcol < blk) & (kcol >= qrow)))
        else:
            masks.append((kcol <= qrow, None))

    def body(it, carry):
        for bi, (window, dil) in enumerate(patterns):
            nblk = t_len // (dil * blk)
            cur_ok, prev_ok = masks[bi]
            cls = it // nblk
            ib = it % nblk

            def rows_at(block, cls=cls, dil=dil):
                start = cls + dil * blk * block
                if dil == 1:
                    return pl.ds(pl.multiple_of(start, blk), blk)
                return pl.ds(start, blk, stride=dil)

            rows = rows_at(ib)
            prows = rows_at(jnp.maximum(ib - 1, 0))
            for tile in range(tiles):
                q_ref, k_ref, v_ref = q_refs[tile], k_refs[tile], v_refs[tile]
                lse_s, out_s = state[2 * (tile * n_br + bi):2 * (tile * n_br + bi) + 2]
                q = q_ref[rows, :] * QK_SCALE
                qst = jnp.concatenate([jnp.where(lo, q, 0.0), jnp.where(lo, 0.0, q)], axis=0).astype(BF16)
                kk = k_ref[rows, :].astype(BF16)
                vv = v_ref[rows, :].astype(BF16)
                ok = cur_ok
                if prev_ok is not None:
                    kk = jnp.concatenate([k_ref[prows, :].astype(BF16), kk], axis=0)
                    vv = jnp.concatenate([v_ref[prows, :].astype(BF16), vv], axis=0)
                    ok = cur_ok | (prev_ok & (ib > 0))
                s = jnp.where(ok, lax.dot_general(qst, kk, NT, preferred_element_type=F32), NEG)
                m = jnp.max(s, axis=1, keepdims=True)
                p = jnp.exp(s - m)
                l = jnp.sum(p, axis=1, keepdims=True)
                pv = jnp.dot(p.astype(BF16), vv, preferred_element_type=F32)
                lse = m + jnp.log(l)
                outn = pv / l
                lse_s[rows, :] = jnp.where(lo, lse[:blk], lse[blk:])
                out_s[rows, :] = jnp.where(lo, outn[:blk], outn[blk:])
        return carry

    lax.fori_loop(0, n_iter, body, 0)

    chunk = 128

    def merge(c, carry):
        rows = pl.ds(pl.multiple_of(c * chunk, chunk), chunk)
        for tile in range(tiles):
            st = state[2 * tile * n_br:2 * (tile + 1) * n_br]
            ms = [st[2 * r][rows, :] for r in range(n_br)]
            m = functools.reduce(jnp.maximum, ms)
            num = jnp.zeros((chunk, LANES), F32)
            den = jnp.zeros((chunk, LANES), F32)
            for r in range(n_br):
                w = jnp.exp(ms[r] - m)
                num = num + w * st[2 * r + 1][rows, :]
                den = den + w
            o_ref[rows, tile * LANES:(tile + 1) * LANES] = (num / den).astype(BF16)
        return carry

    lax.fori_loop(0, t_len // chunk, merge, 0)


def _dilated_prompt(q, k, v, bsz, t_len, *, tiles=4):
    width = q.shape[1]
    specs = [pl.BlockSpec((t_len, LANES), lambda b, j, t=t: (b, j * tiles + t)) for t in range(tiles)]
    return pl.pallas_call(
        functools.partial(_dilated_prompt_body, patterns=C_PATTERNS, tiles=tiles),
        grid=(bsz, width // (tiles * LANES)),
        in_specs=specs * 3,
        out_specs=pl.BlockSpec((t_len, tiles * LANES), lambda b, j: (b, j)),
        out_shape=jax.ShapeDtypeStruct((bsz * t_len, width), BF16),
        scratch_shapes=[pltpu.VMEM((t_len, LANES), F32)] * (2 * len(C_PATTERNS) * tiles),
        compiler_params=_cparams(("arbitrary", "arbitrary")),
        name="dilated_prompt",
    )(*([q] * tiles), *([k] * tiles), *([v] * tiles))


def _dilated_sample_body(q_ref, kn_ref, vn_ref, kc_ref, vc_ref, o_ref, *, patterns):
    t_new = q_ref.shape[0]
    w_len = kc_ref.shape[1]
    nrow = 2 * t_new
    lane = _iota((t_new, LANES), 1)
    q = q_ref[...] * QK_SCALE
    qst = jnp.concatenate([jnp.where(lane < HEAD_DIM, q, 0.0), jnp.where(lane >= HEAD_DIM, q, 0.0)], axis=0)
    qst = qst.astype(BF16)

    def mult(delta):
        cnt = jnp.zeros(delta.shape, F32)
        for window, dil in patterns:
            hit = (delta >= 0) & (delta <= window) & ((delta % dil) == 0)
            cnt = cnt + hit.astype(F32)
        return cnt

    t_c = _iota((nrow, w_len), 0) % t_new
    w_c = mult(w_len + t_c - _iota((nrow, w_len), 1))
    t_n = _iota((nrow, LANES), 0) % t_new
    c_n = _iota((nrow, LANES), 1)
    w_n = jnp.where(c_n < t_new, mult(t_n - c_n), 0.0)

    pad = jnp.zeros((LANES - t_new, LANES), F32)
    kn = jnp.concatenate([kn_ref[...], pad], axis=0).astype(BF16)
    vn = jnp.concatenate([vn_ref[...], pad], axis=0).astype(BF16)
    s_c = jnp.dot(qst, kc_ref[...].astype(BF16), preferred_element_type=F32)
    s_n = lax.dot_general(qst, kn, NT, preferred_element_type=F32)
    s_c = jnp.where(w_c > 0, s_c, NEG)
    s_n = jnp.where(w_n > 0, s_n, NEG)
    m = jnp.maximum(jnp.max(s_c, axis=1, keepdims=True), jnp.max(s_n, axis=1, keepdims=True))
    p_c = w_c * jnp.exp(s_c - m)
    p_n = w_n * jnp.exp(s_n - m)
    l = jnp.sum(p_c, axis=1, keepdims=True) + jnp.sum(p_n, axis=1, keepdims=True)
    acc = (lax.dot_general(p_c.astype(BF16), vc_ref[...].astype(BF16), NT, preferred_element_type=F32)
           + jnp.dot(p_n.astype(BF16), vn, preferred_element_type=F32))
    out = acc / l
    o_ref[...] = jnp.where(lane < HEAD_DIM, out[:t_new], out[t_new:])


def _dilated_sample(q, k, v, row_block0, cache_kt, cache_vt):
    bsz, width, w_len = cache_kt.shape
    t_new = 8
    new_spec = pl.BlockSpec((t_new, LANES), lambda b, j: (row_block0 + b, j))
    cache_spec = pl.BlockSpec((None, LANES, w_len), lambda b, j: (b, j, 0))
    return pl.pallas_call(
        functools.partial(_dilated_sample_body, patterns=C_PATTERNS),
        grid=(bsz, width // LANES),
        in_specs=[new_spec, new_spec, new_spec, cache_spec, cache_spec],
        out_specs=pl.BlockSpec((t_new, LANES), lambda b, j: (b, j)),
        out_shape=jax.ShapeDtypeStruct((bsz * t_new, width), F32),
        compiler_params=_cparams(("arbitrary", "arbitrary")),
        name="dilated_sample",
    )(q, k, v, cache_kt, cache_vt)


def _moe_route_body(x_ref, c_ref, wo_ref, g_ref, r_ref,
                    x3_ref, hn_ref, comb_ref, pos_ref, post_ref, cnt_ref, before_ref):
    tb = x_ref.shape[0]

    @pl.when(pl.program_id(0) == 0)
    def _():
        before_ref[...] = jnp.where(_iota((tb, tb), 1) < _iota((tb, tb), 0), 1.0, 0.0).astype(BF16)

    x3 = x_ref[...] + jnp.dot(c_ref[...], wo_ref[...], preferred_element_type=F32)
    x3_ref[...] = x3
    hf = _rms(x3, g_ref[...])
    hn_ref[...] = hf.astype(BF16)
    lane = _iota((tb, LANES), 1)
    logits = jnp.dot(hf, r_ref[...], precision=HIGHEST, preferred_element_type=F32)
    lg = jnp.where(lane < N_EXPERTS, logits, -jnp.inf)
    m1 = jnp.max(lg, axis=1, keepdims=True)
    p1 = lane == jnp.min(jnp.where(lg == m1, lane, LANES), axis=1, keepdims=True)
    lg2 = jnp.where(p1, -jnp.inf, lg)
    m2 = jnp.max(lg2, axis=1, keepdims=True)
    p2 = lane == jnp.min(jnp.where(lg2 == m2, lane, LANES), axis=1, keepdims=True)
    e2 = jnp.exp(m2 - m1)
    den = 1.0 + e2
    comb_ref[...] = jnp.where(p1, 1.0 / den, 0.0) + jnp.where(p2, e2 / den, 0.0)
    routed = p1 | p2
    ind = jnp.where(routed, 1.0, 0.0)
    slot = jnp.where(routed, jnp.dot(before_ref[...], ind.astype(BF16), preferred_element_type=F32), -1.0)
    pos_ref[...] = slot
    eye = jnp.where(_iota((LANES, LANES), 0) == _iota((LANES, LANES), 1), 1.0, 0.0)
    post_ref[...] = lax.dot_general(eye, slot, NT, precision=HIGHEST, preferred_element_type=F32)
    cnt = jnp.sum(ind, axis=0, keepdims=True)
    cnt_ref[...] = jnp.broadcast_to(cnt, cnt_ref.shape).astype(jnp.int32)


def _moe_route(x, c, wo16, g, router_pad, *, tb):
    rows, dm = x.shape
    nblk = rows // tb
    row_spec = lambda width: pl.BlockSpec((tb, width), lambda i: (i, 0))
    return pl.pallas_call(
        _moe_route_body,
        grid=(nblk,),
        in_specs=[
            row_spec(dm), row_spec(dm),
            pl.BlockSpec((dm, dm), lambda i: (0, 0)),
            pl.BlockSpec((1, dm), lambda i: (0, 0)),
            pl.BlockSpec((dm, LANES), lambda i: (0, 0)),
        ],
        out_specs=[
            row_spec(dm), row_spec(dm), row_spec(LANES), row_spec(LANES),
            pl.BlockSpec((None, LANES, tb), lambda i: (i, 0, 0)),
            pl.BlockSpec((None, 8, LANES), lambda i: (i, 0, 0)),
        ],
        out_shape=[
            jax.ShapeDtypeStruct((rows, dm), F32),
            jax.ShapeDtypeStruct((rows, dm), BF16),
            jax.ShapeDtypeStruct((rows, LANES), F32),
            jax.ShapeDtypeStruct((rows, LANES), F32),
            jax.ShapeDtypeStruct((nblk, LANES, tb), F32),
            jax.ShapeDtypeStruct((nblk, 8, LANES), jnp.int32),
        ],
        scratch_shapes=[pltpu.VMEM((tb, tb), BF16)],
        compiler_params=_cparams(("arbitrary",)),
        name="moe_route",
    )(x, c, wo16, g, router_pad)


def _moe_expert_body(cnt_ref, acc_ref, hn_ref, comb_ref, pos_ref, post_ref, gf_ref,
                     wg_ref, wu_ref, wd_ref, o_ref, *, e, last, cap):
    i = pl.program_id(0)
    tb = hn_ref.shape[0]
    n_sub = (cnt_ref[i, e] + cap - 1) // cap
    lane = _iota((tb, LANES), 1)
    w_col = jnp.sum(jnp.where(lane == e, comb_ref[...], 0.0), axis=1, keepdims=True)
    slot_col = jnp.sum(jnp.where(lane == e, pos_ref[...], 0.0), axis=1, keepdims=True)
    slot_row = post_ref[pl.ds(e, 1), :]
    sub_of_row = _iota((cap, tb), 0).astype(F32)
    sub_of_col = _iota((tb, cap), 1).astype(F32)
    o_ref[...] = acc_ref[...]

    def body(s, carry):
        base = (s * cap).astype(F32)
        gather = jnp.where(slot_row - base == sub_of_row, 1.0, 0.0).astype(BF16)
        xe = jnp.dot(gather, hn_ref[...], preferred_element_type=F32).astype(BF16)
        gate = jnp.dot(xe, wg_ref[...], preferred_element_type=F32)
        up = jnp.dot(xe, wu_ref[...], preferred_element_type=F32)
        act = (_silu(gate) * up).astype(BF16)
        y = jnp.dot(act, wd_ref[...], preferred_element_type=F32).astype(BF16)
        scatter = jnp.where(slot_col - base == sub_of_col, 1.0, 0.0).astype(BF16)
        o_ref[...] += w_col * jnp.dot(scatter, y, preferred_element_type=F32)
        return carry

    lax.fori_loop(0, n_sub, body, 0)
    if last:
        o_ref[...] = _rms(o_ref[...], gf_ref[...])


def _moe_experts(counts, x3, hn, comb, pos, post, g_final, wg16, wu16, wd16, *, tb, cap):
    rows, dm = x3.shape
    n_exp, _, dff = wg16.shape
    once = pl.Buffered(1)
    acc = x3
    for e in range(n_exp):
        acc = pl.pallas_call(
            functools.partial(_moe_expert_body, e=e, last=(e == n_exp - 1), cap=cap),
            grid_spec=pltpu.PrefetchScalarGridSpec(
                num_scalar_prefetch=1,
                grid=(rows // tb,),
                in_specs=[
                    pl.BlockSpec((tb, dm), lambda i, cnt: (i, 0)),
                    pl.BlockSpec((tb, dm), lambda i, cnt: (i, 0)),
                    pl.BlockSpec((tb, LANES), lambda i, cnt: (i, 0)),
                    pl.BlockSpec((tb, LANES), lambda i, cnt: (i, 0)),
                    pl.BlockSpec((None, LANES, tb), lambda i, cnt: (i, 0, 0)),
                    pl.BlockSpec((1, dm), lambda i, cnt: (0, 0)),
                    pl.BlockSpec((None, dm, dff), lambda i, cnt, e=e: (e, 0, 0), pipeline_mode=once),
                    pl.BlockSpec((None, dm, dff), lambda i, cnt, e=e: (e, 0, 0), pipeline_mode=once),
                    pl.BlockSpec((None, dff, dm), lambda i, cnt, e=e: (e, 0, 0), pipeline_mode=once),
                ],
                out_specs=pl.BlockSpec((tb, dm), lambda i, cnt: (i, 0)),
            ),
            out_shape=jax.ShapeDtypeStruct((rows, dm), F32),
            compiler_params=_cparams(("arbitrary",)),
            name=f"moe_expert{e}",
        )(counts, acc, hn, comb, pos, post, g_final, wg16, wu16, wd16)
    return acc


def _moe(x, c, wo16, g, router_pad, g_final, wg16, wu16, wd16, *, tb=416, cap=128):
    x3, hn, comb, pos, post, cnt = _moe_route(x, c, wo16, g, router_pad, tb=tb)
    counts = cnt[:, 0, :N_EXPERTS]
    return _moe_experts(counts, x3, hn, comb, pos, post, g_final, wg16, wu16, wd16, tb=tb, cap=cap)


def kernel(x_prompt, x_sample, cache_b_k, cache_b_v, cache_c_k, cache_c_v, page_table, norm_mix, norm_ffn,
           norm_final, w_in_even, w_out_even, a_ln_g, a_ln_b, a_ws, a_bs, ffn_gate, ffn_up, ffn_down,
           w_in_odd, w_out_odd, router, moe_gate, moe_up, moe_down):
    bsz, t_len, dm = x_prompt.shape
    dbsz, t_new, _ = x_sample.shape
    assert norm_mix.shape[0] == 2 and w_in_even.shape[0] == 1 and w_in_odd.shape[0] == 1
    assert cache_b_k.shape[0] == 1 and cache_c_k.shape[0] == 1
    assert t_new == 8 and A_CHUNK % t_new == 0
    n_p = bsz * t_len
    n_s = dbsz * t_new
    aw = a_ln_g.shape[-1]
    n_groups = aw // HEAD_DIM
    seq_per_chunk = A_CHUNK // t_new

    x = jnp.concatenate([x_prompt.reshape(n_p, dm), x_sample.reshape(n_s, dm)], axis=0)

    ws = a_ws[0]
    ws_sample = jnp.tile(ws[:, :t_new, :t_new], (1, seq_per_chunk, seq_per_chunk))
    ws2 = jnp.stack([ws, ws_sample])
    bias_p = jnp.repeat(a_bs[0].T, HEAD_DIM, axis=1)
    bias_s = jnp.repeat(jnp.tile(a_bs[0][:, :t_new].T, (seq_per_chunk, 1)), HEAD_DIM, axis=1)
    bias2 = jnp.stack([bias_p, bias_s])
    row = lambda vec: vec.reshape(1, -1)
    router_pad = jnp.pad(router[0], ((0, 0), (0, LANES - router.shape[-1])))

    w_even = w_in_even[0].astype(BF16)
    a_out, v_a, q_b, k_b, v_b, kt_b, vt_b, v_a_last = _even_in(
        x, bsz, t_len, row(norm_mix[0]), w_even, row(a_ln_g[0]), row(a_ln_b[0]), ws2, bias2)
    b_prompt = _moba_prompt(q_b, k_b, v_b, bsz, t_len)
    page = cache_b_k.shape[2]
    bw = q_b.shape[1]
    feature_major = lambda c: jnp.transpose(c, (0, 1, 3, 4, 2))
    pool_kt = feature_major(cache_b_k).reshape(-1, B_HEADS, HEAD_DIM, page)
    pool_vt = feature_major(cache_b_v).reshape(-1, B_HEADS, HEAD_DIM, page)
    sel = _moba_select(q_b, n_p // t_new, pool_kt.reshape(-1, bw, page), page_table)
    sel = sel.reshape(dbsz, B_HEADS * t_new, LANES)[:, :, :B_TOPK].reshape(dbsz, -1)
    new_t = lambda a: jnp.transpose(a[n_p:].reshape(dbsz, t_new, -1), (0, 2, 1))
    ppb = B_BLOCK // page
    logical = (jnp.maximum(sel, 0)[:, :, None] * ppb + jnp.arange(ppb, dtype=jnp.int32)).reshape(dbsz, -1)
    hit = logical[:, :, None] == jnp.arange(page_table.shape[1], dtype=jnp.int32)
    sel_pages = jnp.sum(jnp.where(hit, page_table[:, None, :], 0), axis=-1)
    b_sample_t = _moba_gather(new_t(q_b), new_t(k_b), new_t(v_b), pool_kt, pool_vt, sel_pages, sel)
    b_sample = jnp.transpose(b_sample_t, (0, 2, 1)).reshape(n_s, bw)
    b_out = jnp.concatenate([b_prompt, b_sample.astype(BF16)], axis=0)
    x2 = _ffn_even(x, a_out, b_out, w_out_even[0].astype(BF16), row(norm_ffn[0]),
                   ffn_gate[0].astype(BF16), ffn_up[0].astype(BF16), ffn_down[0].astype(BF16))

    w_odd = w_in_odd[0].astype(BF16)
    q_c, k_c, v_c, kt_c, vt_c = _odd_in(x2, bsz, t_len, row(norm_mix[1]), w_odd)
    c_prompt = _dilated_prompt(q_c, k_c, v_c, bsz, t_len)
    cw = q_c.shape[1]
    w_len = cache_c_k.shape[2]
    c_sample = _dilated_sample(q_c, k_c, v_c, n_p // t_new,
                               feature_major(cache_c_k).reshape(-1, cw, w_len),
                               feature_major(cache_c_v).reshape(-1, cw, w_len))
    c_out = jnp.concatenate([c_prompt, c_sample.astype(BF16)], axis=0)
    y = _moe(x2, c_out, w_out_odd[0].astype(BF16), row(norm_ffn[1]), router_pad, row(norm_final),
             moe_gate[0].astype(BF16), moe_up[0].astype(BF16), moe_down[0].astype(BF16))

    bh = (B_HEADS, HEAD_DIM)
    ch = (cw // HEAD_DIM, HEAD_DIM)
    assert t_len % 256 == 0
    c_keep_p = min(C_PATTERNS[-1][0], t_len)
    leaf = lambda a_t, hd: jnp.transpose(a_t.reshape(1, bsz, *hd, a_t.shape[-1]), (0, 1, 4, 2, 3))
    return (
        y[:n_p].reshape(bsz, t_len, dm),
        y[n_p:].reshape(dbsz, t_new, dm),
        v_a_last.reshape(1, bsz, A_CHUNK, aw),
        v_a[n_p:].reshape(1, dbsz, t_new, aw),
        leaf(kt_b, bh),
        leaf(vt_b, bh),
        k_b[n_p:].reshape(1, dbsz, t_new, *bh),
        v_b[n_p:].reshape(1, dbsz, t_new, *bh),
        leaf(kt_c[:, :, t_len - c_keep_p:], ch),
        leaf(vt_c[:, :, t_len - c_keep_p:], ch),
        k_c[n_p:].reshape(1, dbsz, t_new, *ch),
        v_c[n_p:].reshape(1, dbsz, t_new, *ch),
    )
```

```python
import functools

import jax
import jax.numpy as jnp
from jax import lax
from jax.experimental import pallas as pl
from jax.experimental.pallas import tpu as pltpu

F32 = jnp.float32
BF16 = jnp.bfloat16
HIGHEST = lax.Precision.HIGHEST

HEAD_DIM = 64
LANES = 128
A_CHUNK = 128
B_HEADS = 8
B_BLOCK = 256
B_TOPK = 3
C_PATTERNS = ((128, 1), (512, 4), (2048, 16))
N_EXPERTS = 8
RMS_EPS = 1e-6
LN_EPS = 1e-5
NEG = -1e30
QK_SCALE = HEAD_DIM ** -0.5
VMEM_LIMIT = 56 * 1024 * 1024

NT = (((1,), (1,)), ((), ()))


def _cparams(sem):
    return pltpu.CompilerParams(dimension_semantics=sem, vmem_limit_bytes=VMEM_LIMIT)


def _rms(x, g):
    return x * lax.rsqrt(jnp.mean(x * x, axis=-1, keepdims=True) + RMS_EPS) * g


def _gelu(x):
    return 0.5 * x * (1.0 + lax.erf(x * 0.7071067811865476))


def _iota(shape, dim):
    return lax.broadcasted_iota(jnp.int32, shape, dim)


def _feature_major_kv(i, n_prompt_tiles, k_tile, v_tile, kt_ref, vt_ref):
    @pl.when(i < n_prompt_tiles)
    def _():
        kt_ref[...] = k_tile.T
        vt_ref[...] = v_tile.T


def _feature_major_spec(width, tm, n_prompt_tiles, tiles_per_seq):
    def imap(i):
        ic = jnp.minimum(i, n_prompt_tiles - 1)
        return (ic // tiles_per_seq, 0, ic % tiles_per_seq)
    return pl.BlockSpec((None, width, tm), imap)


def _even_in_body(x_ref, g_ref, w_ref, lng_ref, lnb_ref, ws_ref, bias_ref,
                  a_ref, v_ref, q_ref, k_ref, vv_ref, kt_ref, vt_ref, vlast_ref,
                  *, n_prompt_tiles, tiles_per_seq, aw, bw):
    i = pl.program_id(0)
    tm = x_ref.shape[0]
    h = _rms(x_ref[...], g_ref[...]).astype(BF16)
    y = jnp.dot(h, w_ref[...], preferred_element_type=F32)
    _feature_major_kv(i, n_prompt_tiles, y[:, 2 * aw + bw:2 * aw + 2 * bw], y[:, 2 * aw + 2 * bw:], kt_ref, vt_ref)
    u = _gelu(y[:, :aw])
    v = _gelu(y[:, aw:2 * aw])
    mu = jnp.mean(v, axis=-1, keepdims=True)
    vc = v - mu
    var = jnp.mean(vc * vc, axis=-1, keepdims=True)
    v = vc * lax.rsqrt(var + LN_EPS) * lng_ref[...] + lnb_ref[...]
    v_ref[...] = v

    @pl.when((i < n_prompt_tiles) & (i % tiles_per_seq == tiles_per_seq - 1))
    def _():
        vlast_ref[...] = v[tm - A_CHUNK:, :]
    q_ref[...] = y[:, 2 * aw:2 * aw + bw]
    k_ref[...] = y[:, 2 * aw + bw:2 * aw + 2 * bw]
    vv_ref[...] = y[:, 2 * aw + 2 * bw:]

    row = _iota((A_CHUNK, A_CHUNK), 0)
    col = _iota((A_CHUNK, A_CHUNK), 1)
    is_prompt = i < n_prompt_tiles
    ok = (col <= row) & (((row // 8) == (col // 8)) | is_prompt)
    lo = _iota((A_CHUNK, LANES), 1) < HEAD_DIM
    v16 = v.astype(BF16)
    n_groups = aw // HEAD_DIM
    wm = [jnp.where(ok, ws_ref[g], 0.0).astype(BF16) for g in range(n_groups)]
    for c in range(tm // A_CHUNK):
        rs = slice(c * A_CHUNK, (c + 1) * A_CHUNK)
        for j in range(aw // LANES):
            cs = slice(j * LANES, (j + 1) * LANES)
            vj = v16[rs, cs]
            ma = jnp.dot(wm[2 * j], vj, preferred_element_type=F32)
            mb = jnp.dot(wm[2 * j + 1], vj, preferred_element_type=F32)
            mixed = jnp.where(lo, ma, mb) + bias_ref[:, cs]
            a_ref[rs, cs] = (u[rs, cs] * mixed).astype(BF16)


def _even_in(x, bsz, t_len, g, w16, lng, lnb, ws2, bias2, *, tm=256):
    rows, dm = x.shape
    aw = lng.shape[-1]
    n_in = w16.shape[1]
    bw = (n_in - 2 * aw) // 3
    n_groups = aw // HEAD_DIM
    npt = bsz * t_len // tm
    sel = lambda i: jnp.minimum(i // npt, 1)
    row_spec = lambda width: pl.BlockSpec((tm, width), lambda i: (i, 0))
    tps = t_len // tm
    fm_spec = _feature_major_spec(bw, tm, npt, tps)
    return pl.pallas_call(
        functools.partial(_even_in_body, n_prompt_tiles=npt, tiles_per_seq=tps, aw=aw, bw=bw),
        grid=(rows // tm,),
        in_specs=[
            row_spec(dm),
            pl.BlockSpec((1, dm), lambda i: (0, 0)),
            pl.BlockSpec((dm, n_in), lambda i: (0, 0)),
            pl.BlockSpec((1, aw), lambda i: (0, 0)),
            pl.BlockSpec((1, aw), lambda i: (0, 0)),
            pl.BlockSpec((None, n_groups, A_CHUNK, A_CHUNK), lambda i: (sel(i), 0, 0, 0)),
            pl.BlockSpec((None, A_CHUNK, aw), lambda i: (sel(i), 0, 0)),
        ],
        out_specs=[row_spec(aw), row_spec(aw), row_spec(bw), row_spec(bw), row_spec(bw), fm_spec, fm_spec,
                   pl.BlockSpec((None, A_CHUNK, aw), lambda i: (jnp.minimum(i, npt - 1) // tps, 0, 0))],
        out_shape=[
            jax.ShapeDtypeStruct((rows, aw), BF16),
            jax.ShapeDtypeStruct((rows, aw), F32),
            jax.ShapeDtypeStruct((rows, bw), F32),
            jax.ShapeDtypeStruct((rows, bw), F32),
            jax.ShapeDtypeStruct((rows, bw), F32),
            jax.ShapeDtypeStruct((bsz, bw, t_len), F32),
            jax.ShapeDtypeStruct((bsz, bw, t_len), F32),
            jax.ShapeDtypeStruct((bsz, A_CHUNK, aw), F32),
        ],
        compiler_params=_cparams(("arbitrary",)),
        name="even_in",
    )(x, g, w16, lng, lnb, ws2, bias2)


def _moba_prompt_body(q_ref, k_ref, v_ref, o_ref, k16, v16, mt_ref, s_scr, p_scr, *, nb):
    i = pl.program_id(1)
    t_len, width = k_ref.shape
    n_gate = B_HEADS * nb

    @pl.when(i == 0)
    def _():
        k = k_ref[...]
        k16[...] = k.astype(BF16)
        v16[...] = v_ref[...].astype(BF16)
        r = _iota((LANES, t_len), 0)
        key = _iota((LANES, t_len), 1)
        avg = jnp.where(((key // B_BLOCK) == (r % nb)) & (r < n_gate), 1.0 / B_BLOCK, 0.0).astype(F32)
        mt = jnp.dot(avg, k, precision=HIGHEST, preferred_element_type=F32)
        rr = _iota((LANES, width), 0)
        cc = _iota((LANES, width), 1)
        mt_ref[...] = jnp.where(((rr // nb) == (cc // HEAD_DIM)) & (rr < n_gate), mt, 0.0)

    q = q_ref[...]
    tq = q.shape[0]
    gate = lax.dot_general(q, mt_ref[...], NT, precision=HIGHEST, preferred_element_type=F32)
    lane = _iota((tq, LANES), 1)
    n_of = lane % nb
    valid = (n_of < i) & (lane < n_gate)
    gm = jnp.where(valid, gate, -jnp.inf)
    rank = jnp.zeros((tq, LANES), jnp.int32)
    for s in range(1, nb):
        fwd = pltpu.roll(gm, LANES - s, 1)
        bwd = pltpu.roll(gm, nb - s, 1)
        wrap = (n_of + s) >= nb
        other = jnp.where(wrap, bwd, fwd)
        m_idx = jnp.where(wrap, n_of + s - nb, n_of + s)
        beats = (other > gm) | ((other == gm) & (m_idx < n_of))
        rank = rank + beats.astype(jnp.int32)
    selbias = jnp.where(valid & (rank < B_TOPK), 0.0, NEG)

    row2 = _iota((2 * tq, B_BLOCK), 0) % tq
    col2 = _iota((2 * tq, B_BLOCK), 1)
    causal_bias = jnp.where(col2 <= row2, 0.0, NEG)
    lo = lane < HEAD_DIM
    own = pl.multiple_of(i * B_BLOCK, B_BLOCK)
    half = B_BLOCK // 2

    key_lane = _iota((B_BLOCK, LANES), 1)

    group = s_scr.shape[0]

    def zero_body(n, carry):
        for g in range(group):
            p_scr[g, n] = jnp.zeros((2 * tq, B_BLOCK), BF16)
        return carry

    lax.fori_loop(i + 1, nb, zero_body, 0)
    for j0 in range(0, width // LANES, group):
        cols = [slice((j0 + g) * LANES, (j0 + g + 1) * LANES) for g in range(group)]
        q_aug, m_tiles = [], []
        for g in range(group):
            j = j0 + g
            qj = q[:, cols[g]] * QK_SCALE
            qst = jnp.concatenate([jnp.where(lo, qj, 0.0), jnp.where(lo, 0.0, qj)], axis=0).astype(BF16)
            s = (lax.dot_general(qst, k16[pl.ds(own, B_BLOCK), cols[g]], NT, preferred_element_type=F32)
                 + causal_bias)
            s_scr[g, i] = s
            m_tiles.append(jnp.maximum(s[:, :half], s[:, half:]))
            sb = []
            for a in range(2):
                off = (2 * j + a) * nb
                moved = selbias if off == 0 else pltpu.roll(selbias, LANES - off, 1)
                sb.append(jnp.where(lane < nb, moved, 0.0))
            q_aug.append(jnp.concatenate([qst, jnp.concatenate(sb, axis=0).astype(BF16)], axis=1))

        def score_body(n, m_tiles, q_aug=q_aug, cols=cols):
            start = pl.multiple_of(n * B_BLOCK, B_BLOCK)
            onehot = jnp.where(key_lane == n, 1.0, 0.0).astype(BF16)
            out = []
            for g in range(group):
                k_aug = jnp.concatenate([k16[pl.ds(start, B_BLOCK), cols[g]], onehot], axis=1)
                s = lax.dot_general(q_aug[g], k_aug, NT, preferred_element_type=F32)
                s_scr[g, n] = s
                out.append(jnp.maximum(m_tiles[g], jnp.maximum(s[:, :half], s[:, half:])))
            return tuple(out)

        m_tiles = lax.fori_loop(0, i, score_body, tuple(m_tiles))
        ms = [jnp.max(m_tiles[g], axis=1, keepdims=True) for g in range(group)]

        def prob_body(n, l_tiles, ms=ms):
            out = []
            for g in range(group):
                p = jnp.exp(s_scr[g, n] - ms[g])
                p_scr[g, n] = p.astype(BF16)
                out.append(l_tiles[g] + p[:, :half] + p[:, half:])
            return tuple(out)

        l_tiles = lax.fori_loop(0, i + 1, prob_body,
                                tuple(jnp.zeros((2 * tq, half), F32) for _ in range(group)))
        for g in range(group):
            l = jnp.sum(l_tiles[g], axis=1, keepdims=True)
            acc = jnp.dot(p_scr[g, 0], v16[0:B_BLOCK, cols[g]], preferred_element_type=F32)
            for n in range(1, nb):
                acc = acc + jnp.dot(p_scr[g, n], v16[n * B_BLOCK:(n + 1) * B_BLOCK, cols[g]],
                                    preferred_element_type=F32)
            o = acc / l
            o_ref[:, cols[g]] = jnp.where(lo, o[:tq], o[tq:]).astype(BF16)


def _moba_prompt(q, k, v, bsz, t_len):
    width = q.shape[1]
    nb = t_len // B_BLOCK
    return pl.pallas_call(
        functools.partial(_moba_prompt_body, nb=nb),
        grid=(bsz, nb),
        in_specs=[
            pl.BlockSpec((B_BLOCK, width), lambda b, i: (b * nb + i, 0)),
            pl.BlockSpec((t_len, width), lambda b, i: (b, 0)),
            pl.BlockSpec((t_len, width), lambda b, i: (b, 0)),
        ],
        out_specs=pl.BlockSpec((B_BLOCK, width), lambda b, i: (b * nb + i, 0)),
        out_shape=jax.ShapeDtypeStruct((bsz * t_len, width), BF16),
        scratch_shapes=[
            pltpu.VMEM((t_len, width), BF16),
            pltpu.VMEM((t_len, width), BF16),
            pltpu.VMEM((LANES, width), F32),
            pltpu.VMEM((2, nb, 2 * B_BLOCK, B_BLOCK), F32),
            pltpu.VMEM((2, nb, 2 * B_BLOCK, B_BLOCK), BF16),
        ],
        compiler_params=_cparams(("arbitrary", "arbitrary")),
        name="moba_prompt",
    )(q, k, v)


def _moba_select_body(pt_ref, q_ref, *refs, npg, ppb, n_blocks):
    pages = refs[:npg]
    idx_ref = refs[npg]
    mt_ref = refs[npg + 1]
    s = pl.program_id(1)
    t_new, width = q_ref.shape
    bps = npg // ppb

    @pl.when(s == 0)
    def _():
        mt_ref[...] = jnp.zeros(mt_ref.shape, F32)

    lane_w = _iota((width, LANES), 1)
    cur = mt_ref[...]
    for blk in range(bps):
        tot = pages[blk * ppb][...]
        for r in range(1, ppb):
            tot = tot + pages[blk * ppb + r][...]
        col = jnp.sum(tot, axis=1, keepdims=True) * (1.0 / B_BLOCK)
        cur = jnp.where(lane_w == s * bps + blk, col, cur)
    mt_ref[...] = cur

    @pl.when(s == pl.num_programs(1) - 1)
    def _():
        ht = B_HEADS * t_new
        rr = _iota((ht, width), 0)
        cc = _iota((ht, width), 1)
        q = q_ref[...]
        qst = jnp.where((rr // t_new) == (cc // HEAD_DIM), jnp.concatenate([q] * B_HEADS, axis=0), 0.0)
        gate = jnp.dot(qst, cur, precision=HIGHEST, preferred_element_type=F32)
        lane = _iota((ht, LANES), 1)
        g = jnp.where(lane < n_blocks, gate, -jnp.inf)
        out = jnp.full((ht, LANES), -1, jnp.int32)
        for r in range(B_TOPK):
            mx = jnp.max(g, axis=1, keepdims=True)
            idx = jnp.min(jnp.where(g == mx, lane, LANES), axis=1, keepdims=True)
            out = jnp.where(lane == r, jnp.where(mx > -jnp.inf, idx, -1), out)
            g = jnp.where(lane == idx, -jnp.inf, g)
        idx_ref[...] = out


def _moba_select(q, row_block0, pool_kt, page_table, *, npg=32):
    bsz, n_pages = page_table.shape
    _, width, page = pool_kt.shape
    ppb = B_BLOCK // page
    t_new = 8
    ht = B_HEADS * t_new
    page_specs = [
        pl.BlockSpec((None, width, page), lambda b, s, pt, r=r: (pt[b, s * npg + r], 0, 0))
        for r in range(npg)
    ]
    return pl.pallas_call(
        functools.partial(_moba_select_body, npg=npg, ppb=ppb, n_blocks=n_pages // ppb),
        grid_spec=pltpu.PrefetchScalarGridSpec(
            num_scalar_prefetch=1,
            grid=(bsz, n_pages // npg),
            in_specs=[pl.BlockSpec((t_new, width), lambda b, s, pt: (row_block0 + b, 0))] + page_specs,
            out_specs=pl.BlockSpec((ht, LANES), lambda b, s, pt: (b, 0)),
            scratch_shapes=[pltpu.VMEM((width, LANES), F32)],
        ),
        out_shape=jax.ShapeDtypeStruct((bsz * ht, LANES), jnp.int32),
        compiler_params=_cparams(("arbitrary", "arbitrary")),
        name="moba_select",
    )(page_table, q, *([pool_kt] * npg))


def _moba_gather_body(pages_ref, sel_ref, qt_ref, knt_ref, vnt_ref, pool_k, pool_v, o_ref,
                      kbuf, vbuf, sem, *, ppb):
    b = pl.program_id(0)
    h = pl.program_id(1)
    n_heads = pl.num_programs(1)
    t_new = qt_ref.shape[1]
    n_chunk = t_new * B_TOPK * ppb
    step = b * n_heads + h
    slot = step % 2

    def chunk_copies(bb, hh, c, to_slot):
        page = pages_ref[bb, hh * n_chunk + c]
        return (pltpu.make_async_copy(pool_k.at[page, hh], kbuf.at[to_slot, c], sem.at[to_slot, 0]),
                pltpu.make_async_copy(pool_v.at[page, hh], vbuf.at[to_slot, c], sem.at[to_slot, 1]))

    def start_all(bb, hh, to_slot):
        def body(c, carry):
            for cp in chunk_copies(bb, hh, c, to_slot):
                cp.start()
            return carry
        lax.fori_loop(0, n_chunk, body, 0)

    @pl.when(step == 0)
    def _():
        start_all(b, h, slot)

    nxt = step + 1

    @pl.when(nxt < pl.num_programs(0) * n_heads)
    def _():
        start_all(nxt // n_heads, nxt % n_heads, 1 - slot)

    pltpu.make_async_copy(pool_k.at[pl.ds(0, n_chunk), 0], kbuf.at[slot], sem.at[slot, 0]).wait()
    pltpu.make_async_copy(pool_v.at[pl.ds(0, n_chunk), 0], vbuf.at[slot], sem.at[slot, 1]).wait()
    kch = [kbuf.at[slot, c] for c in range(n_chunk)]
    vch = [vbuf.at[slot, c] for c in range(n_chunk)]
    qt = qt_ref[...] * QK_SCALE
    knt = knt_ref[...]
    vnt = vnt_ref[...]
    lane_n = _iota((1, t_new), 1)
    lane_o = _iota((HEAD_DIM, t_new), 1)
    out = jnp.zeros((HEAD_DIM, t_new), F32)
    for t in range(t_new):
        qcol = qt[:, t:t + 1]
        s_own = jnp.where(lane_n <= t, jnp.sum(qcol * knt, axis=0, keepdims=True), NEG)
        m = jnp.max(s_own, axis=1, keepdims=True)
        scores = []
        for r in range(B_TOPK):
            picked = sel_ref[b, (h * t_new + t) * B_TOPK + r] >= 0
            for pg in range(ppb):
                kc = kch[(t * B_TOPK + r) * ppb + pg][...]
                s = jnp.where(picked, jnp.sum(qcol * kc, axis=0, keepdims=True), NEG)
                m = jnp.maximum(m, jnp.max(s, axis=1, keepdims=True))
                scores.append(s)
        p_own = jnp.exp(s_own - m)
        l = jnp.sum(p_own, axis=1, keepdims=True)
        acc = jnp.sum(p_own * vnt, axis=1, keepdims=True)
        accv = jnp.zeros((HEAD_DIM, LANES), F32)
        for c, s in enumerate(scores):
            p = jnp.exp(s - m)
            l = l + jnp.sum(p, axis=1, keepdims=True)
            accv = accv + p * vch[t * B_TOPK * ppb + c][...]
        acc = acc + jnp.sum(accv, axis=1, keepdims=True)
        out = jnp.where(lane_o == t, acc / l, out)
    o_ref[...] = out


def _moba_gather(qt, knt, vnt, pool_kt, pool_vt, page_table, sel):
    bsz, width, t_new = qt.shape
    _, n_heads, hd, page = pool_kt.shape
    ppb = B_BLOCK // page
    n_chunk = t_new * B_TOPK * ppb
    new_spec = pl.BlockSpec((None, hd, t_new), lambda b, h, pt, sl: (b, h, 0))
    hbm = pl.BlockSpec(memory_space=pl.ANY)
    return pl.pallas_call(
        functools.partial(_moba_gather_body, ppb=ppb),
        grid_spec=pltpu.PrefetchScalarGridSpec(
            num_scalar_prefetch=2,
            grid=(bsz, n_heads),
            in_specs=[new_spec, new_spec, new_spec, hbm, hbm],
            out_specs=pl.BlockSpec((None, hd, t_new), lambda b, h, pt, sl: (b, h, 0)),
            scratch_shapes=[
                pltpu.VMEM((2, n_chunk, hd, page), F32),
                pltpu.VMEM((2, n_chunk, hd, page), F32),
                pltpu.SemaphoreType.DMA((2, 2)),
            ],
        ),
        out_shape=jax.ShapeDtypeStruct((bsz, width, t_new), F32),
        compiler_params=_cparams(("arbitrary", "arbitrary")),
        name="moba_gather",
    )(page_table, sel, qt, knt, vnt, pool_kt, pool_vt)


def _silu(x):
    return x / (1.0 + jnp.exp(-x))


def _ffn_even_body(x_ref, a_ref, b_ref, wo_ref, g_ref, wg_ref, wu_ref, wd_ref, o_ref, hn_ref, acc_ref):
    f = pl.program_id(1)
    aw = a_ref.shape[1]

    @pl.when(f == 0)
    def _():
        x1 = (x_ref[...]
              + jnp.dot(a_ref[...], wo_ref[:aw, :], preferred_element_type=F32)
              + jnp.dot(b_ref[...], wo_ref[aw:, :], preferred_element_type=F32))
        acc_ref[...] = x1
        hn_ref[...] = _rms(x1, g_ref[...]).astype(BF16)

    hn = hn_ref[...]
    gate = jnp.dot(hn, wg_ref[...], preferred_element_type=F32)
    up = jnp.dot(hn, wu_ref[...], preferred_element_type=F32)
    act = (_silu(gate) * up).astype(BF16)
    acc_ref[...] += jnp.dot(act, wd_ref[...], preferred_element_type=F32)

    @pl.when(f == pl.num_programs(1) - 1)
    def _():
        o_ref[...] = acc_ref[...]


def _ffn_even(x, a, b, wo16, g, wg16, wu16, wd16, *, tm=640, tf=1408):
    rows, dm = x.shape
    aw = a.shape[1]
    bw = b.shape[1]
    dff = wg16.shape[1]
    return pl.pallas_call(
        _ffn_even_body,
        grid=(rows // tm, dff // tf),
        in_specs=[
            pl.BlockSpec((tm, dm), lambda i, f: (i, 0)),
            pl.BlockSpec((tm, aw), lambda i, f: (i, 0)),
            pl.BlockSpec((tm, bw), lambda i, f: (i, 0)),
            pl.BlockSpec((aw + bw, dm), lambda i, f: (0, 0)),
            pl.BlockSpec((1, dm), lambda i, f: (0, 0)),
            pl.BlockSpec((dm, tf), lambda i, f: (0, f)),
            pl.BlockSpec((dm, tf), lambda i, f: (0, f)),
            pl.BlockSpec((tf, dm), lambda i, f: (f, 0)),
        ],
        out_specs=pl.BlockSpec((tm, dm), lambda i, f: (i, 0)),
        out_shape=jax.ShapeDtypeStruct((rows, dm), F32),
        scratch_shapes=[pltpu.VMEM((tm, dm), BF16), pltpu.VMEM((tm, dm), F32)],
        compiler_params=_cparams(("arbitrary", "arbitrary")),
        name="ffn_even",
    )(x, a, b, wo16, g, wg16, wu16, wd16)


def _odd_in_body(x_ref, g_ref, w_ref, q_ref, k_ref, v_ref, kt_ref, vt_ref, *, n_prompt_tiles):
    cw = q_ref.shape[1]
    h = _rms(x_ref[...], g_ref[...]).astype(BF16)
    y = jnp.dot(h, w_ref[...], preferred_element_type=F32)
    q_ref[...] = y[:, :cw]
    k_ref[...] = y[:, cw:2 * cw]
    v_ref[...] = y[:, 2 * cw:]
    _feature_major_kv(pl.program_id(0), n_prompt_tiles, y[:, cw:2 * cw], y[:, 2 * cw:], kt_ref, vt_ref)


def _odd_in(x, bsz, t_len, g, w16, *, tm=256):
    rows, dm = x.shape
    cw = w16.shape[1] // 3
    npt = bsz * t_len // tm
    out = jax.ShapeDtypeStruct((rows, cw), F32)
    out_t = jax.ShapeDtypeStruct((bsz, cw, t_len), F32)
    spec = pl.BlockSpec((tm, cw), lambda i: (i, 0))
    fm_spec = _feature_major_spec(cw, tm, npt, t_len // tm)
    return pl.pallas_call(
        functools.partial(_odd_in_body, n_prompt_tiles=npt),
        grid=(rows // tm,),
        in_specs=[
            pl.BlockSpec((tm, dm), lambda i: (i, 0)),
            pl.BlockSpec((1, dm), lambda i: (0, 0)),
            pl.BlockSpec((dm, 3 * cw), lambda i: (0, 0)),
        ],
        out_specs=[spec, spec, spec, fm_spec, fm_spec],
        out_shape=[out, out, out, out_t, out_t],
        compiler_params=_cparams(("arbitrary",)),
        name="odd_in",
    )(x, g, w16)


def _dilated_prompt_body(*refs, patterns, tiles):
    q_refs, k_refs, v_refs = (refs[g * tiles:(g + 1) * tiles] for g in range(3))
    o_ref = refs[3 * tiles]
    state = refs[3 * tiles + 1:]
    t_len = q_refs[0].shape[0]
    blk = 128
    lo = _iota((blk, LANES), 1) < HEAD_DIM
    n_br = len(patterns)
    n_iter = t_len // blk
    masks = []
    for window, dil in patterns:
        assert window // dil == blk
        has_prev = t_len // (dil * blk) > 1
        n_keys = 2 * blk if has_prev else blk
        qrow = _iota((2 * blk, n_keys), 0) % blk
        kcol = _iota((2 * blk, n_keys), 1)
        if has_prev:
            masks.append(((kcol >= blk) & ((kcol - blk) <= qrow), (kcol < blk) & (kcol >= qrow)))
        else:
            masks.append((kcol <= qrow, None))

    def body(it, carry):
        for bi, (window, dil) in enumerate(patterns):
            nblk = t_len // (dil * blk)
            cur_ok, prev_ok = masks[bi]
            cls = it // nblk
            ib = it % nblk

            def rows_at(block, cls=cls, dil=dil):
                start = cls + dil * blk * block
                if dil == 1:
                    return pl.ds(pl.multiple_of(start, blk), blk)
                return pl.ds(start, blk, stride=dil)

            rows = rows_at(ib)
            prows = rows_at(jnp.maximum(ib - 1, 0))
            for tile in range(tiles):
                q_ref, k_ref, v_ref = q_refs[tile], k_refs[tile], v_refs[tile]
                lse_s, out_s = state[2 * (tile * n_br + bi):2 * (tile * n_br + bi) + 2]
                q = q_ref[rows, :] * QK_SCALE
                qst = jnp.concatenate([jnp.where(lo, q, 0.0), jnp.where(lo, 0.0, q)], axis=0).astype(BF16)
                kk = k_ref[rows, :].astype(BF16)
                vv = v_ref[rows, :].astype(BF16)
                ok = cur_ok
                if prev_ok is not None:
                    kk = jnp.concatenate([k_ref[prows, :].astype(BF16), kk], axis=0)
                    vv = jnp.concatenate([v_ref[prows, :].astype(BF16), vv], axis=0)
                    ok = cur_ok | (prev_ok & (ib > 0))
                s = jnp.where(ok, lax.dot_general(qst, kk, NT, preferred_element_type=F32), NEG)
                m = jnp.max(s, axis=1, keepdims=True)
                p = jnp.exp(s - m)
                l = jnp.sum(p, axis=1, keepdims=True)
                pv = jnp.dot(p.astype(BF16), vv, preferred_element_type=F32)
                lse = m + jnp.log(l)
                outn = pv / l
                lse_s[rows, :] = jnp.where(lo, lse[:blk], lse[blk:])
                out_s[rows, :] = jnp.where(lo, outn[:blk], outn[blk:])
        return carry

    lax.fori_loop(0, n_iter, body, 0)

    chunk = 128

    def merge(c, carry):
        rows = pl.ds(pl.multiple_of(c * chunk, chunk), chunk)
        for tile in range(tiles):
            st = state[2 * tile * n_br:2 * (tile + 1) * n_br]
            ms = [st[2 * r][rows, :] for r in range(n_br)]
            m = functools.reduce(jnp.maximum, ms)
            num = jnp.zeros((chunk, LANES), F32)
            den = jnp.zeros((chunk, LANES), F32)
            for r in range(n_br):
                w = jnp.exp(ms[r] - m)
                num = num + w * st[2 * r + 1][rows, :]
                den = den + w
            o_ref[rows, tile * LANES:(tile + 1) * LANES] = (num / den).astype(BF16)
        return carry

    lax.fori_loop(0, t_len // chunk, merge, 0)


def _dilated_prompt(q, k, v, bsz, t_len, *, tiles=4):
    width = q.shape[1]
    specs = [pl.BlockSpec((t_len, LANES), lambda b, j, t=t: (b, j * tiles + t)) for t in range(tiles)]
    return pl.pallas_call(
        functools.partial(_dilated_prompt_body, patterns=C_PATTERNS, tiles=tiles),
        grid=(bsz, width // (tiles * LANES)),
        in_specs=specs * 3,
        out_specs=pl.BlockSpec((t_len, tiles * LANES), lambda b, j: (b, j)),
        out_shape=jax.ShapeDtypeStruct((bsz * t_len, width), BF16),
        scratch_shapes=[pltpu.VMEM((t_len, LANES), F32)] * (2 * len(C_PATTERNS) * tiles),
        compiler_params=_cparams(("arbitrary", "arbitrary")),
        name="dilated_prompt",
    )(*([q] * tiles), *([k] * tiles), *([v] * tiles))


def _dilated_sample_body(q_ref, kn_ref, vn_ref, kc_ref, vc_ref, o_ref, *, patterns):
    t_new, cols = q_ref.shape
    w_len = kc_ref.shape[1]
    nrow = 2 * t_new
    lane = _iota((t_new, LANES), 1)

    def mult(delta):
        cnt = jnp.zeros(delta.shape, F32)
        for window, dil in patterns:
            hit = (delta >= 0) & (delta <= window) & ((delta % dil) == 0)
            cnt = cnt + hit.astype(F32)
        return cnt

    t_c = _iota((nrow, w_len), 0) % t_new
    w_c = mult(w_len + t_c - _iota((nrow, w_len), 1))
    t_n = _iota((nrow, LANES), 0) % t_new
    c_n = _iota((nrow, LANES), 1)
    w_n = jnp.where(c_n < t_new, mult(t_n - c_n), 0.0)
    pad = jnp.zeros((LANES - t_new, LANES), F32)
    for tile in range(cols // LANES):
        cs = slice(tile * LANES, (tile + 1) * LANES)
        q = q_ref[:, cs] * QK_SCALE
        qst = jnp.concatenate([jnp.where(lane < HEAD_DIM, q, 0.0), jnp.where(lane >= HEAD_DIM, q, 0.0)], axis=0)
        qst = qst.astype(BF16)
        kn = jnp.concatenate([kn_ref[:, cs], pad], axis=0).astype(BF16)
        vn = jnp.concatenate([vn_ref[:, cs], pad], axis=0).astype(BF16)
        s_c = jnp.dot(qst, kc_ref[cs, :].astype(BF16), preferred_element_type=F32)
        s_n = lax.dot_general(qst, kn, NT, preferred_element_type=F32)
        s_c = jnp.where(w_c > 0, s_c, NEG)
        s_n = jnp.where(w_n > 0, s_n, NEG)
        m = jnp.maximum(jnp.max(s_c, axis=1, keepdims=True), jnp.max(s_n, axis=1, keepdims=True))
        p_c = w_c * jnp.exp(s_c - m)
        p_n = w_n * jnp.exp(s_n - m)
        l = jnp.sum(p_c, axis=1, keepdims=True) + jnp.sum(p_n, axis=1, keepdims=True)
        acc = (lax.dot_general(p_c.astype(BF16), vc_ref[cs, :].astype(BF16), NT, preferred_element_type=F32)
               + jnp.dot(p_n.astype(BF16), vn, preferred_element_type=F32))
        out = acc / l
        o_ref[:, cs] = jnp.where(lane < HEAD_DIM, out[:t_new], out[t_new:])


def _dilated_sample(q, k, v, row_block0, cache_kt, cache_vt, *, tiles=4):
    bsz, width, w_len = cache_kt.shape
    t_new = 8
    cols = tiles * LANES
    new_spec = pl.BlockSpec((t_new, cols), lambda b, j: (row_block0 + b, j))
    cache_spec = pl.BlockSpec((None, cols, w_len), lambda b, j: (b, j, 0))
    return pl.pallas_call(
        functools.partial(_dilated_sample_body, patterns=C_PATTERNS),
        grid=(bsz, width // cols),
        in_specs=[new_spec, new_spec, new_spec, cache_spec, cache_spec],
        out_specs=pl.BlockSpec((t_new, cols), lambda b, j: (b, j)),
        out_shape=jax.ShapeDtypeStruct((bsz * t_new, width), F32),
        compiler_params=_cparams(("arbitrary", "arbitrary")),
        name="dilated_sample",
    )(q, k, v, cache_kt, cache_vt)


def _moe_route_body(x_ref, c_ref, wo_ref, g_ref, r_ref,
                    x3_ref, hn_ref, comb_ref, pos_ref, post_ref, cnt_ref, before_ref):
    tb = x_ref.shape[0]

    @pl.when(pl.program_id(0) == 0)
    def _():
        before_ref[...] = jnp.where(_iota((tb, tb), 1) < _iota((tb, tb), 0), 1.0, 0.0).astype(BF16)

    x3 = x_ref[...] + jnp.dot(c_ref[...], wo_ref[...], preferred_element_type=F32)
    x3_ref[...] = x3
    hf = _rms(x3, g_ref[...])
    hn = hf.astype(BF16)
    hn_ref[...] = hn
    lane = _iota((tb, LANES), 1)
    h_lo = (hf - hn.astype(F32)).astype(BF16)
    r = r_ref[...]
    r_hi = r.astype(BF16)
    r_lo = (r - r_hi.astype(F32)).astype(BF16)
    logits = (jnp.dot(hn, r_hi, preferred_element_type=F32)
              + (jnp.dot(hn, r_lo, preferred_element_type=F32)
                 + jnp.dot(h_lo, r_hi, preferred_element_type=F32)))
    lg = jnp.where(lane < N_EXPERTS, logits, -jnp.inf)
    m1 = jnp.max(lg, axis=1, keepdims=True)
    p1 = lane == jnp.min(jnp.where(lg == m1, lane, LANES), axis=1, keepdims=True)
    lg2 = jnp.where(p1, -jnp.inf, lg)
    m2 = jnp.max(lg2, axis=1, keepdims=True)
    p2 = lane == jnp.min(jnp.where(lg2 == m2, lane, LANES), axis=1, keepdims=True)
    e2 = jnp.exp(m2 - m1)
    den = 1.0 + e2
    comb_ref[...] = jnp.where(p1, 1.0 / den, 0.0) + jnp.where(p2, e2 / den, 0.0)
    routed = p1 | p2
    ind = jnp.where(routed, 1.0, 0.0)
    slot = jnp.where(routed, jnp.dot(before_ref[...], ind.astype(BF16), preferred_element_type=F32), -1.0)
    pos_ref[...] = slot
    eye = jnp.where(_iota((LANES, LANES), 0) == _iota((LANES, LANES), 1), 1.0, 0.0)
    post_ref[...] = lax.dot_general(eye, slot, NT, precision=HIGHEST, preferred_element_type=F32)
    cnt = jnp.sum(ind, axis=0, keepdims=True)
    cnt_ref[...] = jnp.broadcast_to(cnt, cnt_ref.shape).astype(jnp.int32)


def _moe_route(x, c, wo16, g, router_pad, *, tb):
    rows, dm = x.shape
    nblk = rows // tb
    row_spec = lambda width: pl.BlockSpec((tb, width), lambda i: (i, 0))
    return pl.pallas_call(
        _moe_route_body,
        grid=(nblk,),
        in_specs=[
            row_spec(dm), row_spec(dm),
            pl.BlockSpec((dm, dm), lambda i: (0, 0)),
            pl.BlockSpec((1, dm), lambda i: (0, 0)),
            pl.BlockSpec((dm, LANES), lambda i: (0, 0)),
        ],
        out_specs=[
            row_spec(dm), row_spec(dm), row_spec(LANES), row_spec(LANES),
            pl.BlockSpec((None, LANES, tb), lambda i: (i, 0, 0)),
            pl.BlockSpec((None, 8, LANES), lambda i: (i, 0, 0)),
        ],
        out_shape=[
            jax.ShapeDtypeStruct((rows, dm), F32),
            jax.ShapeDtypeStruct((rows, dm), BF16),
            jax.ShapeDtypeStruct((rows, LANES), F32),
            jax.ShapeDtypeStruct((rows, LANES), F32),
            jax.ShapeDtypeStruct((nblk, LANES, tb), F32),
            jax.ShapeDtypeStruct((nblk, 8, LANES), jnp.int32),
        ],
        scratch_shapes=[pltpu.VMEM((tb, tb), BF16)],
        compiler_params=_cparams(("arbitrary",)),
        name="moe_route",
    )(x, c, wo16, g, router_pad)


def _moe_expert_body(cnt_ref, acc_ref, hn_ref, comb_ref, pos_ref, post_ref, gf_ref,
                     wg_ref, wu_ref, wd_ref, o_ref, *, e, last, tile_rows):
    i = pl.program_id(0)
    tb = hn_ref.shape[0]
    n_tok = cnt_ref[i, e]
    lane = _iota((tb, LANES), 1)
    w_col = jnp.sum(jnp.where(lane == e, comb_ref[...], 0.0), axis=1, keepdims=True)
    slot_col = jnp.sum(jnp.where(lane == e, pos_ref[...], 0.0), axis=1, keepdims=True)
    slot_row = post_ref[pl.ds(e, 1), :]
    o_ref[...] = acc_ref[...]

    def run_tile(rows, base):
        sub_of_row = _iota((rows, tb), 0).astype(F32)
        sub_of_col = _iota((tb, rows), 1).astype(F32)
        gather = jnp.where(slot_row - base == sub_of_row, 1.0, 0.0).astype(BF16)
        xe = jnp.dot(gather, hn_ref[...], preferred_element_type=F32).astype(BF16)
        gate = jnp.dot(xe, wg_ref[...], preferred_element_type=F32)
        up = jnp.dot(xe, wu_ref[...], preferred_element_type=F32)
        act = (_silu(gate) * up).astype(BF16)
        y = jnp.dot(act, wd_ref[...], preferred_element_type=F32).astype(BF16)
        scatter = jnp.where(slot_col - base == sub_of_col, 1.0, 0.0).astype(BF16)
        o_ref[...] += w_col * jnp.dot(scatter, y, preferred_element_type=F32)

    lower = 0
    for rows in tile_rows:
        @pl.when((n_tok > lower) & (n_tok <= rows))
        def _(rows=rows):
            run_tile(rows, 0.0)
        lower = rows
    big = tile_rows[-1]

    @pl.when(n_tok > big)
    def _():
        def body(s, carry):
            run_tile(big, (s * big).astype(F32))
            return carry
        lax.fori_loop(0, (n_tok + big - 1) // big, body, 0)

    if last:
        o_ref[...] = _rms(o_ref[...], gf_ref[...])


def _moe_experts(counts, x3, hn, comb, pos, post, g_final, wg16, wu16, wd16, *, tb, tile_rows):
    rows, dm = x3.shape
    n_exp, _, dff = wg16.shape
    once = pl.Buffered(1)
    acc = x3
    for e in range(n_exp):
        acc = pl.pallas_call(
            functools.partial(_moe_expert_body, e=e, last=(e == n_exp - 1), tile_rows=tile_rows),
            grid_spec=pltpu.PrefetchScalarGridSpec(
                num_scalar_prefetch=1,
                grid=(rows // tb,),
                in_specs=[
                    pl.BlockSpec((tb, dm), lambda i, cnt: (i, 0)),
                    pl.BlockSpec((tb, dm), lambda i, cnt: (i, 0)),
                    pl.BlockSpec((tb, LANES), lambda i, cnt: (i, 0)),
                    pl.BlockSpec((tb, LANES), lambda i, cnt: (i, 0)),
                    pl.BlockSpec((None, LANES, tb), lambda i, cnt: (i, 0, 0)),
                    pl.BlockSpec((1, dm), lambda i, cnt: (0, 0)),
                    pl.BlockSpec((None, dm, dff), lambda i, cnt, e=e: (e, 0, 0), pipeline_mode=once),
                    pl.BlockSpec((None, dm, dff), lambda i, cnt, e=e: (e, 0, 0), pipeline_mode=once),
                    pl.BlockSpec((None, dff, dm), lambda i, cnt, e=e: (e, 0, 0), pipeline_mode=once),
                ],
                out_specs=pl.BlockSpec((tb, dm), lambda i, cnt: (i, 0)),
            ),
            out_shape=jax.ShapeDtypeStruct((rows, dm), F32),
            compiler_params=_cparams(("arbitrary",)),
            name=f"moe_expert{e}",
        )(counts, acc, hn, comb, pos, post, g_final, wg16, wu16, wd16)
    return acc


def _moe(x, c, wo16, g, router_pad, g_final, wg16, wu16, wd16, *, tb=416, tile_rows=(128, 160, 192, 256)):
    x3, hn, comb, pos, post, cnt = _moe_route(x, c, wo16, g, router_pad, tb=tb)
    counts = cnt[:, 0, :N_EXPERTS]
    return _moe_experts(counts, x3, hn, comb, pos, post, g_final, wg16, wu16, wd16, tb=tb,
                        tile_rows=tile_rows)


def kernel(x_prompt, x_sample, cache_b_k, cache_b_v, cache_c_k, cache_c_v, page_table, norm_mix, norm_ffn,
           norm_final, w_in_even, w_out_even, a_ln_g, a_ln_b, a_ws, a_bs, ffn_gate, ffn_up, ffn_down,
           w_in_odd, w_out_odd, router, moe_gate, moe_up, moe_down):
    bsz, t_len, dm = x_prompt.shape
    dbsz, t_new, _ = x_sample.shape
    assert norm_mix.shape[0] == 2 and w_in_even.shape[0] == 1 and w_in_odd.shape[0] == 1
    assert cache_b_k.shape[0] == 1 and cache_c_k.shape[0] == 1
    assert t_new == 8 and A_CHUNK % t_new == 0
    n_p = bsz * t_len
    n_s = dbsz * t_new
    aw = a_ln_g.shape[-1]
    n_groups = aw // HEAD_DIM
    seq_per_chunk = A_CHUNK // t_new

    x = jnp.concatenate([x_prompt.reshape(n_p, dm), x_sample.reshape(n_s, dm)], axis=0)

    ws = a_ws[0]
    ws_sample = jnp.tile(ws[:, :t_new, :t_new], (1, seq_per_chunk, seq_per_chunk))
    ws2 = jnp.stack([ws, ws_sample])
    bias_p = jnp.repeat(a_bs[0].T, HEAD_DIM, axis=1)
    bias_s = jnp.repeat(jnp.tile(a_bs[0][:, :t_new].T, (seq_per_chunk, 1)), HEAD_DIM, axis=1)
    bias2 = jnp.stack([bias_p, bias_s])
    row = lambda vec: vec.reshape(1, -1)
    router_pad = jnp.pad(router[0], ((0, 0), (0, LANES - router.shape[-1])))

    w_even = w_in_even[0].astype(BF16)
    a_out, v_a, q_b, k_b, v_b, kt_b, vt_b, v_a_last = _even_in(
        x, bsz, t_len, row(norm_mix[0]), w_even, row(a_ln_g[0]), row(a_ln_b[0]), ws2, bias2)
    b_prompt = _moba_prompt(q_b, k_b, v_b, bsz, t_len)
    page = cache_b_k.shape[2]
    bw = q_b.shape[1]
    feature_major = lambda c: jnp.transpose(c, (0, 1, 3, 4, 2))
    pool_kt = feature_major(cache_b_k).reshape(-1, B_HEADS, HEAD_DIM, page)
    pool_vt = feature_major(cache_b_v).reshape(-1, B_HEADS, HEAD_DIM, page)
    sel = _moba_select(q_b, n_p // t_new, pool_kt.reshape(-1, bw, page), page_table)
    sel = sel.reshape(dbsz, B_HEADS * t_new, LANES)[:, :, :B_TOPK].reshape(dbsz, -1)
    new_t = lambda a: jnp.transpose(a[n_p:].reshape(dbsz, t_new, -1), (0, 2, 1))
    ppb = B_BLOCK // page
    logical = (jnp.maximum(sel, 0)[:, :, None] * ppb + jnp.arange(ppb, dtype=jnp.int32)).reshape(dbsz, -1)
    hit = logical[:, :, None] == jnp.arange(page_table.shape[1], dtype=jnp.int32)
    sel_pages = jnp.sum(jnp.where(hit, page_table[:, None, :], 0), axis=-1)
    b_sample_t = _moba_gather(new_t(q_b), new_t(k_b), new_t(v_b), pool_kt, pool_vt, sel_pages, sel)
    b_sample = jnp.transpose(b_sample_t, (0, 2, 1)).reshape(n_s, bw)
    b_out = jnp.concatenate([b_prompt, b_sample.astype(BF16)], axis=0)
    x2 = _ffn_even(x, a_out, b_out, w_out_even[0].astype(BF16), row(norm_ffn[0]),
                   ffn_gate[0].astype(BF16), ffn_up[0].astype(BF16), ffn_down[0].astype(BF16))

    w_odd = w_in_odd[0].astype(BF16)
    q_c, k_c, v_c, kt_c, vt_c = _odd_in(x2, bsz, t_len, row(norm_mix[1]), w_odd)
    c_prompt = _dilated_prompt(q_c, k_c, v_c, bsz, t_len)
    cw = q_c.shape[1]
    w_len = cache_c_k.shape[2]
    c_sample = _dilated_sample(q_c, k_c, v_c, n_p // t_new,
                               feature_major(cache_c_k).reshape(-1, cw, w_len),
                               feature_major(cache_c_v).reshape(-1, cw, w_len))
    c_out = jnp.concatenate([c_prompt, c_sample.astype(BF16)], axis=0)
    y = _moe(x2, c_out, w_out_odd[0].astype(BF16), row(norm_ffn[1]), router_pad, row(norm_final),
             moe_gate[0].astype(BF16), moe_up[0].astype(BF16), moe_down[0].astype(BF16))

    bh = (B_HEADS, HEAD_DIM)
    ch = (cw // HEAD_DIM, HEAD_DIM)
    assert t_len % 256 == 0
    c_keep_p = min(C_PATTERNS[-1][0], t_len)
    leaf = lambda a_t, hd: jnp.transpose(a_t.reshape(1, bsz, *hd, a_t.shape[-1]), (0, 1, 4, 2, 3))
    return (
        y[:n_p].reshape(bsz, t_len, dm),
        y[n_p:].reshape(dbsz, t_new, dm),
        v_a_last.reshape(1, bsz, A_CHUNK, aw),
        v_a[n_p:].reshape(1, dbsz, t_new, aw),
        leaf(kt_b, bh),
        leaf(vt_b, bh),
        k_b[n_p:].reshape(1, dbsz, t_new, *bh),
        v_b[n_p:].reshape(1, dbsz, t_new, *bh),
        leaf(kt_c[:, :, t_len - c_keep_p:], ch),
        leaf(vt_c[:, :, t_len - c_keep_p:], ch),
        k_c[n_p:].reshape(1, dbsz, t_new, *ch),
        v_c[n_p:].reshape(1, dbsz, t_new, *ch),
    )
```

```python
import functools

import jax
import jax.numpy as jnp
from jax import lax
from jax.experimental import pallas as pl
from jax.experimental.pallas import tpu as pltpu

F32 = jnp.float32
BF16 = jnp.bfloat16
HIGHEST = lax.Precision.HIGHEST

HEAD_DIM = 64
LANES = 128
A_CHUNK = 128
B_HEADS = 8
B_BLOCK = 256
B_TOPK = 3
C_PATTERNS = ((128, 1), (512, 4), (2048, 16))
N_EXPERTS = 8
RMS_EPS = 1e-6
LN_EPS = 1e-5
NEG = -1e30
QK_SCALE = HEAD_DIM ** -0.5
VMEM_LIMIT = 56 * 1024 * 1024

NT = (((1,), (1,)), ((), ()))


def _cparams(sem):
    return pltpu.CompilerParams(dimension_semantics=sem, vmem_limit_bytes=VMEM_LIMIT)


def _rms(x, g):
    return x * lax.rsqrt(jnp.mean(x * x, axis=-1, keepdims=True) + RMS_EPS) * g


def _gelu(x):
    return 0.5 * x * (1.0 + lax.erf(x * 0.7071067811865476))


def _iota(shape, dim):
    return lax.broadcasted_iota(jnp.int32, shape, dim)


def _feature_major_kv(i, n_prompt_tiles, k_tile, v_tile, kt_ref, vt_ref):
    @pl.when(i < n_prompt_tiles)
    def _():
        kt_ref[...] = k_tile.T
        vt_ref[...] = v_tile.T


def _feature_major_spec(width, tm, n_prompt_tiles, tiles_per_seq):
    def imap(i):
        ic = jnp.minimum(i, n_prompt_tiles - 1)
        return (ic // tiles_per_seq, 0, ic % tiles_per_seq)
    return pl.BlockSpec((None, width, tm), imap)


def _even_in_body(x_ref, g_ref, w_ref, lng_ref, lnb_ref, ws_ref, bias_ref,
                  a_ref, v_ref, q_ref, k_ref, vv_ref, kt_ref, vt_ref, vlast_ref,
                  *, n_prompt_tiles, tiles_per_seq, aw, bw):
    i = pl.program_id(0)
    tm = x_ref.shape[0]
    h = _rms(x_ref[...], g_ref[...]).astype(BF16)
    y = jnp.dot(h, w_ref[...], preferred_element_type=F32)
    _feature_major_kv(i, n_prompt_tiles, y[:, 2 * aw + bw:2 * aw + 2 * bw], y[:, 2 * aw + 2 * bw:], kt_ref, vt_ref)
    u = _gelu(y[:, :aw])
    v = _gelu(y[:, aw:2 * aw])
    mu = jnp.mean(v, axis=-1, keepdims=True)
    vc = v - mu
    var = jnp.mean(vc * vc, axis=-1, keepdims=True)
    v = vc * lax.rsqrt(var + LN_EPS) * lng_ref[...] + lnb_ref[...]
    v_ref[...] = v

    @pl.when((i < n_prompt_tiles) & (i % tiles_per_seq == tiles_per_seq - 1))
    def _():
        vlast_ref[...] = v[tm - A_CHUNK:, :]
    q_ref[...] = y[:, 2 * aw:2 * aw + bw]
    k_ref[...] = y[:, 2 * aw + bw:2 * aw + 2 * bw]
    vv_ref[...] = y[:, 2 * aw + 2 * bw:]

    row = _iota((A_CHUNK, A_CHUNK), 0)
    col = _iota((A_CHUNK, A_CHUNK), 1)
    is_prompt = i < n_prompt_tiles
    ok = (col <= row) & (((row // 8) == (col // 8)) | is_prompt)
    lo = _iota((A_CHUNK, LANES), 1) < HEAD_DIM
    v16 = v.astype(BF16)
    n_groups = aw // HEAD_DIM
    wm = [jnp.where(ok, ws_ref[g], 0.0).astype(BF16) for g in range(n_groups)]
    for c in range(tm // A_CHUNK):
        rs = slice(c * A_CHUNK, (c + 1) * A_CHUNK)
        for j in range(aw // LANES):
            cs = slice(j * LANES, (j + 1) * LANES)
            vj = v16[rs, cs]
            ma = jnp.dot(wm[2 * j], vj, preferred_element_type=F32)
            mb = jnp.dot(wm[2 * j + 1], vj, preferred_element_type=F32)
            mixed = jnp.where(lo, ma, mb) + bias_ref[:, cs]
            a_ref[rs, cs] = (u[rs, cs] * mixed).astype(BF16)


def _even_in(x, bsz, t_len, g, w16, lng, lnb, ws2, bias2, *, tm=256):
    rows, dm = x.shape
    aw = lng.shape[-1]
    n_in = w16.shape[1]
    bw = (n_in - 2 * aw) // 3
    n_groups = aw // HEAD_DIM
    npt = bsz * t_len // tm
    sel = lambda i: jnp.minimum(i // npt, 1)
    row_spec = lambda width: pl.BlockSpec((tm, width), lambda i: (i, 0))
    tps = t_len // tm
    fm_spec = _feature_major_spec(bw, tm, npt, tps)
    return pl.pallas_call(
        functools.partial(_even_in_body, n_prompt_tiles=npt, tiles_per_seq=tps, aw=aw, bw=bw),
        grid=(rows // tm,),
        in_specs=[
            row_spec(dm),
            pl.BlockSpec((1, dm), lambda i: (0, 0)),
            pl.BlockSpec((dm, n_in), lambda i: (0, 0)),
            pl.BlockSpec((1, aw), lambda i: (0, 0)),
            pl.BlockSpec((1, aw), lambda i: (0, 0)),
            pl.BlockSpec((None, n_groups, A_CHUNK, A_CHUNK), lambda i: (sel(i), 0, 0, 0)),
            pl.BlockSpec((None, A_CHUNK, aw), lambda i: (sel(i), 0, 0)),
        ],
        out_specs=[row_spec(aw), row_spec(aw), row_spec(bw), row_spec(bw), row_spec(bw), fm_spec, fm_spec,
                   pl.BlockSpec((None, A_CHUNK, aw), lambda i: (jnp.minimum(i, npt - 1) // tps, 0, 0))],
        out_shape=[
            jax.ShapeDtypeStruct((rows, aw), BF16),
            jax.ShapeDtypeStruct((rows, aw), F32),
            jax.ShapeDtypeStruct((rows, bw), F32),
            jax.ShapeDtypeStruct((rows, bw), F32),
            jax.ShapeDtypeStruct((rows, bw), F32),
            jax.ShapeDtypeStruct((bsz, bw, t_len), F32),
            jax.ShapeDtypeStruct((bsz, bw, t_len), F32),
            jax.ShapeDtypeStruct((bsz, A_CHUNK, aw), F32),
        ],
        compiler_params=_cparams(("arbitrary",)),
        name="even_in",
    )(x, g, w16, lng, lnb, ws2, bias2)


def _moba_prompt_body(q_ref, k_ref, v_ref, o_ref, k16, v16, mt_ref, s_scr, p_scr, *, nb):
    i = pl.program_id(1)
    t_len, width = k_ref.shape
    n_gate = B_HEADS * nb

    @pl.when(i == 0)
    def _():
        k = k_ref[...]
        k16[...] = k.astype(BF16)
        v16[...] = v_ref[...].astype(BF16)
        r = _iota((LANES, t_len), 0)
        key = _iota((LANES, t_len), 1)
        avg = jnp.where(((key // B_BLOCK) == (r % nb)) & (r < n_gate), 1.0 / B_BLOCK, 0.0).astype(F32)
        mt = jnp.dot(avg, k, precision=HIGHEST, preferred_element_type=F32)
        rr = _iota((LANES, width), 0)
        cc = _iota((LANES, width), 1)
        mt_ref[...] = jnp.where(((rr // nb) == (cc // HEAD_DIM)) & (rr < n_gate), mt, 0.0)

    q = q_ref[...]
    tq = q.shape[0]
    gate = lax.dot_general(q, mt_ref[...], NT, precision=HIGHEST, preferred_element_type=F32)
    lane = _iota((tq, LANES), 1)
    n_of = lane % nb
    valid = (n_of < i) & (lane < n_gate)
    gm = jnp.where(valid, gate, -jnp.inf)
    rank = jnp.zeros((tq, LANES), jnp.int32)
    for s in range(1, nb):
        fwd = pltpu.roll(gm, LANES - s, 1)
        bwd = pltpu.roll(gm, nb - s, 1)
        wrap = (n_of + s) >= nb
        other = jnp.where(wrap, bwd, fwd)
        m_idx = jnp.where(wrap, n_of + s - nb, n_of + s)
        beats = (other > gm) | ((other == gm) & (m_idx < n_of))
        rank = rank + beats.astype(jnp.int32)
    selbias = jnp.where(valid & (rank < B_TOPK), 0.0, NEG)

    row2 = _iota((2 * tq, B_BLOCK), 0) % tq
    col2 = _iota((2 * tq, B_BLOCK), 1)
    causal_bias = jnp.where(col2 <= row2, 0.0, NEG)
    lo = lane < HEAD_DIM
    own = pl.multiple_of(i * B_BLOCK, B_BLOCK)
    half = B_BLOCK // 2

    key_lane = _iota((B_BLOCK, LANES), 1)

    group = s_scr.shape[0]

    def zero_body(n, carry):
        for g in range(group):
            p_scr[g, n] = jnp.zeros((2 * tq, B_BLOCK), BF16)
        return carry

    lax.fori_loop(i + 1, nb, zero_body, 0)
    for j0 in range(0, width // LANES, group):
        cols = [slice((j0 + g) * LANES, (j0 + g + 1) * LANES) for g in range(group)]
        q_aug, m_tiles = [], []
        for g in range(group):
            j = j0 + g
            qj = q[:, cols[g]] * QK_SCALE
            qst = jnp.concatenate([jnp.where(lo, qj, 0.0), jnp.where(lo, 0.0, qj)], axis=0).astype(BF16)
            s = (lax.dot_general(qst, k16[pl.ds(own, B_BLOCK), cols[g]], NT, preferred_element_type=F32)
                 + causal_bias)
            s_scr[g, i] = s
            m_tiles.append(jnp.maximum(s[:, :half], s[:, half:]))
            sb = []
            for a in range(2):
                off = (2 * j + a) * nb
                moved = selbias if off == 0 else pltpu.roll(selbias, LANES - off, 1)
                sb.append(jnp.where(lane < nb, moved, 0.0))
            q_aug.append(jnp.concatenate([qst, jnp.concatenate(sb, axis=0).astype(BF16)], axis=1))

        def score_body(n, m_tiles, q_aug=q_aug, cols=cols):
            start = pl.multiple_of(n * B_BLOCK, B_BLOCK)
            onehot = jnp.where(key_lane == n, 1.0, 0.0).astype(BF16)
            out = []
            for g in range(group):
                k_aug = jnp.concatenate([k16[pl.ds(start, B_BLOCK), cols[g]], onehot], axis=1)
                s = lax.dot_general(q_aug[g], k_aug, NT, preferred_element_type=F32)
                s_scr[g, n] = s
                out.append(jnp.maximum(m_tiles[g], jnp.maximum(s[:, :half], s[:, half:])))
            return tuple(out)

        m_tiles = lax.fori_loop(0, i, score_body, tuple(m_tiles))
        ms = [jnp.max(m_tiles[g], axis=1, keepdims=True) for g in range(group)]

        def prob_body(n, l_tiles, ms=ms):
            out = []
            for g in range(group):
                p = jnp.exp(s_scr[g, n] - ms[g])
                p_scr[g, n] = p.astype(BF16)
                out.append(l_tiles[g] + p[:, :half] + p[:, half:])
            return tuple(out)

        l_tiles = lax.fori_loop(0, i + 1, prob_body,
                                tuple(jnp.zeros((2 * tq, half), F32) for _ in range(group)))
        for g in range(group):
            l = jnp.sum(l_tiles[g], axis=1, keepdims=True)
            acc = jnp.dot(p_scr[g, 0], v16[0:B_BLOCK, cols[g]], preferred_element_type=F32)
            for n in range(1, nb):
                acc = acc + jnp.dot(p_scr[g, n], v16[n * B_BLOCK:(n + 1) * B_BLOCK, cols[g]],
                                    preferred_element_type=F32)
            o = acc / l
            o_ref[:, cols[g]] = jnp.where(lo, o[:tq], o[tq:]).astype(BF16)


def _moba_prompt(q, k, v, bsz, t_len):
    width = q.shape[1]
    nb = t_len // B_BLOCK
    return pl.pallas_call(
        functools.partial(_moba_prompt_body, nb=nb),
        grid=(bsz, nb),
        in_specs=[
            pl.BlockSpec((B_BLOCK, width), lambda b, i: (b * nb + i, 0)),
            pl.BlockSpec((t_len, width), lambda b, i: (b, 0)),
            pl.BlockSpec((t_len, width), lambda b, i: (b, 0)),
        ],
        out_specs=pl.BlockSpec((B_BLOCK, width), lambda b, i: (b * nb + i, 0)),
        out_shape=jax.ShapeDtypeStruct((bsz * t_len, width), BF16),
        scratch_shapes=[
            pltpu.VMEM((t_len, width), BF16),
            pltpu.VMEM((t_len, width), BF16),
            pltpu.VMEM((LANES, width), F32),
            pltpu.VMEM((2, nb, 2 * B_BLOCK, B_BLOCK), F32),
            pltpu.VMEM((2, nb, 2 * B_BLOCK, B_BLOCK), BF16),
        ],
        compiler_params=_cparams(("arbitrary", "arbitrary")),
        name="moba_prompt",
    )(q, k, v)


def _moba_select_body(pt_ref, q_ref, *refs, npg, ppb, n_blocks):
    pages = refs[:npg]
    idx_ref = refs[npg]
    mt_ref = refs[npg + 1]
    s = pl.program_id(1)
    t_new, width = q_ref.shape
    bps = npg // ppb

    @pl.when(s == 0)
    def _():
        mt_ref[...] = jnp.zeros(mt_ref.shape, F32)

    lane_w = _iota((width, LANES), 1)
    cur = mt_ref[...]
    for blk in range(bps):
        tot = pages[blk * ppb][...]
        for r in range(1, ppb):
            tot = tot + pages[blk * ppb + r][...]
        col = jnp.sum(tot, axis=1, keepdims=True) * (1.0 / B_BLOCK)
        cur = jnp.where(lane_w == s * bps + blk, col, cur)
    mt_ref[...] = cur

    @pl.when(s == pl.num_programs(1) - 1)
    def _():
        ht = B_HEADS * t_new
        rr = _iota((ht, width), 0)
        cc = _iota((ht, width), 1)
        q = q_ref[...]
        qst = jnp.where((rr // t_new) == (cc // HEAD_DIM), jnp.concatenate([q] * B_HEADS, axis=0), 0.0)
        gate = jnp.dot(qst, cur, precision=HIGHEST, preferred_element_type=F32)
        lane = _iota((ht, LANES), 1)
        g = jnp.where(lane < n_blocks, gate, -jnp.inf)
        out = jnp.full((ht, LANES), -1, jnp.int32)
        for r in range(B_TOPK):
            mx = jnp.max(g, axis=1, keepdims=True)
            idx = jnp.min(jnp.where(g == mx, lane, LANES), axis=1, keepdims=True)
            out = jnp.where(lane == r, jnp.where(mx > -jnp.inf, idx, -1), out)
            g = jnp.where(lane == idx, -jnp.inf, g)
        idx_ref[...] = out


def _moba_select(q, row_block0, pool_kt, page_table, *, npg=32):
    bsz, n_pages = page_table.shape
    _, width, page = pool_kt.shape
    ppb = B_BLOCK // page
    t_new = 8
    ht = B_HEADS * t_new
    page_specs = [
        pl.BlockSpec((None, width, page), lambda b, s, pt, r=r: (pt[b, s * npg + r], 0, 0))
        for r in range(npg)
    ]
    return pl.pallas_call(
        functools.partial(_moba_select_body, npg=npg, ppb=ppb, n_blocks=n_pages // ppb),
        grid_spec=pltpu.PrefetchScalarGridSpec(
            num_scalar_prefetch=1,
            grid=(bsz, n_pages // npg),
            in_specs=[pl.BlockSpec((t_new, width), lambda b, s, pt: (row_block0 + b, 0))] + page_specs,
            out_specs=pl.BlockSpec((ht, LANES), lambda b, s, pt: (b, 0)),
            scratch_shapes=[pltpu.VMEM((width, LANES), F32)],
        ),
        out_shape=jax.ShapeDtypeStruct((bsz * ht, LANES), jnp.int32),
        compiler_params=_cparams(("arbitrary", "arbitrary")),
        name="moba_select",
    )(page_table, q, *([pool_kt] * npg))


def _moba_gather_body(pages_ref, sel_ref, qt_ref, knt_ref, vnt_ref, pool_k, pool_v, o_ref,
                      kbuf, vbuf, sem, *, ppb):
    b = pl.program_id(0)
    h = pl.program_id(1)
    n_heads = pl.num_programs(1)
    t_new = qt_ref.shape[1]
    n_chunk = t_new * B_TOPK * ppb
    step = b * n_heads + h
    slot = step % 2

    def chunk_copies(bb, hh, c, to_slot):
        page = pages_ref[bb, hh * n_chunk + c]
        return (pltpu.make_async_copy(pool_k.at[page, hh], kbuf.at[to_slot, c], sem.at[to_slot, 0]),
                pltpu.make_async_copy(pool_v.at[page, hh], vbuf.at[to_slot, c], sem.at[to_slot, 1]))

    def start_all(bb, hh, to_slot):
        def body(c, carry):
            for cp in chunk_copies(bb, hh, c, to_slot):
                cp.start()
            return carry
        lax.fori_loop(0, n_chunk, body, 0)

    @pl.when(step == 0)
    def _():
        start_all(b, h, slot)

    nxt = step + 1

    @pl.when(nxt < pl.num_programs(0) * n_heads)
    def _():
        start_all(nxt // n_heads, nxt % n_heads, 1 - slot)

    pltpu.make_async_copy(pool_k.at[pl.ds(0, n_chunk), 0], kbuf.at[slot], sem.at[slot, 0]).wait()
    pltpu.make_async_copy(pool_v.at[pl.ds(0, n_chunk), 0], vbuf.at[slot], sem.at[slot, 1]).wait()
    page_rows = kbuf.shape[3]
    kch = [kbuf.at[slot, c] for c in range(n_chunk)]
    vch = [vbuf.at[slot, c] for c in range(n_chunk)]
    qt = qt_ref[...] * QK_SCALE
    knt = knt_ref[...]
    vnt = vnt_ref[...]
    per_t = B_TOPK * ppb
    rows, own_rows = [], []
    for t in range(t_new):
        qcol = qt[:, t:t + 1]
        own_rows.append(jnp.sum(qcol * knt, axis=0, keepdims=True))
        parts = []
        for r in range(B_TOPK):
            picked = sel_ref[b, (h * t_new + t) * B_TOPK + r] >= 0
            for pg in range(ppb):
                kc = kch[(t * B_TOPK + r) * ppb + pg][...]
                parts.append(jnp.where(picked, jnp.sum(qcol * kc, axis=0, keepdims=True), NEG))
        rows.append(jnp.concatenate(parts, axis=1))
    s_all = jnp.concatenate(rows, axis=0)
    s_own = jnp.concatenate(own_rows, axis=0)
    s_own = jnp.where(_iota((t_new, t_new), 1) <= _iota((t_new, t_new), 0), s_own, NEG)
    m = jnp.maximum(jnp.max(s_all, axis=1, keepdims=True), jnp.max(s_own, axis=1, keepdims=True))
    p_all = jnp.exp(s_all - m)
    p_own = jnp.exp(s_own - m)
    l = jnp.sum(p_all, axis=1, keepdims=True) + jnp.sum(p_own, axis=1, keepdims=True)
    lane_o = _iota((HEAD_DIM, t_new), 1)
    out = jnp.zeros((HEAD_DIM, t_new), F32)
    for t in range(t_new):
        accv = jnp.zeros((HEAD_DIM, page_rows), F32)
        for c in range(per_t):
            accv = accv + p_all[t:t + 1, c * page_rows:(c + 1) * page_rows] * vch[t * per_t + c][...]
        acc = (jnp.sum(accv, axis=1, keepdims=True)
               + jnp.sum(p_own[t:t + 1, :] * vnt, axis=1, keepdims=True))
        out = jnp.where(lane_o == t, acc / l[t:t + 1, :], out)
    o_ref[...] = out


def _moba_gather(qt, knt, vnt, pool_kt, pool_vt, page_table, sel):
    bsz, width, t_new = qt.shape
    _, n_heads, hd, page = pool_kt.shape
    ppb = B_BLOCK // page
    n_chunk = t_new * B_TOPK * ppb
    new_spec = pl.BlockSpec((None, hd, t_new), lambda b, h, pt, sl: (b, h, 0))
    hbm = pl.BlockSpec(memory_space=pl.ANY)
    return pl.pallas_call(
        functools.partial(_moba_gather_body, ppb=ppb),
        grid_spec=pltpu.PrefetchScalarGridSpec(
            num_scalar_prefetch=2,
            grid=(bsz, n_heads),
            in_specs=[new_spec, new_spec, new_spec, hbm, hbm],
            out_specs=pl.BlockSpec((None, hd, t_new), lambda b, h, pt, sl: (b, h, 0)),
            scratch_shapes=[
                pltpu.VMEM((2, n_chunk, hd, page), F32),
                pltpu.VMEM((2, n_chunk, hd, page), F32),
                pltpu.SemaphoreType.DMA((2, 2)),
            ],
        ),
        out_shape=jax.ShapeDtypeStruct((bsz, width, t_new), F32),
        compiler_params=_cparams(("arbitrary", "arbitrary")),
        name="moba_gather",
    )(page_table, sel, qt, knt, vnt, pool_kt, pool_vt)


def _silu(x):
    return x / (1.0 + jnp.exp(-x))


def _ffn_even_body(x_ref, a_ref, b_ref, wo_ref, g_ref, wg_ref, wu_ref, wd_ref, o_ref, hn_ref, acc_ref):
    f = pl.program_id(1)
    aw = a_ref.shape[1]

    @pl.when(f == 0)
    def _():
        x1 = (x_ref[...]
              + jnp.dot(a_ref[...], wo_ref[:aw, :], preferred_element_type=F32)
              + jnp.dot(b_ref[...], wo_ref[aw:, :], preferred_element_type=F32))
        acc_ref[...] = x1
        hn_ref[...] = _rms(x1, g_ref[...]).astype(BF16)

    hn = hn_ref[...]
    gate = jnp.dot(hn, wg_ref[...], preferred_element_type=F32)
    up = jnp.dot(hn, wu_ref[...], preferred_element_type=F32)
    act = (_silu(gate) * up).astype(BF16)
    acc_ref[...] += jnp.dot(act, wd_ref[...], preferred_element_type=F32)

    @pl.when(f == pl.num_programs(1) - 1)
    def _():
        o_ref[...] = acc_ref[...]


def _ffn_even(x, a, b, wo16, g, wg16, wu16, wd16, *, tm=640, tf=1408):
    rows, dm = x.shape
    aw = a.shape[1]
    bw = b.shape[1]
    dff = wg16.shape[1]
    return pl.pallas_call(
        _ffn_even_body,
        grid=(rows // tm, dff // tf),
        in_specs=[
            pl.BlockSpec((tm, dm), lambda i, f: (i, 0)),
            pl.BlockSpec((tm, aw), lambda i, f: (i, 0)),
            pl.BlockSpec((tm, bw), lambda i, f: (i, 0)),
            pl.BlockSpec((aw + bw, dm), lambda i, f: (0, 0)),
            pl.BlockSpec((1, dm), lambda i, f: (0, 0)),
            pl.BlockSpec((dm, tf), lambda i, f: (0, f)),
            pl.BlockSpec((dm, tf), lambda i, f: (0, f)),
            pl.BlockSpec((tf, dm), lambda i, f: (f, 0)),
        ],
        out_specs=pl.BlockSpec((tm, dm), lambda i, f: (i, 0)),
        out_shape=jax.ShapeDtypeStruct((rows, dm), F32),
        scratch_shapes=[pltpu.VMEM((tm, dm), BF16), pltpu.VMEM((tm, dm), F32)],
        compiler_params=_cparams(("arbitrary", "arbitrary")),
        name="ffn_even",
    )(x, a, b, wo16, g, wg16, wu16, wd16)


def _odd_in_body(x_ref, g_ref, w_ref, q_ref, k_ref, v_ref, kt_ref, vt_ref, *, n_prompt_tiles):
    cw = q_ref.shape[1]
    h = _rms(x_ref[...], g_ref[...]).astype(BF16)
    y = jnp.dot(h, w_ref[...], preferred_element_type=F32)
    q_ref[...] = y[:, :cw]
    k_ref[...] = y[:, cw:2 * cw]
    v_ref[...] = y[:, 2 * cw:]
    _feature_major_kv(pl.program_id(0), n_prompt_tiles, y[:, cw:2 * cw], y[:, 2 * cw:], kt_ref, vt_ref)


def _odd_in(x, bsz, t_len, g, w16, *, tm=256):
    rows, dm = x.shape
    cw = w16.shape[1] // 3
    npt = bsz * t_len // tm
    out = jax.ShapeDtypeStruct((rows, cw), F32)
    out_t = jax.ShapeDtypeStruct((bsz, cw, t_len), F32)
    spec = pl.BlockSpec((tm, cw), lambda i: (i, 0))
    fm_spec = _feature_major_spec(cw, tm, npt, t_len // tm)
    return pl.pallas_call(
        functools.partial(_odd_in_body, n_prompt_tiles=npt),
        grid=(rows // tm,),
        in_specs=[
            pl.BlockSpec((tm, dm), lambda i: (i, 0)),
            pl.BlockSpec((1, dm), lambda i: (0, 0)),
            pl.BlockSpec((dm, 3 * cw), lambda i: (0, 0)),
        ],
        out_specs=[spec, spec, spec, fm_spec, fm_spec],
        out_shape=[out, out, out, out_t, out_t],
        compiler_params=_cparams(("arbitrary",)),
        name="odd_in",
    )(x, g, w16)


def _dilated_prompt_body(*refs, patterns, tiles):
    q_refs, k_refs, v_refs = (refs[g * tiles:(g + 1) * tiles] for g in range(3))
    o_ref = refs[3 * tiles]
    state = refs[3 * tiles + 1:]
    t_len = q_refs[0].shape[0]
    blk = 128
    lo = _iota((blk, LANES), 1) < HEAD_DIM
    n_br = len(patterns)
    n_iter = t_len // blk
    masks = []
    for window, dil in patterns:
        assert window // dil == blk
        has_prev = t_len // (dil * blk) > 1
        n_keys = 2 * blk if has_prev else blk
        qrow = _iota((2 * blk, n_keys), 0) % blk
        kcol = _iota((2 * blk, n_keys), 1)
        if has_prev:
            masks.append(((kcol >= blk) & ((kcol - blk) <= qrow), (kcol < blk) & (kcol >= qrow)))
        else:
            masks.append((kcol <= qrow, None))

    def body(it, carry):
        for bi, (window, dil) in enumerate(patterns):
            nblk = t_len // (dil * blk)
            cur_ok, prev_ok = masks[bi]
            cls = it // nblk
            ib = it % nblk

            def rows_at(block, cls=cls, dil=dil):
                start = cls + dil * blk * block
                if dil == 1:
                    return pl.ds(pl.multiple_of(start, blk), blk)
                return pl.ds(start, blk, stride=dil)

            rows = rows_at(ib)
            prows = rows_at(jnp.maximum(ib - 1, 0))
            for tile in range(tiles):
                q_ref, k_ref, v_ref = q_refs[tile], k_refs[tile], v_refs[tile]
                lse_s, out_s = state[2 * (tile * n_br + bi):2 * (tile * n_br + bi) + 2]
                q = q_ref[rows, :] * QK_SCALE
                qst = jnp.concatenate([jnp.where(lo, q, 0.0), jnp.where(lo, 0.0, q)], axis=0).astype(BF16)
                kk = k_ref[rows, :].astype(BF16)
                vv = v_ref[rows, :].astype(BF16)
                ok = cur_ok
                if prev_ok is not None:
                    kk = jnp.concatenate([k_ref[prows, :].astype(BF16), kk], axis=0)
                    vv = jnp.concatenate([v_ref[prows, :].astype(BF16), vv], axis=0)
                    ok = cur_ok | (prev_ok & (ib > 0))
                s = jnp.where(ok, lax.dot_general(qst, kk, NT, preferred_element_type=F32), NEG)
                m = jnp.max(s, axis=1, keepdims=True)
                p = jnp.exp(s - m)
                l = jnp.sum(p, axis=1, keepdims=True)
                pv = jnp.dot(p.astype(BF16), vv, preferred_element_type=F32)
                lse = m + jnp.log(l)
                outn = pv / l
                lse_s[rows, :] = jnp.where(lo, lse[:blk], lse[blk:])
                out_s[rows, :] = jnp.where(lo, outn[:blk], outn[blk:])
        return carry

    lax.fori_loop(0, n_iter, body, 0)

    chunk = 128

    def merge(c, carry):
        rows = pl.ds(pl.multiple_of(c * chunk, chunk), chunk)
        for tile in range(tiles):
            st = state[2 * tile * n_br:2 * (tile + 1) * n_br]
            ms = [st[2 * r][rows, :] for r in range(n_br)]
            m = functools.reduce(jnp.maximum, ms)
            num = jnp.zeros((chunk, LANES), F32)
            den = jnp.zeros((chunk, LANES), F32)
            for r in range(n_br):
                w = jnp.exp(ms[r] - m)
                num = num + w * st[2 * r + 1][rows, :]
                den = den + w
            o_ref[rows, tile * LANES:(tile + 1) * LANES] = (num / den).astype(BF16)
        return carry

    lax.fori_loop(0, t_len // chunk, merge, 0)


def _dilated_prompt(q, k, v, bsz, t_len, *, tiles=4):
    width = q.shape[1]
    specs = [pl.BlockSpec((t_len, LANES), lambda b, j, t=t: (b, j * tiles + t)) for t in range(tiles)]
    return pl.pallas_call(
        functools.partial(_dilated_prompt_body, patterns=C_PATTERNS, tiles=tiles),
        grid=(bsz, width // (tiles * LANES)),
        in_specs=specs * 3,
        out_specs=pl.BlockSpec((t_len, tiles * LANES), lambda b, j: (b, j)),
        out_shape=jax.ShapeDtypeStruct((bsz * t_len, width), BF16),
        scratch_shapes=[pltpu.VMEM((t_len, LANES), F32)] * (2 * len(C_PATTERNS) * tiles),
        compiler_params=_cparams(("arbitrary", "arbitrary")),
        name="dilated_prompt",
    )(*([q] * tiles), *([k] * tiles), *([v] * tiles))


def _dilated_sample_body(q_ref, kn_ref, vn_ref, kc_ref, vc_ref, o_ref, *, patterns):
    t_new, cols = q_ref.shape
    w_len = kc_ref.shape[1]
    nrow = 2 * t_new
    lane = _iota((t_new, LANES), 1)

    def mult(delta):
        cnt = jnp.zeros(delta.shape, F32)
        for window, dil in patterns:
            hit = (delta >= 0) & (delta <= window) & ((delta % dil) == 0)
            cnt = cnt + hit.astype(F32)
        return cnt

    t_c = _iota((nrow, w_len), 0) % t_new
    w_c = mult(w_len + t_c - _iota((nrow, w_len), 1))
    t_n = _iota((nrow, LANES), 0) % t_new
    c_n = _iota((nrow, LANES), 1)
    w_n = jnp.where(c_n < t_new, mult(t_n - c_n), 0.0)
    pad = jnp.zeros((LANES - t_new, LANES), F32)
    for tile in range(cols // LANES):
        cs = slice(tile * LANES, (tile + 1) * LANES)
        q = q_ref[:, cs] * QK_SCALE
        qst = jnp.concatenate([jnp.where(lane < HEAD_DIM, q, 0.0), jnp.where(lane >= HEAD_DIM, q, 0.0)], axis=0)
        qst = qst.astype(BF16)
        kn = jnp.concatenate([kn_ref[:, cs], pad], axis=0).astype(BF16)
        vn = jnp.concatenate([vn_ref[:, cs], pad], axis=0).astype(BF16)
        s_c = jnp.dot(qst, kc_ref[cs, :].astype(BF16), preferred_element_type=F32)
        s_n = lax.dot_general(qst, kn, NT, preferred_element_type=F32)
        s_c = jnp.where(w_c > 0, s_c, NEG)
        s_n = jnp.where(w_n > 0, s_n, NEG)
        m = jnp.maximum(jnp.max(s_c, axis=1, keepdims=True), jnp.max(s_n, axis=1, keepdims=True))
        p_c = w_c * jnp.exp(s_c - m)
        p_n = w_n * jnp.exp(s_n - m)
        l = jnp.sum(p_c, axis=1, keepdims=True) + jnp.sum(p_n, axis=1, keepdims=True)
        acc = (lax.dot_general(p_c.astype(BF16), vc_ref[cs, :].astype(BF16), NT, preferred_element_type=F32)
               + jnp.dot(p_n.astype(BF16), vn, preferred_element_type=F32))
        out = acc / l
        o_ref[:, cs] = jnp.where(lane < HEAD_DIM, out[:t_new], out[t_new:])


def _dilated_sample(q, k, v, row_block0, cache_kt, cache_vt, *, tiles=4):
    bsz, width, w_len = cache_kt.shape
    t_new = 8
    cols = tiles * LANES
    new_spec = pl.BlockSpec((t_new, cols), lambda b, j: (row_block0 + b, j))
    cache_spec = pl.BlockSpec((None, cols, w_len), lambda b, j: (b, j, 0))
    return pl.pallas_call(
        functools.partial(_dilated_sample_body, patterns=C_PATTERNS),
        grid=(bsz, width // cols),
        in_specs=[new_spec, new_spec, new_spec, cache_spec, cache_spec],
        out_specs=pl.BlockSpec((t_new, cols), lambda b, j: (b, j)),
        out_shape=jax.ShapeDtypeStruct((bsz * t_new, width), F32),
        compiler_params=_cparams(("arbitrary", "arbitrary")),
        name="dilated_sample",
    )(q, k, v, cache_kt, cache_vt)


def _moe_route_body(x_ref, c_ref, wo_ref, g_ref, r_ref,
                    x3_ref, hn_ref, comb_ref, pos_ref, post_ref, cnt_ref, before_ref):
    tb = x_ref.shape[0]

    @pl.when(pl.program_id(0) == 0)
    def _():
        before_ref[...] = jnp.where(_iota((tb, tb), 1) < _iota((tb, tb), 0), 1.0, 0.0).astype(BF16)

    x3 = x_ref[...] + jnp.dot(c_ref[...], wo_ref[...], preferred_element_type=F32)
    x3_ref[...] = x3
    hf = _rms(x3, g_ref[...])
    hn = hf.astype(BF16)
    hn_ref[...] = hn
    lane = _iota((tb, LANES), 1)
    h_lo = (hf - hn.astype(F32)).astype(BF16)
    r = r_ref[...]
    r_hi = r.astype(BF16)
    r_lo = (r - r_hi.astype(F32)).astype(BF16)
    logits = (jnp.dot(hn, r_hi, preferred_element_type=F32)
              + (jnp.dot(hn, r_lo, preferred_element_type=F32)
                 + jnp.dot(h_lo, r_hi, preferred_element_type=F32)))
    lg = jnp.where(lane < N_EXPERTS, logits, -jnp.inf)
    m1 = jnp.max(lg, axis=1, keepdims=True)
    p1 = lane == jnp.min(jnp.where(lg == m1, lane, LANES), axis=1, keepdims=True)
    lg2 = jnp.where(p1, -jnp.inf, lg)
    m2 = jnp.max(lg2, axis=1, keepdims=True)
    p2 = lane == jnp.min(jnp.where(lg2 == m2, lane, LANES), axis=1, keepdims=True)
    e2 = jnp.exp(m2 - m1)
    den = 1.0 + e2
    comb_ref[...] = jnp.where(p1, 1.0 / den, 0.0) + jnp.where(p2, e2 / den, 0.0)
    routed = p1 | p2
    ind = jnp.where(routed, 1.0, 0.0)
    slot = jnp.where(routed, jnp.dot(before_ref[...], ind.astype(BF16), preferred_element_type=F32), -1.0)
    pos_ref[...] = slot
    eye = jnp.where(_iota((LANES, LANES), 0) == _iota((LANES, LANES), 1), 1.0, 0.0)
    post_ref[...] = lax.dot_general(eye, slot, NT, precision=HIGHEST, preferred_element_type=F32)
    cnt = jnp.sum(ind, axis=0, keepdims=True)
    cnt_ref[...] = jnp.broadcast_to(cnt, cnt_ref.shape).astype(jnp.int32)


def _moe_route(x, c, wo16, g, router_pad, *, tb):
    rows, dm = x.shape
    nblk = rows // tb
    row_spec = lambda width: pl.BlockSpec((tb, width), lambda i: (i, 0))
    return pl.pallas_call(
        _moe_route_body,
        grid=(nblk,),
        in_specs=[
            row_spec(dm), row_spec(dm),
            pl.BlockSpec((dm, dm), lambda i: (0, 0)),
            pl.BlockSpec((1, dm), lambda i: (0, 0)),
            pl.BlockSpec((dm, LANES), lambda i: (0, 0)),
        ],
        out_specs=[
            row_spec(dm), row_spec(dm), row_spec(LANES), row_spec(LANES),
            pl.BlockSpec((None, LANES, tb), lambda i: (i, 0, 0)),
            pl.BlockSpec((None, 8, LANES), lambda i: (i, 0, 0)),
        ],
        out_shape=[
            jax.ShapeDtypeStruct((rows, dm), F32),
            jax.ShapeDtypeStruct((rows, dm), BF16),
            jax.ShapeDtypeStruct((rows, LANES), F32),
            jax.ShapeDtypeStruct((rows, LANES), F32),
            jax.ShapeDtypeStruct((nblk, LANES, tb), F32),
            jax.ShapeDtypeStruct((nblk, 8, LANES), jnp.int32),
        ],
        scratch_shapes=[pltpu.VMEM((tb, tb), BF16)],
        compiler_params=_cparams(("arbitrary",)),
        name="moe_route",
    )(x, c, wo16, g, router_pad)


def _moe_expert_body(cnt_ref, acc_ref, hn_ref, comb_ref, pos_ref, post_ref, gf_ref,
                     wg_ref, wu_ref, wd_ref, o_ref, *, e, last, tile_rows):
    i = pl.program_id(0)
    tb = hn_ref.shape[0]
    n_tok = cnt_ref[i, e]
    lane = _iota((tb, LANES), 1)
    w_col = jnp.sum(jnp.where(lane == e, comb_ref[...], 0.0), axis=1, keepdims=True)
    slot_col = jnp.sum(jnp.where(lane == e, pos_ref[...], 0.0), axis=1, keepdims=True)
    slot_row = post_ref[pl.ds(e, 1), :]
    finish = (lambda v: _rms(v, gf_ref[...])) if last else (lambda v: v)

    def tile_out(rows, base):
        sub_of_row = _iota((rows, tb), 0).astype(F32)
        sub_of_col = _iota((tb, rows), 1).astype(F32)
        gather = jnp.where(slot_row - base == sub_of_row, 1.0, 0.0).astype(BF16)
        xe = jnp.dot(gather, hn_ref[...], preferred_element_type=F32).astype(BF16)
        gate = jnp.dot(xe, wg_ref[...], preferred_element_type=F32)
        up = jnp.dot(xe, wu_ref[...], preferred_element_type=F32)
        act = (_silu(gate) * up).astype(BF16)
        y = jnp.dot(act, wd_ref[...], preferred_element_type=F32).astype(BF16)
        scatter = jnp.where(slot_col - base == sub_of_col, 1.0, 0.0).astype(BF16)
        return w_col * jnp.dot(scatter, y, preferred_element_type=F32)

    @pl.when(n_tok == 0)
    def _():
        o_ref[...] = finish(acc_ref[...])

    lower = 0
    for rows in tile_rows:
        @pl.when((n_tok > lower) & (n_tok <= rows))
        def _(rows=rows):
            o_ref[...] = finish(acc_ref[...] + tile_out(rows, 0.0))
        lower = rows
    big = tile_rows[-1]

    @pl.when(n_tok > big)
    def _():
        o_ref[...] = acc_ref[...]

        def body(s, carry):
            o_ref[...] += tile_out(big, (s * big).astype(F32))
            return carry
        lax.fori_loop(0, (n_tok + big - 1) // big, body, 0)
        o_ref[...] = finish(o_ref[...])


def _moe_experts(counts, x3, hn, comb, pos, post, g_final, wg16, wu16, wd16, *, tb, tile_rows):
    rows, dm = x3.shape
    n_exp, _, dff = wg16.shape
    once = pl.Buffered(1)
    acc = x3
    for e in range(n_exp):
        acc = pl.pallas_call(
            functools.partial(_moe_expert_body, e=e, last=(e == n_exp - 1), tile_rows=tile_rows),
            grid_spec=pltpu.PrefetchScalarGridSpec(
                num_scalar_prefetch=1,
                grid=(rows // tb,),
                in_specs=[
                    pl.BlockSpec((tb, dm), lambda i, cnt: (i, 0)),
                    pl.BlockSpec((tb, dm), lambda i, cnt: (i, 0)),
                    pl.BlockSpec((tb, LANES), lambda i, cnt: (i, 0)),
                    pl.BlockSpec((tb, LANES), lambda i, cnt: (i, 0)),
                    pl.BlockSpec((None, LANES, tb), lambda i, cnt: (i, 0, 0)),
                    pl.BlockSpec((1, dm), lambda i, cnt: (0, 0)),
                    pl.BlockSpec((None, dm, dff), lambda i, cnt, e=e: (e, 0, 0), pipeline_mode=once),
                    pl.BlockSpec((None, dm, dff), lambda i, cnt, e=e: (e, 0, 0), pipeline_mode=once),
                    pl.BlockSpec((None, dff, dm), lambda i, cnt, e=e: (e, 0, 0), pipeline_mode=once),
                ],
                out_specs=pl.BlockSpec((tb, dm), lambda i, cnt: (i, 0)),
            ),
            out_shape=jax.ShapeDtypeStruct((rows, dm), F32),
            compiler_params=_cparams(("arbitrary",)),
            name=f"moe_expert{e}",
        )(counts, acc, hn, comb, pos, post, g_final, wg16, wu16, wd16)
    return acc


def _moe(x, c, wo16, g, router_pad, g_final, wg16, wu16, wd16, *, tb=416, tile_rows=(128, 160, 192, 256)):
    x3, hn, comb, pos, post, cnt = _moe_route(x, c, wo16, g, router_pad, tb=tb)
    counts = cnt[:, 0, :N_EXPERTS]
    return _moe_experts(counts, x3, hn, comb, pos, post, g_final, wg16, wu16, wd16, tb=tb,
                        tile_rows=tile_rows)


def kernel(x_prompt, x_sample, cache_b_k, cache_b_v, cache_c_k, cache_c_v, page_table, norm_mix, norm_ffn,
           norm_final, w_in_even, w_out_even, a_ln_g, a_ln_b, a_ws, a_bs, ffn_gate, ffn_up, ffn_down,
           w_in_odd, w_out_odd, router, moe_gate, moe_up, moe_down):
    bsz, t_len, dm = x_prompt.shape
    dbsz, t_new, _ = x_sample.shape
    assert norm_mix.shape[0] == 2 and w_in_even.shape[0] == 1 and w_in_odd.shape[0] == 1
    assert cache_b_k.shape[0] == 1 and cache_c_k.shape[0] == 1
    assert t_new == 8 and A_CHUNK % t_new == 0
    n_p = bsz * t_len
    n_s = dbsz * t_new
    aw = a_ln_g.shape[-1]
    n_groups = aw // HEAD_DIM
    seq_per_chunk = A_CHUNK // t_new

    x = jnp.concatenate([x_prompt.reshape(n_p, dm), x_sample.reshape(n_s, dm)], axis=0)

    ws = a_ws[0]
    ws_sample = jnp.tile(ws[:, :t_new, :t_new], (1, seq_per_chunk, seq_per_chunk))
    ws2 = jnp.stack([ws, ws_sample])
    bias_p = jnp.repeat(a_bs[0].T, HEAD_DIM, axis=1)
    bias_s = jnp.repeat(jnp.tile(a_bs[0][:, :t_new].T, (seq_per_chunk, 1)), HEAD_DIM, axis=1)
    bias2 = jnp.stack([bias_p, bias_s])
    row = lambda vec: vec.reshape(1, -1)
    router_pad = jnp.pad(router[0], ((0, 0), (0, LANES - router.shape[-1])))

    w_even = w_in_even[0].astype(BF16)
    a_out, v_a, q_b, k_b, v_b, kt_b, vt_b, v_a_last = _even_in(
        x, bsz, t_len, row(norm_mix[0]), w_even, row(a_ln_g[0]), row(a_ln_b[0]), ws2, bias2)
    b_prompt = _moba_prompt(q_b, k_b, v_b, bsz, t_len)
    page = cache_b_k.shape[2]
    bw = q_b.shape[1]
    feature_major = lambda c: jnp.transpose(c, (0, 1, 3, 4, 2))
    pool_kt = feature_major(cache_b_k).reshape(-1, B_HEADS, HEAD_DIM, page)
    pool_vt = feature_major(cache_b_v).reshape(-1, B_HEADS, HEAD_DIM, page)
    sel = _moba_select(q_b, n_p // t_new, pool_kt.reshape(-1, bw, page), page_table)
    sel = sel.reshape(dbsz, B_HEADS * t_new, LANES)[:, :, :B_TOPK].reshape(dbsz, -1)
    new_t = lambda a: jnp.transpose(a[n_p:].reshape(dbsz, t_new, -1), (0, 2, 1))
    ppb = B_BLOCK // page
    logical = (jnp.maximum(sel, 0)[:, :, None] * ppb + jnp.arange(ppb, dtype=jnp.int32)).reshape(dbsz, -1)
    hit = logical[:, :, None] == jnp.arange(page_table.shape[1], dtype=jnp.int32)
    sel_pages = jnp.sum(jnp.where(hit, page_table[:, None, :], 0), axis=-1)
    b_sample_t = _moba_gather(new_t(q_b), new_t(k_b), new_t(v_b), pool_kt, pool_vt, sel_pages, sel)
    b_sample = jnp.transpose(b_sample_t, (0, 2, 1)).reshape(n_s, bw)
    b_out = jnp.concatenate([b_prompt, b_sample.astype(BF16)], axis=0)
    x2 = _ffn_even(x, a_out, b_out, w_out_even[0].astype(BF16), row(norm_ffn[0]),
                   ffn_gate[0].astype(BF16), ffn_up[0].astype(BF16), ffn_down[0].astype(BF16))

    w_odd = w_in_odd[0].astype(BF16)
    q_c, k_c, v_c, kt_c, vt_c = _odd_in(x2, bsz, t_len, row(norm_mix[1]), w_odd)
    c_prompt = _dilated_prompt(q_c, k_c, v_c, bsz, t_len)
    cw = q_c.shape[1]
    w_len = cache_c_k.shape[2]
    c_sample = _dilated_sample(q_c, k_c, v_c, n_p // t_new,
                               feature_major(cache_c_k).reshape(-1, cw, w_len),
                               feature_major(cache_c_v).reshape(-1, cw, w_len))
    c_out = jnp.concatenate([c_prompt, c_sample.astype(BF16)], axis=0)
    y = _moe(x2, c_out, w_out_odd[0].astype(BF16), row(norm_ffn[1]), router_pad, row(norm_final),
             moe_gate[0].astype(BF16), moe_up[0].astype(BF16), moe_down[0].astype(BF16))

    bh = (B_HEADS, HEAD_DIM)
    ch = (cw // HEAD_DIM, HEAD_DIM)
    assert t_len % 256 == 0
    c_keep_p = min(C_PATTERNS[-1][0], t_len)
    leaf = lambda a_t, hd: jnp.transpose(a_t.reshape(1, bsz, *hd, a_t.shape[-1]), (0, 1, 4, 2, 3))
    return (
        y[:n_p].reshape(bsz, t_len, dm),
        y[n_p:].reshape(dbsz, t_new, dm),
        v_a_last.reshape(1, bsz, A_CHUNK, aw),
        v_a[n_p:].reshape(1, dbsz, t_new, aw),
        leaf(kt_b, bh),
        leaf(vt_b, bh),
        k_b[n_p:].reshape(1, dbsz, t_new, *bh),
        v_b[n_p:].reshape(1, dbsz, t_new, *bh),
        leaf(kt_c[:, :, t_len - c_keep_p:], ch),
        leaf(vt_c[:, :, t_len - c_keep_p:], ch),
        k_c[n_p:].reshape(1, dbsz, t_new, *ch),
        v_c[n_p:].reshape(1, dbsz, t_new, *ch),
    )
```

```python
import functools

import jax
import jax.numpy as jnp
from jax import lax
from jax.experimental import pallas as pl
from jax.experimental.pallas import tpu as pltpu

F32 = jnp.float32
BF16 = jnp.bfloat16
HIGHEST = lax.Precision.HIGHEST

HEAD_DIM = 64
LANES = 128
A_CHUNK = 128
B_HEADS = 8
B_BLOCK = 256
B_TOPK = 3
C_PATTERNS = ((128, 1), (512, 4), (2048, 16))
N_EXPERTS = 8
RMS_EPS = 1e-6
LN_EPS = 1e-5
NEG = -1e30
QK_SCALE = HEAD_DIM ** -0.5
VMEM_LIMIT = 56 * 1024 * 1024

NT = (((1,), (1,)), ((), ()))


def _cparams(sem):
    return pltpu.CompilerParams(dimension_semantics=sem, vmem_limit_bytes=VMEM_LIMIT)


def _rms(x, g):
    return x * lax.rsqrt(jnp.mean(x * x, axis=-1, keepdims=True) + RMS_EPS) * g


def _gelu(x):
    return 0.5 * x * (1.0 + lax.erf(x * 0.7071067811865476))


def _iota(shape, dim):
    return lax.broadcasted_iota(jnp.int32, shape, dim)


def _feature_major_kv(i, n_prompt_tiles, k_tile, v_tile, kt_ref, vt_ref):
    @pl.when(i < n_prompt_tiles)
    def _():
        kt_ref[...] = k_tile.T
        vt_ref[...] = v_tile.T


def _feature_major_spec(width, tm, n_prompt_tiles, tiles_per_seq):
    def imap(i):
        ic = jnp.minimum(i, n_prompt_tiles - 1)
        return (ic // tiles_per_seq, 0, ic % tiles_per_seq)
    return pl.BlockSpec((None, width, tm), imap)


def _even_in_body(xp_ref, xs_ref, g_ref, w_ref, lng_ref, lnb_ref, ws_ref, bias_ref,
                  a_ref, v_ref, q_ref, k_ref, vv_ref, kt_ref, vt_ref, vlast_ref, xcat_ref,
                  *, n_prompt_tiles, tiles_per_seq, aw, bw):
    i = pl.program_id(0)
    tm = xp_ref.shape[0]
    x = jnp.where(i < n_prompt_tiles, xp_ref[...], xs_ref[...])
    xcat_ref[...] = x
    h = _rms(x, g_ref[...]).astype(BF16)
    y = jnp.dot(h, w_ref[...], preferred_element_type=F32)
    _feature_major_kv(i, n_prompt_tiles, y[:, 2 * aw + bw:2 * aw + 2 * bw], y[:, 2 * aw + 2 * bw:], kt_ref, vt_ref)
    u = _gelu(y[:, :aw])
    v = _gelu(y[:, aw:2 * aw])
    mu = jnp.mean(v, axis=-1, keepdims=True)
    vc = v - mu
    var = jnp.mean(vc * vc, axis=-1, keepdims=True)
    v = vc * lax.rsqrt(var + LN_EPS) * lng_ref[...] + lnb_ref[...]
    v_ref[...] = v

    @pl.when((i < n_prompt_tiles) & (i % tiles_per_seq == tiles_per_seq - 1))
    def _():
        vlast_ref[...] = v[tm - A_CHUNK:, :]
    q_ref[...] = y[:, 2 * aw:2 * aw + bw]
    k_ref[...] = y[:, 2 * aw + bw:2 * aw + 2 * bw]
    vv_ref[...] = y[:, 2 * aw + 2 * bw:]

    row = _iota((A_CHUNK, A_CHUNK), 0)
    col = _iota((A_CHUNK, A_CHUNK), 1)
    is_prompt = i < n_prompt_tiles
    ok = (col <= row) & (((row // 8) == (col // 8)) | is_prompt)
    lo = _iota((A_CHUNK, LANES), 1) < HEAD_DIM
    v16 = v.astype(BF16)
    n_groups = aw // HEAD_DIM
    wm = [jnp.where(ok, ws_ref[g], 0.0).astype(BF16) for g in range(n_groups)]
    for c in range(tm // A_CHUNK):
        rs = slice(c * A_CHUNK, (c + 1) * A_CHUNK)
        for j in range(aw // LANES):
            cs = slice(j * LANES, (j + 1) * LANES)
            vj = v16[rs, cs]
            ma = jnp.dot(wm[2 * j], vj, preferred_element_type=F32)
            mb = jnp.dot(wm[2 * j + 1], vj, preferred_element_type=F32)
            mixed = jnp.where(lo, ma, mb) + bias_ref[:, cs]
            a_ref[rs, cs] = (u[rs, cs] * mixed).astype(BF16)


def _even_in(xp, xs, bsz, t_len, g, w16, lng, lnb, ws2, bias2, *, tm=256):
    dm = xp.shape[1]
    assert xs.shape[0] == tm
    rows = xp.shape[0] + tm
    aw = lng.shape[-1]
    n_in = w16.shape[1]
    bw = (n_in - 2 * aw) // 3
    n_groups = aw // HEAD_DIM
    npt = bsz * t_len // tm
    sel = lambda i: jnp.minimum(i // npt, 1)
    row_spec = lambda width: pl.BlockSpec((tm, width), lambda i: (i, 0))
    tps = t_len // tm
    fm_spec = _feature_major_spec(bw, tm, npt, tps)
    return pl.pallas_call(
        functools.partial(_even_in_body, n_prompt_tiles=npt, tiles_per_seq=tps, aw=aw, bw=bw),
        grid=(rows // tm,),
        in_specs=[
            pl.BlockSpec((tm, dm), lambda i: (jnp.minimum(i, npt - 1), 0)),
            pl.BlockSpec((tm, dm), lambda i: (0, 0)),
            pl.BlockSpec((1, dm), lambda i: (0, 0)),
            pl.BlockSpec((dm, n_in), lambda i: (0, 0)),
            pl.BlockSpec((1, aw), lambda i: (0, 0)),
            pl.BlockSpec((1, aw), lambda i: (0, 0)),
            pl.BlockSpec((None, n_groups, A_CHUNK, A_CHUNK), lambda i: (sel(i), 0, 0, 0)),
            pl.BlockSpec((None, A_CHUNK, aw), lambda i: (sel(i), 0, 0)),
        ],
        out_specs=[row_spec(aw), row_spec(aw), row_spec(bw), row_spec(bw), row_spec(bw), fm_spec, fm_spec,
                   pl.BlockSpec((None, A_CHUNK, aw), lambda i: (jnp.minimum(i, npt - 1) // tps, 0, 0)),
                   row_spec(dm)],
        out_shape=[
            jax.ShapeDtypeStruct((rows, aw), BF16),
            jax.ShapeDtypeStruct((rows, aw), F32),
            jax.ShapeDtypeStruct((rows, bw), F32),
            jax.ShapeDtypeStruct((rows, bw), F32),
            jax.ShapeDtypeStruct((rows, bw), F32),
            jax.ShapeDtypeStruct((bsz, bw, t_len), F32),
            jax.ShapeDtypeStruct((bsz, bw, t_len), F32),
            jax.ShapeDtypeStruct((bsz, A_CHUNK, aw), F32),
            jax.ShapeDtypeStruct((rows, dm), F32),
        ],
        compiler_params=_cparams(("arbitrary",)),
        name="even_in",
    )(xp, xs, g, w16, lng, lnb, ws2, bias2)


def _moba_prompt_body(q_ref, k_ref, v_ref, o_ref, k16, v16, mt_ref, s_scr, p_scr, *, nb):
    i = pl.program_id(1)
    t_len, width = k_ref.shape
    n_gate = B_HEADS * nb

    @pl.when(i == 0)
    def _():
        k = k_ref[...]
        k16[...] = k.astype(BF16)
        v16[...] = v_ref[...].astype(BF16)
        r = _iota((LANES, t_len), 0)
        key = _iota((LANES, t_len), 1)
        avg = jnp.where(((key // B_BLOCK) == (r % nb)) & (r < n_gate), 1.0 / B_BLOCK, 0.0).astype(F32)
        mt = jnp.dot(avg, k, precision=HIGHEST, preferred_element_type=F32)
        rr = _iota((LANES, width), 0)
        cc = _iota((LANES, width), 1)
        mt_ref[...] = jnp.where(((rr // nb) == (cc // HEAD_DIM)) & (rr < n_gate), mt, 0.0)

    q = q_ref[...]
    tq = q.shape[0]
    gate = lax.dot_general(q, mt_ref[...], NT, precision=HIGHEST, preferred_element_type=F32)
    lane = _iota((tq, LANES), 1)
    n_of = lane % nb
    valid = (n_of < i) & (lane < n_gate)
    gm = jnp.where(valid, gate, -jnp.inf)
    rank = jnp.zeros((tq, LANES), jnp.int32)
    for s in range(1, nb):
        fwd = pltpu.roll(gm, LANES - s, 1)
        bwd = pltpu.roll(gm, nb - s, 1)
        wrap = (n_of + s) >= nb
        other = jnp.where(wrap, bwd, fwd)
        m_idx = jnp.where(wrap, n_of + s - nb, n_of + s)
        beats = (other > gm) | ((other == gm) & (m_idx < n_of))
        rank = rank + beats.astype(jnp.int32)
    selbias = jnp.where(valid & (rank < B_TOPK), 0.0, NEG)

    row2 = _iota((2 * tq, B_BLOCK), 0) % tq
    col2 = _iota((2 * tq, B_BLOCK), 1)
    causal_bias = jnp.where(col2 <= row2, 0.0, NEG)
    lo = lane < HEAD_DIM
    own = pl.multiple_of(i * B_BLOCK, B_BLOCK)
    half = B_BLOCK // 2

    key_lane = _iota((B_BLOCK, LANES), 1)

    group = s_scr.shape[0]

    def zero_body(n, carry):
        for g in range(group):
            p_scr[g, n] = jnp.zeros((2 * tq, B_BLOCK), BF16)
        return carry

    lax.fori_loop(i + 1, nb, zero_body, 0)
    for j0 in range(0, width // LANES, group):
        cols = [slice((j0 + g) * LANES, (j0 + g + 1) * LANES) for g in range(group)]
        q_aug, m_tiles = [], []
        for g in range(group):
            j = j0 + g
            qj = q[:, cols[g]] * QK_SCALE
            qst = jnp.concatenate([jnp.where(lo, qj, 0.0), jnp.where(lo, 0.0, qj)], axis=0).astype(BF16)
            s = (lax.dot_general(qst, k16[pl.ds(own, B_BLOCK), cols[g]], NT, preferred_element_type=F32)
                 + causal_bias)
            s_scr[g, i] = s
            m_tiles.append(jnp.maximum(s[:, :half], s[:, half:]))
            sb = []
            for a in range(2):
                off = (2 * j + a) * nb
                moved = selbias if off == 0 else pltpu.roll(selbias, LANES - off, 1)
                sb.append(jnp.where(lane < nb, moved, 0.0))
            q_aug.append(jnp.concatenate([qst, jnp.concatenate(sb, axis=0).astype(BF16)], axis=1))

        def score_body(n, m_tiles, q_aug=q_aug, cols=cols):
            start = pl.multiple_of(n * B_BLOCK, B_BLOCK)
            onehot = jnp.where(key_lane == n, 1.0, 0.0).astype(BF16)
            out = []
            for g in range(group):
                k_aug = jnp.concatenate([k16[pl.ds(start, B_BLOCK), cols[g]], onehot], axis=1)
                s = lax.dot_general(q_aug[g], k_aug, NT, preferred_element_type=F32)
                s_scr[g, n] = s
                out.append(jnp.maximum(m_tiles[g], jnp.maximum(s[:, :half], s[:, half:])))
            return tuple(out)

        m_tiles = lax.fori_loop(0, i, score_body, tuple(m_tiles))
        ms = [jnp.max(m_tiles[g], axis=1, keepdims=True) for g in range(group)]

        def prob_body(n, l_tiles, ms=ms):
            out = []
            for g in range(group):
                p = jnp.exp(s_scr[g, n] - ms[g])
                p_scr[g, n] = p.astype(BF16)
                out.append(l_tiles[g] + p[:, :half] + p[:, half:])
            return tuple(out)

        l_tiles = lax.fori_loop(0, i + 1, prob_body,
                                tuple(jnp.zeros((2 * tq, half), F32) for _ in range(group)))
        for g in range(group):
            l = jnp.sum(l_tiles[g], axis=1, keepdims=True)
            acc = jnp.dot(p_scr[g, 0], v16[0:B_BLOCK, cols[g]], preferred_element_type=F32)
            for n in range(1, nb):
                acc = acc + jnp.dot(p_scr[g, n], v16[n * B_BLOCK:(n + 1) * B_BLOCK, cols[g]],
                                    preferred_element_type=F32)
            o = acc / l
            o_ref[:, cols[g]] = jnp.where(lo, o[:tq], o[tq:]).astype(BF16)


def _moba_prompt(q, k, v, bsz, t_len):
    width = q.shape[1]
    nb = t_len // B_BLOCK
    return pl.pallas_call(
        functools.partial(_moba_prompt_body, nb=nb),
        grid=(bsz, nb),
        in_specs=[
            pl.BlockSpec((B_BLOCK, width), lambda b, i: (b * nb + i, 0)),
            pl.BlockSpec((t_len, width), lambda b, i: (b, 0)),
            pl.BlockSpec((t_len, width), lambda b, i: (b, 0)),
        ],
        out_specs=pl.BlockSpec((B_BLOCK, width), lambda b, i: (b * nb + i, 0)),
        out_shape=jax.ShapeDtypeStruct((bsz * t_len, width), BF16),
        scratch_shapes=[
            pltpu.VMEM((t_len, width), BF16),
            pltpu.VMEM((t_len, width), BF16),
            pltpu.VMEM((LANES, width), F32),
            pltpu.VMEM((2, nb, 2 * B_BLOCK, B_BLOCK), F32),
            pltpu.VMEM((2, nb, 2 * B_BLOCK, B_BLOCK), BF16),
        ],
        compiler_params=_cparams(("arbitrary", "arbitrary")),
        name="moba_prompt",
    )(q, k, v)


def _moba_select_body(pt_ref, q_ref, *refs, npg, ppb, n_blocks):
    pages = refs[:npg]
    idx_ref = refs[npg]
    mt_ref = refs[npg + 1]
    s = pl.program_id(1)
    t_new, width = q_ref.shape
    bps = npg // ppb

    @pl.when(s == 0)
    def _():
        mt_ref[...] = jnp.zeros(mt_ref.shape, F32)

    lane_w = _iota((width, LANES), 1)
    cur = mt_ref[...]
    for blk in range(bps):
        tot = pages[blk * ppb][...]
        for r in range(1, ppb):
            tot = tot + pages[blk * ppb + r][...]
        col = jnp.sum(tot, axis=1, keepdims=True) * (1.0 / B_BLOCK)
        cur = jnp.where(lane_w == s * bps + blk, col, cur)
    mt_ref[...] = cur

    @pl.when(s == pl.num_programs(1) - 1)
    def _():
        ht = B_HEADS * t_new
        rr = _iota((ht, width), 0)
        cc = _iota((ht, width), 1)
        q = q_ref[...]
        qst = jnp.where((rr // t_new) == (cc // HEAD_DIM), jnp.concatenate([q] * B_HEADS, axis=0), 0.0)
        gate = jnp.dot(qst, cur, precision=HIGHEST, preferred_element_type=F32)
        lane = _iota((ht, LANES), 1)
        g = jnp.where(lane < n_blocks, gate, -jnp.inf)
        out = jnp.full((ht, LANES), -1, jnp.int32)
        for r in range(B_TOPK):
            mx = jnp.max(g, axis=1, keepdims=True)
            idx = jnp.min(jnp.where(g == mx, lane, LANES), axis=1, keepdims=True)
            out = jnp.where(lane == r, jnp.where(mx > -jnp.inf, idx, -1), out)
            g = jnp.where(lane == idx, -jnp.inf, g)
        idx_ref[...] = out


def _moba_select(q, row_block0, pool_kt, page_table, *, npg=32):
    bsz, n_pages = page_table.shape
    _, width, page = pool_kt.shape
    ppb = B_BLOCK // page
    t_new = 8
    ht = B_HEADS * t_new
    page_specs = [
        pl.BlockSpec((None, width, page), lambda b, s, pt, r=r: (pt[b, s * npg + r], 0, 0))
        for r in range(npg)
    ]
    return pl.pallas_call(
        functools.partial(_moba_select_body, npg=npg, ppb=ppb, n_blocks=n_pages // ppb),
        grid_spec=pltpu.PrefetchScalarGridSpec(
            num_scalar_prefetch=1,
            grid=(bsz, n_pages // npg),
            in_specs=[pl.BlockSpec((t_new, width), lambda b, s, pt: (row_block0 + b, 0))] + page_specs,
            out_specs=pl.BlockSpec((ht, LANES), lambda b, s, pt: (b, 0)),
            scratch_shapes=[pltpu.VMEM((width, LANES), F32)],
        ),
        out_shape=jax.ShapeDtypeStruct((bsz * ht, LANES), jnp.int32),
        compiler_params=_cparams(("arbitrary", "arbitrary")),
        name="moba_select",
    )(page_table, q, *([pool_kt] * npg))


def _moba_gather_body(pages_ref, sel_ref, qt_ref, knt_ref, vnt_ref, pool_k, pool_v, o_ref,
                      kbuf, vbuf, sem, *, ppb):
    b = pl.program_id(0)
    h = pl.program_id(1)
    n_heads = pl.num_programs(1)
    t_new = qt_ref.shape[1]
    n_chunk = t_new * B_TOPK * ppb
    step = b * n_heads + h
    slot = step % 2

    def chunk_copies(bb, hh, c, to_slot):
        page = pages_ref[bb, hh * n_chunk + c]
        return (pltpu.make_async_copy(pool_k.at[page, hh], kbuf.at[to_slot, c], sem.at[to_slot, 0]),
                pltpu.make_async_copy(pool_v.at[page, hh], vbuf.at[to_slot, c], sem.at[to_slot, 1]))

    def start_all(bb, hh, to_slot):
        def body(c, carry):
            for cp in chunk_copies(bb, hh, c, to_slot):
                cp.start()
            return carry
        lax.fori_loop(0, n_chunk, body, 0)

    @pl.when(step == 0)
    def _():
        start_all(b, h, slot)

    nxt = step + 1

    @pl.when(nxt < pl.num_programs(0) * n_heads)
    def _():
        start_all(nxt // n_heads, nxt % n_heads, 1 - slot)

    pltpu.make_async_copy(pool_k.at[pl.ds(0, n_chunk), 0], kbuf.at[slot], sem.at[slot, 0]).wait()
    pltpu.make_async_copy(pool_v.at[pl.ds(0, n_chunk), 0], vbuf.at[slot], sem.at[slot, 1]).wait()
    page_rows = kbuf.shape[3]
    kch = [kbuf.at[slot, c] for c in range(n_chunk)]
    vch = [vbuf.at[slot, c] for c in range(n_chunk)]
    qt = qt_ref[...] * QK_SCALE
    knt = knt_ref[...]
    vnt = vnt_ref[...]
    per_t = B_TOPK * ppb
    rows, own_rows = [], []
    for t in range(t_new):
        qcol = qt[:, t:t + 1]
        own_rows.append(jnp.sum(qcol * knt, axis=0, keepdims=True))
        parts = []
        for r in range(B_TOPK):
            picked = sel_ref[b, (h * t_new + t) * B_TOPK + r] >= 0
            for pg in range(ppb):
                kc = kch[(t * B_TOPK + r) * ppb + pg][...]
                parts.append(jnp.where(picked, jnp.sum(qcol * kc, axis=0, keepdims=True), NEG))
        rows.append(jnp.concatenate(parts, axis=1))
    s_all = jnp.concatenate(rows, axis=0)
    s_own = jnp.concatenate(own_rows, axis=0)
    s_own = jnp.where(_iota((t_new, t_new), 1) <= _iota((t_new, t_new), 0), s_own, NEG)
    m = jnp.maximum(jnp.max(s_all, axis=1, keepdims=True), jnp.max(s_own, axis=1, keepdims=True))
    p_all = jnp.exp(s_all - m)
    p_own = jnp.exp(s_own - m)
    l = jnp.sum(p_all, axis=1, keepdims=True) + jnp.sum(p_own, axis=1, keepdims=True)
    lane_o = _iota((HEAD_DIM, t_new), 1)
    out = jnp.zeros((HEAD_DIM, t_new), F32)
    for t in range(t_new):
        accv = jnp.zeros((HEAD_DIM, page_rows), F32)
        for c in range(per_t):
            accv = accv + p_all[t:t + 1, c * page_rows:(c + 1) * page_rows] * vch[t * per_t + c][...]
        acc = (jnp.sum(accv, axis=1, keepdims=True)
               + jnp.sum(p_own[t:t + 1, :] * vnt, axis=1, keepdims=True))
        out = jnp.where(lane_o == t, acc / l[t:t + 1, :], out)
    o_ref[...] = out


def _moba_gather(qt, knt, vnt, pool_kt, pool_vt, page_table, sel):
    bsz, width, t_new = qt.shape
    _, n_heads, hd, page = pool_kt.shape
    ppb = B_BLOCK // page
    n_chunk = t_new * B_TOPK * ppb
    new_spec = pl.BlockSpec((None, hd, t_new), lambda b, h, pt, sl: (b, h, 0))
    hbm = pl.BlockSpec(memory_space=pl.ANY)
    return pl.pallas_call(
        functools.partial(_moba_gather_body, ppb=ppb),
        grid_spec=pltpu.PrefetchScalarGridSpec(
            num_scalar_prefetch=2,
            grid=(bsz, n_heads),
            in_specs=[new_spec, new_spec, new_spec, hbm, hbm],
            out_specs=pl.BlockSpec((None, hd, t_new), lambda b, h, pt, sl: (b, h, 0)),
            scratch_shapes=[
                pltpu.VMEM((2, n_chunk, hd, page), F32),
                pltpu.VMEM((2, n_chunk, hd, page), F32),
                pltpu.SemaphoreType.DMA((2, 2)),
            ],
        ),
        out_shape=jax.ShapeDtypeStruct((bsz, width, t_new), F32),
        compiler_params=_cparams(("arbitrary", "arbitrary")),
        name="moba_gather",
    )(page_table, sel, qt, knt, vnt, pool_kt, pool_vt)


def _silu(x):
    return x / (1.0 + jnp.exp(-x))


def _ffn_even_body(x_ref, a_ref, b_ref, wo_ref, g_ref, wg_ref, wu_ref, wd_ref, o_ref, hn_ref, acc_ref):
    f = pl.program_id(1)
    aw = a_ref.shape[1]

    @pl.when(f == 0)
    def _():
        x1 = (x_ref[...]
              + jnp.dot(a_ref[...], wo_ref[:aw, :], preferred_element_type=F32)
              + jnp.dot(b_ref[...], wo_ref[aw:, :], preferred_element_type=F32))
        acc_ref[...] = x1
        hn_ref[...] = _rms(x1, g_ref[...]).astype(BF16)

    hn = hn_ref[...]
    gate = jnp.dot(hn, wg_ref[...], preferred_element_type=F32)
    up = jnp.dot(hn, wu_ref[...], preferred_element_type=F32)
    act = (_silu(gate) * up).astype(BF16)
    acc_ref[...] += jnp.dot(act, wd_ref[...], preferred_element_type=F32)

    @pl.when(f == pl.num_programs(1) - 1)
    def _():
        o_ref[...] = acc_ref[...]


def _ffn_even(x, a, b, wo16, g, wg16, wu16, wd16, *, tm=640, tf=1408):
    rows, dm = x.shape
    aw = a.shape[1]
    bw = b.shape[1]
    dff = wg16.shape[1]
    return pl.pallas_call(
        _ffn_even_body,
        grid=(rows // tm, dff // tf),
        in_specs=[
            pl.BlockSpec((tm, dm), lambda i, f: (i, 0)),
            pl.BlockSpec((tm, aw), lambda i, f: (i, 0)),
            pl.BlockSpec((tm, bw), lambda i, f: (i, 0)),
            pl.BlockSpec((aw + bw, dm), lambda i, f: (0, 0)),
            pl.BlockSpec((1, dm), lambda i, f: (0, 0)),
            pl.BlockSpec((dm, tf), lambda i, f: (0, f)),
            pl.BlockSpec((dm, tf), lambda i, f: (0, f)),
            pl.BlockSpec((tf, dm), lambda i, f: (f, 0)),
        ],
        out_specs=pl.BlockSpec((tm, dm), lambda i, f: (i, 0)),
        out_shape=jax.ShapeDtypeStruct((rows, dm), F32),
        scratch_shapes=[pltpu.VMEM((tm, dm), BF16), pltpu.VMEM((tm, dm), F32)],
        compiler_params=_cparams(("arbitrary", "arbitrary")),
        name="ffn_even",
    )(x, a, b, wo16, g, wg16, wu16, wd16)


def _odd_in_body(x_ref, g_ref, w_ref, q_ref, k_ref, v_ref, kt_ref, vt_ref, *, n_prompt_tiles):
    cw = q_ref.shape[1]
    h = _rms(x_ref[...], g_ref[...]).astype(BF16)
    y = jnp.dot(h, w_ref[...], preferred_element_type=F32)
    q_ref[...] = y[:, :cw]
    k_ref[...] = y[:, cw:2 * cw]
    v_ref[...] = y[:, 2 * cw:]
    _feature_major_kv(pl.program_id(0), n_prompt_tiles, y[:, cw:2 * cw], y[:, 2 * cw:], kt_ref, vt_ref)


def _odd_in(x, bsz, t_len, g, w16, *, tm=256):
    rows, dm = x.shape
    cw = w16.shape[1] // 3
    npt = bsz * t_len // tm
    out = jax.ShapeDtypeStruct((rows, cw), F32)
    out_t = jax.ShapeDtypeStruct((bsz, cw, t_len), F32)
    spec = pl.BlockSpec((tm, cw), lambda i: (i, 0))
    fm_spec = _feature_major_spec(cw, tm, npt, t_len // tm)
    return pl.pallas_call(
        functools.partial(_odd_in_body, n_prompt_tiles=npt),
        grid=(rows // tm,),
        in_specs=[
            pl.BlockSpec((tm, dm), lambda i: (i, 0)),
            pl.BlockSpec((1, dm), lambda i: (0, 0)),
            pl.BlockSpec((dm, 3 * cw), lambda i: (0, 0)),
        ],
        out_specs=[spec, spec, spec, fm_spec, fm_spec],
        out_shape=[out, out, out, out_t, out_t],
        compiler_params=_cparams(("arbitrary",)),
        name="odd_in",
    )(x, g, w16)


def _dilated_prompt_body(*refs, patterns, tiles):
    q_refs, k_refs, v_refs = (refs[g * tiles:(g + 1) * tiles] for g in range(3))
    o_ref = refs[3 * tiles]
    state = refs[3 * tiles + 1:]
    t_len = q_refs[0].shape[0]
    blk = 128
    lo = _iota((blk, LANES), 1) < HEAD_DIM
    n_br = len(patterns)
    n_iter = t_len // blk
    masks = []
    for window, dil in patterns:
        assert window // dil == blk
        has_prev = t_len // (dil * blk) > 1
        n_keys = 2 * blk if has_prev else blk
        qrow = _iota((2 * blk, n_keys), 0) % blk
        kcol = _iota((2 * blk, n_keys), 1)
        if has_prev:
            masks.append(((kcol >= blk) & ((kcol - blk) <= qrow), (kcol < blk) & (kcol >= qrow)))
        else:
            masks.append((kcol <= qrow, None))

    def body(it, carry):
        for bi, (window, dil) in enumerate(patterns):
            nblk = t_len // (dil * blk)
            cur_ok, prev_ok = masks[bi]
            cls = it // nblk
            ib = it % nblk

            def rows_at(block, cls=cls, dil=dil):
                start = cls + dil * blk * block
                if dil == 1:
                    return pl.ds(pl.multiple_of(start, blk), blk)
                return pl.ds(start, blk, stride=dil)

            rows = rows_at(ib)
            prows = rows_at(jnp.maximum(ib - 1, 0))
            for tile in range(tiles):
                q_ref, k_ref, v_ref = q_refs[tile], k_refs[tile], v_refs[tile]
                lse_s, out_s = state[2 * (tile * n_br + bi):2 * (tile * n_br + bi) + 2]
                q = q_ref[rows, :] * QK_SCALE
                qst = jnp.concatenate([jnp.where(lo, q, 0.0), jnp.where(lo, 0.0, q)], axis=0).astype(BF16)
                kk = k_ref[rows, :].astype(BF16)
                vv = v_ref[rows, :].astype(BF16)
                ok = cur_ok
                if prev_ok is not None:
                    kk = jnp.concatenate([k_ref[prows, :].astype(BF16), kk], axis=0)
                    vv = jnp.concatenate([v_ref[prows, :].astype(BF16), vv], axis=0)
                    ok = cur_ok | (prev_ok & (ib > 0))
                s = jnp.where(ok, lax.dot_general(qst, kk, NT, preferred_element_type=F32), NEG)
                m = jnp.max(s, axis=1, keepdims=True)
                p = jnp.exp(s - m)
                l = jnp.sum(p, axis=1, keepdims=True)
                pv = jnp.dot(p.astype(BF16), vv, preferred_element_type=F32)
                lse = m + jnp.log(l)
                outn = pv / l
                lse_s[rows, :] = jnp.where(lo, lse[:blk], lse[blk:])
                out_s[rows, :] = jnp.where(lo, outn[:blk], outn[blk:])
        return carry

    lax.fori_loop(0, n_iter, body, 0)

    chunk = 128

    def merge(c, carry):
        rows = pl.ds(pl.multiple_of(c * chunk, chunk), chunk)
        for tile in range(tiles):
            st = state[2 * tile * n_br:2 * (tile + 1) * n_br]
            ms = [st[2 * r][rows, :] for r in range(n_br)]
            m = functools.reduce(jnp.maximum, ms)
            num = jnp.zeros((chunk, LANES), F32)
            den = jnp.zeros((chunk, LANES), F32)
            for r in range(n_br):
                w = jnp.exp(ms[r] - m)
                num = num + w * st[2 * r + 1][rows, :]
                den = den + w
            o_ref[rows, tile * LANES:(tile + 1) * LANES] = (num / den).astype(BF16)
        return carry

    lax.fori_loop(0, t_len // chunk, merge, 0)


def _dilated_prompt(q, k, v, bsz, t_len, *, tiles=4):
    width = q.shape[1]
    specs = [pl.BlockSpec((t_len, LANES), lambda b, j, t=t: (b, j * tiles + t)) for t in range(tiles)]
    return pl.pallas_call(
        functools.partial(_dilated_prompt_body, patterns=C_PATTERNS, tiles=tiles),
        grid=(bsz, width // (tiles * LANES)),
        in_specs=specs * 3,
        out_specs=pl.BlockSpec((t_len, tiles * LANES), lambda b, j: (b, j)),
        out_shape=jax.ShapeDtypeStruct((bsz * t_len, width), BF16),
        scratch_shapes=[pltpu.VMEM((t_len, LANES), F32)] * (2 * len(C_PATTERNS) * tiles),
        compiler_params=_cparams(("arbitrary", "arbitrary")),
        name="dilated_prompt",
    )(*([q] * tiles), *([k] * tiles), *([v] * tiles))


def _dilated_sample_body(q_ref, kn_ref, vn_ref, kc_ref, vc_ref, o_ref, *, patterns):
    t_new, cols = q_ref.shape
    w_len = kc_ref.shape[1]
    nrow = 2 * t_new
    lane = _iota((t_new, LANES), 1)

    def mult(delta):
        cnt = jnp.zeros(delta.shape, F32)
        for window, dil in patterns:
            hit = (delta >= 0) & (delta <= window) & ((delta % dil) == 0)
            cnt = cnt + hit.astype(F32)
        return cnt

    t_c = _iota((nrow, w_len), 0) % t_new
    w_c = mult(w_len + t_c - _iota((nrow, w_len), 1))
    t_n = _iota((nrow, LANES), 0) % t_new
    c_n = _iota((nrow, LANES), 1)
    w_n = jnp.where(c_n < t_new, mult(t_n - c_n), 0.0)
    pad = jnp.zeros((LANES - t_new, LANES), F32)
    for tile in range(cols // LANES):
        cs = slice(tile * LANES, (tile + 1) * LANES)
        q = q_ref[:, cs] * QK_SCALE
        qst = jnp.concatenate([jnp.where(lane < HEAD_DIM, q, 0.0), jnp.where(lane >= HEAD_DIM, q, 0.0)], axis=0)
        qst = qst.astype(BF16)
        kn = jnp.concatenate([kn_ref[:, cs], pad], axis=0).astype(BF16)
        vn = jnp.concatenate([vn_ref[:, cs], pad], axis=0).astype(BF16)
        s_c = jnp.dot(qst, kc_ref[cs, :].astype(BF16), preferred_element_type=F32)
        s_n = lax.dot_general(qst, kn, NT, preferred_element_type=F32)
        s_c = jnp.where(w_c > 0, s_c, NEG)
        s_n = jnp.where(w_n > 0, s_n, NEG)
        m = jnp.maximum(jnp.max(s_c, axis=1, keepdims=True), jnp.max(s_n, axis=1, keepdims=True))
        p_c = w_c * jnp.exp(s_c - m)
        p_n = w_n * jnp.exp(s_n - m)
        l = jnp.sum(p_c, axis=1, keepdims=True) + jnp.sum(p_n, axis=1, keepdims=True)
        acc = (lax.dot_general(p_c.astype(BF16), vc_ref[cs, :].astype(BF16), NT, preferred_element_type=F32)
               + jnp.dot(p_n.astype(BF16), vn, preferred_element_type=F32))
        out = acc / l
        o_ref[:, cs] = jnp.where(lane < HEAD_DIM, out[:t_new], out[t_new:])


def _dilated_sample(q, k, v, row_block0, cache_kt, cache_vt, *, tiles=4):
    bsz, width, w_len = cache_kt.shape
    t_new = 8
    cols = tiles * LANES
    new_spec = pl.BlockSpec((t_new, cols), lambda b, j: (row_block0 + b, j))
    cache_spec = pl.BlockSpec((None, cols, w_len), lambda b, j: (b, j, 0))
    return pl.pallas_call(
        functools.partial(_dilated_sample_body, patterns=C_PATTERNS),
        grid=(bsz, width // cols),
        in_specs=[new_spec, new_spec, new_spec, cache_spec, cache_spec],
        out_specs=pl.BlockSpec((t_new, cols), lambda b, j: (b, j)),
        out_shape=jax.ShapeDtypeStruct((bsz * t_new, width), F32),
        compiler_params=_cparams(("arbitrary", "arbitrary")),
        name="dilated_sample",
    )(q, k, v, cache_kt, cache_vt)


def _moe_route_body(x_ref, c_ref, wo_ref, g_ref, r_ref,
                    x3_ref, hn_ref, comb_ref, pos_ref, post_ref, cnt_ref, before_ref):
    tb = x_ref.shape[0]

    @pl.when(pl.program_id(0) == 0)
    def _():
        before_ref[...] = jnp.where(_iota((tb, tb), 1) < _iota((tb, tb), 0), 1.0, 0.0).astype(BF16)

    x3 = x_ref[...] + jnp.dot(c_ref[...], wo_ref[...], preferred_element_type=F32)
    x3_ref[...] = x3
    hf = _rms(x3, g_ref[...])
    hn = hf.astype(BF16)
    hn_ref[...] = hn
    lane = _iota((tb, LANES), 1)
    h_lo = (hf - hn.astype(F32)).astype(BF16)
    r = r_ref[...]
    r_hi = r.astype(BF16)
    r_lo = (r - r_hi.astype(F32)).astype(BF16)
    logits = (jnp.dot(hn, r_hi, preferred_element_type=F32)
              + (jnp.dot(hn, r_lo, preferred_element_type=F32)
                 + jnp.dot(h_lo, r_hi, preferred_element_type=F32)))
    lg = jnp.where(lane < N_EXPERTS, logits, -jnp.inf)
    m1 = jnp.max(lg, axis=1, keepdims=True)
    p1 = lane == jnp.min(jnp.where(lg == m1, lane, LANES), axis=1, keepdims=True)
    lg2 = jnp.where(p1, -jnp.inf, lg)
    m2 = jnp.max(lg2, axis=1, keepdims=True)
    p2 = lane == jnp.min(jnp.where(lg2 == m2, lane, LANES), axis=1, keepdims=True)
    e2 = jnp.exp(m2 - m1)
    den = 1.0 + e2
    comb_ref[...] = jnp.where(p1, 1.0 / den, 0.0) + jnp.where(p2, e2 / den, 0.0)
    routed = p1 | p2
    ind = jnp.where(routed, 1.0, 0.0)
    slot = jnp.where(routed, jnp.dot(before_ref[...], ind.astype(BF16), preferred_element_type=F32), -1.0)
    pos_ref[...] = slot
    eye = jnp.where(_iota((LANES, LANES), 0) == _iota((LANES, LANES), 1), 1.0, 0.0)
    post_ref[...] = lax.dot_general(eye, slot, NT, precision=HIGHEST, preferred_element_type=F32)
    cnt = jnp.sum(ind, axis=0, keepdims=True)
    cnt_ref[...] = jnp.broadcast_to(cnt, cnt_ref.shape).astype(jnp.int32)


def _moe_route(x, c, wo16, g, router_pad, *, tb):
    rows, dm = x.shape
    nblk = rows // tb
    row_spec = lambda width: pl.BlockSpec((tb, width), lambda i: (i, 0))
    return pl.pallas_call(
        _moe_route_body,
        grid=(nblk,),
        in_specs=[
            row_spec(dm), row_spec(dm),
            pl.BlockSpec((dm, dm), lambda i: (0, 0)),
            pl.BlockSpec((1, dm), lambda i: (0, 0)),
            pl.BlockSpec((dm, LANES), lambda i: (0, 0)),
        ],
        out_specs=[
            row_spec(dm), row_spec(dm), row_spec(LANES), row_spec(LANES),
            pl.BlockSpec((None, LANES, tb), lambda i: (i, 0, 0)),
            pl.BlockSpec((None, 8, LANES), lambda i: (i, 0, 0)),
        ],
        out_shape=[
            jax.ShapeDtypeStruct((rows, dm), F32),
            jax.ShapeDtypeStruct((rows, dm), BF16),
            jax.ShapeDtypeStruct((rows, LANES), F32),
            jax.ShapeDtypeStruct((rows, LANES), F32),
            jax.ShapeDtypeStruct((nblk, LANES, tb), F32),
            jax.ShapeDtypeStruct((nblk, 8, LANES), jnp.int32),
        ],
        scratch_shapes=[pltpu.VMEM((tb, tb), BF16)],
        compiler_params=_cparams(("arbitrary",)),
        name="moe_route",
    )(x, c, wo16, g, router_pad)


def _moe_expert_body(cnt_ref, acc_ref, hn_ref, comb_ref, pos_ref, post_ref, gf_ref,
                     wg_ref, wu_ref, wd_ref, o_ref, *tail_ref, e, last, tile_rows, tail_rows):
    i = pl.program_id(0)
    tb = hn_ref.shape[0]
    n_tok = cnt_ref[i, e]
    lane = _iota((tb, LANES), 1)
    w_col = jnp.sum(jnp.where(lane == e, comb_ref[...], 0.0), axis=1, keepdims=True)
    slot_col = jnp.sum(jnp.where(lane == e, pos_ref[...], 0.0), axis=1, keepdims=True)
    slot_row = post_ref[pl.ds(e, 1), :]
    finish = (lambda v: _rms(v, gf_ref[...])) if last else (lambda v: v)

    def tile_out(rows, base):
        sub_of_row = _iota((rows, tb), 0).astype(F32)
        sub_of_col = _iota((tb, rows), 1).astype(F32)
        gather = jnp.where(slot_row - base == sub_of_row, 1.0, 0.0).astype(BF16)
        xe = jnp.dot(gather, hn_ref[...], preferred_element_type=F32).astype(BF16)
        gate = jnp.dot(xe, wg_ref[...], preferred_element_type=F32)
        up = jnp.dot(xe, wu_ref[...], preferred_element_type=F32)
        act = (_silu(gate) * up).astype(BF16)
        y = jnp.dot(act, wd_ref[...], preferred_element_type=F32).astype(BF16)
        scatter = jnp.where(slot_col - base == sub_of_col, 1.0, 0.0).astype(BF16)
        return w_col * jnp.dot(scatter, y, preferred_element_type=F32)

    @pl.when(n_tok == 0)
    def _():
        o_ref[...] = finish(acc_ref[...])

    lower = 0
    for rows in tile_rows:
        @pl.when((n_tok > lower) & (n_tok <= rows))
        def _(rows=rows):
            o_ref[...] = finish(acc_ref[...] + tile_out(rows, 0.0))
        lower = rows
    big = tile_rows[-1]

    @pl.when(n_tok > big)
    def _():
        o_ref[...] = acc_ref[...]

        def body(s, carry):
            o_ref[...] += tile_out(big, (s * big).astype(F32))
            return carry
        lax.fori_loop(0, (n_tok + big - 1) // big, body, 0)
        o_ref[...] = finish(o_ref[...])

    if last:
        @pl.when(i == pl.num_programs(0) - 1)
        def _():
            tb_rows = o_ref.shape[0]
            tail_ref[0][...] = o_ref[tb_rows - tail_rows:, :]


def _moe_experts(counts, x3, hn, comb, pos, post, g_final, wg16, wu16, wd16, *, tb, tile_rows, tail_rows):
    rows, dm = x3.shape
    n_exp, _, dff = wg16.shape
    assert rows % tb == 0 and tail_rows <= tb
    once = pl.Buffered(1)
    acc = x3
    for e in range(n_exp):
        last = e == n_exp - 1
        row_spec = pl.BlockSpec((tb, dm), lambda i, cnt: (i, 0))
        if last:
            out_specs = [row_spec, pl.BlockSpec((tail_rows, dm), lambda i, cnt: (0, 0))]
            out_shape = [jax.ShapeDtypeStruct((rows - tail_rows, dm), F32),
                         jax.ShapeDtypeStruct((tail_rows, dm), F32)]
        else:
            out_specs, out_shape = row_spec, jax.ShapeDtypeStruct((rows, dm), F32)
        acc = pl.pallas_call(
            functools.partial(_moe_expert_body, e=e, last=last, tile_rows=tile_rows, tail_rows=tail_rows),
            grid_spec=pltpu.PrefetchScalarGridSpec(
                num_scalar_prefetch=1,
                grid=(rows // tb,),
                in_specs=[
                    pl.BlockSpec((tb, dm), lambda i, cnt: (i, 0)),
                    pl.BlockSpec((tb, dm), lambda i, cnt: (i, 0)),
                    pl.BlockSpec((tb, LANES), lambda i, cnt: (i, 0)),
                    pl.BlockSpec((tb, LANES), lambda i, cnt: (i, 0)),
                    pl.BlockSpec((None, LANES, tb), lambda i, cnt: (i, 0, 0)),
                    pl.BlockSpec((1, dm), lambda i, cnt: (0, 0)),
                    pl.BlockSpec((None, dm, dff), lambda i, cnt, e=e: (e, 0, 0), pipeline_mode=once),
                    pl.BlockSpec((None, dm, dff), lambda i, cnt, e=e: (e, 0, 0), pipeline_mode=once),
                    pl.BlockSpec((None, dff, dm), lambda i, cnt, e=e: (e, 0, 0), pipeline_mode=once),
                ],
                out_specs=out_specs,
            ),
            out_shape=out_shape,
            compiler_params=_cparams(("arbitrary",)),
            name=f"moe_expert{e}",
        )(counts, acc, hn, comb, pos, post, g_final, wg16, wu16, wd16)
    return acc


def _moe(x, c, wo16, g, router_pad, g_final, wg16, wu16, wd16, tail_rows, *, tb=416,
         tile_rows=(128, 160, 192, 256)):
    x3, hn, comb, pos, post, cnt = _moe_route(x, c, wo16, g, router_pad, tb=tb)
    counts = cnt[:, 0, :N_EXPERTS]
    return _moe_experts(counts, x3, hn, comb, pos, post, g_final, wg16, wu16, wd16, tb=tb,
                        tile_rows=tile_rows, tail_rows=tail_rows)


def kernel(x_prompt, x_sample, cache_b_k, cache_b_v, cache_c_k, cache_c_v, page_table, norm_mix, norm_ffn,
           norm_final, w_in_even, w_out_even, a_ln_g, a_ln_b, a_ws, a_bs, ffn_gate, ffn_up, ffn_down,
           w_in_odd, w_out_odd, router, moe_gate, moe_up, moe_down):
    bsz, t_len, dm = x_prompt.shape
    dbsz, t_new, _ = x_sample.shape
    assert norm_mix.shape[0] == 2 and w_in_even.shape[0] == 1 and w_in_odd.shape[0] == 1
    assert cache_b_k.shape[0] == 1 and cache_c_k.shape[0] == 1
    assert t_new == 8 and A_CHUNK % t_new == 0
    n_p = bsz * t_len
    n_s = dbsz * t_new
    aw = a_ln_g.shape[-1]
    n_groups = aw // HEAD_DIM
    seq_per_chunk = A_CHUNK // t_new


    ws = a_ws[0]
    ws_sample = jnp.tile(ws[:, :t_new, :t_new], (1, seq_per_chunk, seq_per_chunk))
    ws2 = jnp.stack([ws, ws_sample])
    bias_p = jnp.repeat(a_bs[0].T, HEAD_DIM, axis=1)
    bias_s = jnp.repeat(jnp.tile(a_bs[0][:, :t_new].T, (seq_per_chunk, 1)), HEAD_DIM, axis=1)
    bias2 = jnp.stack([bias_p, bias_s])
    row = lambda vec: vec.reshape(1, -1)
    router_pad = jnp.pad(router[0], ((0, 0), (0, LANES - router.shape[-1])))

    w_even = w_in_even[0].astype(BF16)
    a_out, v_a, q_b, k_b, v_b, kt_b, vt_b, v_a_last, x = _even_in(
        x_prompt.reshape(n_p, dm), x_sample.reshape(n_s, dm), bsz, t_len, row(norm_mix[0]), w_even, row(a_ln_g[0]), row(a_ln_b[0]), ws2, bias2)
    b_prompt = _moba_prompt(q_b, k_b, v_b, bsz, t_len)
    page = cache_b_k.shape[2]
    bw = q_b.shape[1]
    feature_major = lambda c: jnp.transpose(c, (0, 1, 3, 4, 2))
    pool_kt = feature_major(cache_b_k).reshape(-1, B_HEADS, HEAD_DIM, page)
    pool_vt = feature_major(cache_b_v).reshape(-1, B_HEADS, HEAD_DIM, page)
    sel = _moba_select(q_b, n_p // t_new, pool_kt.reshape(-1, bw, page), page_table)
    sel = sel.reshape(dbsz, B_HEADS * t_new, LANES)[:, :, :B_TOPK].reshape(dbsz, -1)
    new_t = lambda a: jnp.transpose(a[n_p:].reshape(dbsz, t_new, -1), (0, 2, 1))
    ppb = B_BLOCK // page
    logical = (jnp.maximum(sel, 0)[:, :, None] * ppb + jnp.arange(ppb, dtype=jnp.int32)).reshape(dbsz, -1)
    hit = logical[:, :, None] == jnp.arange(page_table.shape[1], dtype=jnp.int32)
    sel_pages = jnp.sum(jnp.where(hit, page_table[:, None, :], 0), axis=-1)
    b_sample_t = _moba_gather(new_t(q_b), new_t(k_b), new_t(v_b), pool_kt, pool_vt, sel_pages, sel)
    b_sample = jnp.transpose(b_sample_t, (0, 2, 1)).reshape(n_s, bw)
    b_out = jnp.concatenate([b_prompt, b_sample.astype(BF16)], axis=0)
    x2 = _ffn_even(x, a_out, b_out, w_out_even[0].astype(BF16), row(norm_ffn[0]),
                   ffn_gate[0].astype(BF16), ffn_up[0].astype(BF16), ffn_down[0].astype(BF16))

    w_odd = w_in_odd[0].astype(BF16)
    q_c, k_c, v_c, kt_c, vt_c = _odd_in(x2, bsz, t_len, row(norm_mix[1]), w_odd)
    c_prompt = _dilated_prompt(q_c, k_c, v_c, bsz, t_len)
    cw = q_c.shape[1]
    w_len = cache_c_k.shape[2]
    c_sample = _dilated_sample(q_c, k_c, v_c, n_p // t_new,
                               feature_major(cache_c_k).reshape(-1, cw, w_len),
                               feature_major(cache_c_v).reshape(-1, cw, w_len))
    c_out = jnp.concatenate([c_prompt, c_sample.astype(BF16)], axis=0)
    y_p, y_s = _moe(x2, c_out, w_out_odd[0].astype(BF16), row(norm_ffn[1]), router_pad, row(norm_final),
                    moe_gate[0].astype(BF16), moe_up[0].astype(BF16), moe_down[0].astype(BF16), n_s)

    bh = (B_HEADS, HEAD_DIM)
    ch = (cw // HEAD_DIM, HEAD_DIM)
    assert t_len % 256 == 0
    c_keep_p = min(C_PATTERNS[-1][0], t_len)
    leaf = lambda a_t, hd: jnp.transpose(a_t.reshape(1, bsz, *hd, a_t.shape[-1]), (0, 1, 4, 2, 3))
    return (
        y_p.reshape(bsz, t_len, dm),
        y_s.reshape(dbsz, t_new, dm),
        v_a_last.reshape(1, bsz, A_CHUNK, aw),
        v_a[n_p:].reshape(1, dbsz, t_new, aw),
        leaf(kt_b, bh),
        leaf(vt_b, bh),
        k_b[n_p:].reshape(1, dbsz, t_new, *bh),
        v_b[n_p:].reshape(1, dbsz, t_new, *bh),
        leaf(kt_c[:, :, t_len - c_keep_p:], ch),
        leaf(vt_c[:, :, t_len - c_keep_p:], ch),
        k_c[n_p:].reshape(1, dbsz, t_new, *ch),
        v_c[n_p:].reshape(1, dbsz, t_new, *ch),
    )
```

```python
import functools

import jax
import jax.numpy as jnp
from jax import lax
from jax.experimental import pallas as pl
from jax.experimental.pallas import tpu as pltpu

F32 = jnp.float32
BF16 = jnp.bfloat16
HIGHEST = lax.Precision.HIGHEST

HEAD_DIM = 64
LANES = 128
A_CHUNK = 128
B_HEADS = 8
B_BLOCK = 256
B_TOPK = 3
C_PATTERNS = ((128, 1), (512, 4), (2048, 16))
N_EXPERTS = 8
RMS_EPS = 1e-6
LN_EPS = 1e-5
NEG = -1e30
QK_SCALE = HEAD_DIM ** -0.5
VMEM_LIMIT = 56 * 1024 * 1024
WEIGHT_STAGE_ROWS = 256

NT = (((1,), (1,)), ((), ()))


def _cparams(sem):
    return pltpu.CompilerParams(dimension_semantics=sem, vmem_limit_bytes=VMEM_LIMIT)


def _rms(x, g):
    return x * lax.rsqrt(jnp.mean(x * x, axis=-1, keepdims=True) + RMS_EPS) * g


def _gelu(x):
    return 0.5 * x * (1.0 + lax.erf(x * 0.7071067811865476))


def _iota(shape, dim):
    return lax.broadcasted_iota(jnp.int32, shape, dim)


def _feature_major_kv(i, n_prompt_tiles, k_tile, v_tile, kt_ref, vt_ref):
    @pl.when(i < n_prompt_tiles)
    def _():
        kt_ref[...] = k_tile.T
        vt_ref[...] = v_tile.T


def _feature_major_spec(width, tm, n_prompt_tiles, tiles_per_seq):
    def imap(i):
        ic = jnp.minimum(i, n_prompt_tiles - 1)
        return (ic // tiles_per_seq, 0, ic % tiles_per_seq)
    return pl.BlockSpec((None, width, tm), imap)


def _even_in_body(xp_ref, xs_ref, g_ref, w_ref, lng_ref, lnb_ref, ws_ref, bias_ref,
                  a_ref, v_ref, q_ref, k_ref, vv_ref, kt_ref, vt_ref, vlast_ref, xcat_ref,
                  *, n_prompt_tiles, tiles_per_seq, aw, bw):
    i = pl.program_id(0)
    tm = xp_ref.shape[0]
    x = jnp.where(i < n_prompt_tiles, xp_ref[...], xs_ref[...])
    xcat_ref[...] = x
    h = _rms(x, g_ref[...]).astype(BF16)
    y = jnp.dot(h, w_ref[...], preferred_element_type=F32)
    _feature_major_kv(i, n_prompt_tiles, y[:, 2 * aw + bw:2 * aw + 2 * bw], y[:, 2 * aw + 2 * bw:], kt_ref, vt_ref)
    u = _gelu(y[:, :aw])
    v = _gelu(y[:, aw:2 * aw])
    mu = jnp.mean(v, axis=-1, keepdims=True)
    vc = v - mu
    var = jnp.mean(vc * vc, axis=-1, keepdims=True)
    v = vc * lax.rsqrt(var + LN_EPS) * lng_ref[...] + lnb_ref[...]
    v_ref[...] = v

    @pl.when((i < n_prompt_tiles) & (i % tiles_per_seq == tiles_per_seq - 1))
    def _():
        vlast_ref[...] = v[tm - A_CHUNK:, :]
    q_ref[...] = y[:, 2 * aw:2 * aw + bw]
    k_ref[...] = y[:, 2 * aw + bw:2 * aw + 2 * bw]
    vv_ref[...] = y[:, 2 * aw + 2 * bw:]

    row = _iota((A_CHUNK, A_CHUNK), 0)
    col = _iota((A_CHUNK, A_CHUNK), 1)
    is_prompt = i < n_prompt_tiles
    ok = (col <= row) & (((row // 8) == (col // 8)) | is_prompt)
    lo = _iota((A_CHUNK, LANES), 1) < HEAD_DIM
    v16 = v.astype(BF16)
    n_groups = aw // HEAD_DIM
    wm = [jnp.where(ok, ws_ref[g], 0.0).astype(BF16) for g in range(n_groups)]
    for c in range(tm // A_CHUNK):
        rs = slice(c * A_CHUNK, (c + 1) * A_CHUNK)
        for j in range(aw // LANES):
            cs = slice(j * LANES, (j + 1) * LANES)
            vj = v16[rs, cs]
            ma = jnp.dot(wm[2 * j], vj, preferred_element_type=F32)
            mb = jnp.dot(wm[2 * j + 1], vj, preferred_element_type=F32)
            mixed = jnp.where(lo, ma, mb) + bias_ref[:, cs]
            a_ref[rs, cs] = (u[rs, cs] * mixed).astype(BF16)


def _even_in(xp, xs, bsz, t_len, g, w16, lng, lnb, ws2, bias2, *, tm=256):
    dm = xp.shape[1]
    assert xs.shape[0] == tm
    rows = xp.shape[0] + tm
    aw = lng.shape[-1]
    n_in = w16.shape[1]
    bw = (n_in - 2 * aw) // 3
    n_groups = aw // HEAD_DIM
    npt = bsz * t_len // tm
    sel = lambda i: jnp.minimum(i // npt, 1)
    row_spec = lambda width: pl.BlockSpec((tm, width), lambda i: (i, 0))
    tps = t_len // tm
    fm_spec = _feature_major_spec(bw, tm, npt, tps)
    return pl.pallas_call(
        functools.partial(_even_in_body, n_prompt_tiles=npt, tiles_per_seq=tps, aw=aw, bw=bw),
        grid=(rows // tm,),
        in_specs=[
            pl.BlockSpec((tm, dm), lambda i: (jnp.minimum(i, npt - 1), 0)),
            pl.BlockSpec((tm, dm), lambda i: (0, 0)),
            pl.BlockSpec((1, dm), lambda i: (0, 0)),
            pl.BlockSpec((dm, n_in), lambda i: (0, 0)),
            pl.BlockSpec((1, aw), lambda i: (0, 0)),
            pl.BlockSpec((1, aw), lambda i: (0, 0)),
            pl.BlockSpec((None, n_groups, A_CHUNK, A_CHUNK), lambda i: (sel(i), 0, 0, 0)),
            pl.BlockSpec((None, A_CHUNK, aw), lambda i: (sel(i), 0, 0)),
        ],
        out_specs=[row_spec(aw), row_spec(aw), row_spec(bw), row_spec(bw), row_spec(bw), fm_spec, fm_spec,
                   pl.BlockSpec((None, A_CHUNK, aw), lambda i: (jnp.minimum(i, npt - 1) // tps, 0, 0)),
                   row_spec(dm)],
        out_shape=[
            jax.ShapeDtypeStruct((rows, aw), BF16),
            jax.ShapeDtypeStruct((rows, aw), F32),
            jax.ShapeDtypeStruct((rows, bw), F32),
            jax.ShapeDtypeStruct((rows, bw), F32),
            jax.ShapeDtypeStruct((rows, bw), F32),
            jax.ShapeDtypeStruct((bsz, bw, t_len), F32),
            jax.ShapeDtypeStruct((bsz, bw, t_len), F32),
            jax.ShapeDtypeStruct((bsz, A_CHUNK, aw), F32),
            jax.ShapeDtypeStruct((rows, dm), F32),
        ],
        compiler_params=_cparams(("arbitrary",)),
        name="even_in",
    )(xp, xs, g, w16, lng, lnb, ws2, bias2)


def _moba_prompt_body(q_ref, k_ref, v_ref, o_ref, k16, v16, mt_ref, s_scr, p_scr, *, nb):
    i = pl.program_id(1)
    t_len, width = k_ref.shape
    n_gate = B_HEADS * nb

    @pl.when(i == 0)
    def _():
        k = k_ref[...]
        k16[...] = k.astype(BF16)
        v16[...] = v_ref[...].astype(BF16)
        r = _iota((LANES, t_len), 0)
        key = _iota((LANES, t_len), 1)
        avg = jnp.where(((key // B_BLOCK) == (r % nb)) & (r < n_gate), 1.0 / B_BLOCK, 0.0).astype(F32)
        mt = jnp.dot(avg, k, precision=HIGHEST, preferred_element_type=F32)
        rr = _iota((LANES, width), 0)
        cc = _iota((LANES, width), 1)
        mt_ref[...] = jnp.where(((rr // nb) == (cc // HEAD_DIM)) & (rr < n_gate), mt, 0.0)

    q = q_ref[...]
    tq = q.shape[0]
    gate = lax.dot_general(q, mt_ref[...], NT, precision=HIGHEST, preferred_element_type=F32)
    lane = _iota((tq, LANES), 1)
    n_of = lane % nb
    valid = (n_of < i) & (lane < n_gate)
    gm = jnp.where(valid, gate, -jnp.inf)
    rank = jnp.zeros((tq, LANES), jnp.int32)
    for s in range(1, nb):
        fwd = pltpu.roll(gm, LANES - s, 1)
        bwd = pltpu.roll(gm, nb - s, 1)
        wrap = (n_of + s) >= nb
        other = jnp.where(wrap, bwd, fwd)
        m_idx = jnp.where(wrap, n_of + s - nb, n_of + s)
        beats = (other > gm) | ((other == gm) & (m_idx < n_of))
        rank = rank + beats.astype(jnp.int32)
    selbias = jnp.where(valid & (rank < B_TOPK), 0.0, NEG)

    row2 = _iota((2 * tq, B_BLOCK), 0) % tq
    col2 = _iota((2 * tq, B_BLOCK), 1)
    causal_bias = jnp.where(col2 <= row2, 0.0, NEG)
    lo = lane < HEAD_DIM
    own = pl.multiple_of(i * B_BLOCK, B_BLOCK)
    half = B_BLOCK // 2

    key_lane = _iota((B_BLOCK, LANES), 1)

    group = s_scr.shape[0]

    def zero_body(n, carry):
        for g in range(group):
            p_scr[g, n] = jnp.zeros((2 * tq, B_BLOCK), BF16)
        return carry

    lax.fori_loop(i + 1, nb, zero_body, 0)
    for j0 in range(0, width // LANES, group):
        cols = [slice((j0 + g) * LANES, (j0 + g + 1) * LANES) for g in range(group)]
        q_aug, m_tiles = [], []
        for g in range(group):
            j = j0 + g
            qj = q[:, cols[g]] * QK_SCALE
            qst = jnp.concatenate([jnp.where(lo, qj, 0.0), jnp.where(lo, 0.0, qj)], axis=0).astype(BF16)
            s = (lax.dot_general(qst, k16[pl.ds(own, B_BLOCK), cols[g]], NT, preferred_element_type=F32)
                 + causal_bias)
            s_scr[g, i] = s
            m_tiles.append(jnp.maximum(s[:, :half], s[:, half:]))
            sb = []
            for a in range(2):
                off = (2 * j + a) * nb
                moved = selbias if off == 0 else pltpu.roll(selbias, LANES - off, 1)
                sb.append(jnp.where(lane < nb, moved, 0.0))
            q_aug.append(jnp.concatenate([qst, jnp.concatenate(sb, axis=0).astype(BF16)], axis=1))

        def score_body(n, m_tiles, q_aug=q_aug, cols=cols):
            start = pl.multiple_of(n * B_BLOCK, B_BLOCK)
            onehot = jnp.where(key_lane == n, 1.0, 0.0).astype(BF16)
            out = []
            for g in range(group):
                k_aug = jnp.concatenate([k16[pl.ds(start, B_BLOCK), cols[g]], onehot], axis=1)
                s = lax.dot_general(q_aug[g], k_aug, NT, preferred_element_type=F32)
                s_scr[g, n] = s
                out.append(jnp.maximum(m_tiles[g], jnp.maximum(s[:, :half], s[:, half:])))
            return tuple(out)

        m_tiles = lax.fori_loop(0, i, score_body, tuple(m_tiles))
        ms = [jnp.max(m_tiles[g], axis=1, keepdims=True) for g in range(group)]

        def prob_body(n, l_tiles, ms=ms):
            out = []
            for g in range(group):
                p = jnp.exp(s_scr[g, n] - ms[g])
                p_scr[g, n] = p.astype(BF16)
                out.append(l_tiles[g] + p[:, :half] + p[:, half:])
            return tuple(out)

        l_tiles = lax.fori_loop(0, i + 1, prob_body,
                                tuple(jnp.zeros((2 * tq, half), F32) for _ in range(group)))
        for g in range(group):
            l = jnp.sum(l_tiles[g], axis=1, keepdims=True)
            acc = jnp.dot(p_scr[g, 0], v16[0:B_BLOCK, cols[g]], preferred_element_type=F32)
            for n in range(1, nb):
                acc = acc + jnp.dot(p_scr[g, n], v16[n * B_BLOCK:(n + 1) * B_BLOCK, cols[g]],
                                    preferred_element_type=F32)
            o = acc / l
            o_ref[:, cols[g]] = jnp.where(lo, o[:tq], o[tq:]).astype(BF16)


def _moba_prompt(q, k, v, bsz, t_len):
    width = q.shape[1]
    nb = t_len // B_BLOCK
    return pl.pallas_call(
        functools.partial(_moba_prompt_body, nb=nb),
        grid=(bsz, nb),
        in_specs=[
            pl.BlockSpec((B_BLOCK, width), lambda b, i: (b * nb + i, 0)),
            pl.BlockSpec((t_len, width), lambda b, i: (b, 0)),
            pl.BlockSpec((t_len, width), lambda b, i: (b, 0)),
        ],
        out_specs=pl.BlockSpec((B_BLOCK, width), lambda b, i: (b * nb + i, 0)),
        out_shape=jax.ShapeDtypeStruct((bsz * t_len, width), BF16),
        scratch_shapes=[
            pltpu.VMEM((t_len, width), BF16),
            pltpu.VMEM((t_len, width), BF16),
            pltpu.VMEM((LANES, width), F32),
            pltpu.VMEM((2, nb, 2 * B_BLOCK, B_BLOCK), F32),
            pltpu.VMEM((2, nb, 2 * B_BLOCK, B_BLOCK), BF16),
        ],
        compiler_params=_cparams(("arbitrary", "arbitrary")),
        name="moba_prompt",
    )(q, k, v)


def _moba_select_body(pt_ref, q_ref, *refs, npg, ppb, n_blocks):
    pages = refs[:npg]
    idx_ref = refs[npg]
    mt_ref = refs[npg + 1]
    s = pl.program_id(1)
    t_new, width = q_ref.shape
    bps = npg // ppb

    @pl.when(s == 0)
    def _():
        mt_ref[...] = jnp.zeros(mt_ref.shape, F32)

    lane_w = _iota((width, LANES), 1)
    cur = mt_ref[...]
    for blk in range(bps):
        tot = pages[blk * ppb][...]
        for r in range(1, ppb):
            tot = tot + pages[blk * ppb + r][...]
        col = jnp.sum(tot, axis=1, keepdims=True) * (1.0 / B_BLOCK)
        cur = jnp.where(lane_w == s * bps + blk, col, cur)
    mt_ref[...] = cur

    @pl.when(s == pl.num_programs(1) - 1)
    def _():
        ht = B_HEADS * t_new
        rr = _iota((ht, width), 0)
        cc = _iota((ht, width), 1)
        q = q_ref[...]
        qst = jnp.where((rr // t_new) == (cc // HEAD_DIM), jnp.concatenate([q] * B_HEADS, axis=0), 0.0)
        gate = jnp.dot(qst, cur, precision=HIGHEST, preferred_element_type=F32)
        lane = _iota((ht, LANES), 1)
        g = jnp.where(lane < n_blocks, gate, -jnp.inf)
        out = jnp.full((ht, LANES), -1, jnp.int32)
        for r in range(B_TOPK):
            mx = jnp.max(g, axis=1, keepdims=True)
            idx = jnp.min(jnp.where(g == mx, lane, LANES), axis=1, keepdims=True)
            out = jnp.where(lane == r, jnp.where(mx > -jnp.inf, idx, -1), out)
            g = jnp.where(lane == idx, -jnp.inf, g)
        idx_ref[...] = out


def _moba_select(q, row_block0, pool_kt, page_table, *, npg=32):
    bsz, n_pages = page_table.shape
    _, width, page = pool_kt.shape
    ppb = B_BLOCK // page
    t_new = 8
    ht = B_HEADS * t_new
    page_specs = [
        pl.BlockSpec((None, width, page), lambda b, s, pt, r=r: (pt[b, s * npg + r], 0, 0))
        for r in range(npg)
    ]
    return pl.pallas_call(
        functools.partial(_moba_select_body, npg=npg, ppb=ppb, n_blocks=n_pages // ppb),
        grid_spec=pltpu.PrefetchScalarGridSpec(
            num_scalar_prefetch=1,
            grid=(bsz, n_pages // npg),
            in_specs=[pl.BlockSpec((t_new, width), lambda b, s, pt: (row_block0 + b, 0))] + page_specs,
            out_specs=pl.BlockSpec((ht, LANES), lambda b, s, pt: (b, 0)),
            scratch_shapes=[pltpu.VMEM((width, LANES), F32)],
        ),
        out_shape=jax.ShapeDtypeStruct((bsz * ht, LANES), jnp.int32),
        compiler_params=_cparams(("arbitrary", "arbitrary")),
        name="moba_select",
    )(page_table, q, *([pool_kt] * npg))


def _moba_gather_body(pages_ref, sel_ref, qt_ref, knt_ref, vnt_ref, pool_k, pool_v, o_ref,
                      kbuf, vbuf, sem, *, ppb):
    b = pl.program_id(0)
    h = pl.program_id(1)
    n_heads = pl.num_programs(1)
    t_new = qt_ref.shape[1]
    n_chunk = t_new * B_TOPK * ppb
    step = b * n_heads + h
    slot = step % 2

    def chunk_copies(bb, hh, c, to_slot):
        page = pages_ref[bb, hh * n_chunk + c]
        return (pltpu.make_async_copy(pool_k.at[page, hh], kbuf.at[to_slot, c], sem.at[to_slot, 0]),
                pltpu.make_async_copy(pool_v.at[page, hh], vbuf.at[to_slot, c], sem.at[to_slot, 1]))

    def start_all(bb, hh, to_slot):
        def body(c, carry):
            for cp in chunk_copies(bb, hh, c, to_slot):
                cp.start()
            return carry
        lax.fori_loop(0, n_chunk, body, 0)

    @pl.when(step == 0)
    def _():
        start_all(b, h, slot)

    nxt = step + 1

    @pl.when(nxt < pl.num_programs(0) * n_heads)
    def _():
        start_all(nxt // n_heads, nxt % n_heads, 1 - slot)

    pltpu.make_async_copy(pool_k.at[pl.ds(0, n_chunk), 0], kbuf.at[slot], sem.at[slot, 0]).wait()
    pltpu.make_async_copy(pool_v.at[pl.ds(0, n_chunk), 0], vbuf.at[slot], sem.at[slot, 1]).wait()
    page_rows = kbuf.shape[3]
    kch = [kbuf.at[slot, c] for c in range(n_chunk)]
    vch = [vbuf.at[slot, c] for c in range(n_chunk)]
    qt = qt_ref[...] * QK_SCALE
    knt = knt_ref[...]
    vnt = vnt_ref[...]
    per_t = B_TOPK * ppb
    rows, own_rows = [], []
    for t in range(t_new):
        qcol = qt[:, t:t + 1]
        own_rows.append(jnp.sum(qcol * knt, axis=0, keepdims=True))
        parts = []
        for r in range(B_TOPK):
            picked = sel_ref[b, (h * t_new + t) * B_TOPK + r] >= 0
            for pg in range(ppb):
                kc = kch[(t * B_TOPK + r) * ppb + pg][...]
                parts.append(jnp.where(picked, jnp.sum(qcol * kc, axis=0, keepdims=True), NEG))
        rows.append(jnp.concatenate(parts, axis=1))
    s_all = jnp.concatenate(rows, axis=0)
    s_own = jnp.concatenate(own_rows, axis=0)
    s_own = jnp.where(_iota((t_new, t_new), 1) <= _iota((t_new, t_new), 0), s_own, NEG)
    m = jnp.maximum(jnp.max(s_all, axis=1, keepdims=True), jnp.max(s_own, axis=1, keepdims=True))
    p_all = jnp.exp(s_all - m)
    p_own = jnp.exp(s_own - m)
    l = jnp.sum(p_all, axis=1, keepdims=True) + jnp.sum(p_own, axis=1, keepdims=True)
    lane_o = _iota((HEAD_DIM, t_new), 1)
    out = jnp.zeros((HEAD_DIM, t_new), F32)
    for t in range(t_new):
        accv = jnp.zeros((HEAD_DIM, page_rows), F32)
        for c in range(per_t):
            accv = accv + p_all[t:t + 1, c * page_rows:(c + 1) * page_rows] * vch[t * per_t + c][...]
        acc = (jnp.sum(accv, axis=1, keepdims=True)
               + jnp.sum(p_own[t:t + 1, :] * vnt, axis=1, keepdims=True))
        out = jnp.where(lane_o == t, acc / l[t:t + 1, :], out)
    o_ref[...] = out


def _moba_gather(qt, knt, vnt, pool_kt, pool_vt, page_table, sel):
    bsz, width, t_new = qt.shape
    _, n_heads, hd, page = pool_kt.shape
    ppb = B_BLOCK // page
    n_chunk = t_new * B_TOPK * ppb
    new_spec = pl.BlockSpec((None, hd, t_new), lambda b, h, pt, sl: (b, h, 0))
    hbm = pl.BlockSpec(memory_space=pl.ANY)
    return pl.pallas_call(
        functools.partial(_moba_gather_body, ppb=ppb),
        grid_spec=pltpu.PrefetchScalarGridSpec(
            num_scalar_prefetch=2,
            grid=(bsz, n_heads),
            in_specs=[new_spec, new_spec, new_spec, hbm, hbm],
            out_specs=pl.BlockSpec((None, hd, t_new), lambda b, h, pt, sl: (b, h, 0)),
            scratch_shapes=[
                pltpu.VMEM((2, n_chunk, hd, page), F32),
                pltpu.VMEM((2, n_chunk, hd, page), F32),
                pltpu.SemaphoreType.DMA((2, 2)),
            ],
        ),
        out_shape=jax.ShapeDtypeStruct((bsz, width, t_new), F32),
        compiler_params=_cparams(("arbitrary", "arbitrary")),
        name="moba_gather",
    )(page_table, sel, qt, knt, vnt, pool_kt, pool_vt)


def _silu(x):
    return x / (1.0 + jnp.exp(-x))


def _ffn_even_body(x_ref, a_ref, b_ref, wo_ref, g_ref, wg_ref, wu_ref, wd_ref, o_ref, hn_ref, acc_ref):
    f = pl.program_id(1)
    aw = a_ref.shape[1]

    @pl.when(f == 0)
    def _():
        x1 = (x_ref[...]
              + jnp.dot(a_ref[...], wo_ref[:aw, :], preferred_element_type=F32)
              + jnp.dot(b_ref[...], wo_ref[aw:, :], preferred_element_type=F32))
        acc_ref[...] = x1
        hn_ref[...] = _rms(x1, g_ref[...]).astype(BF16)

    hn = hn_ref[...]
    gate = jnp.dot(hn, wg_ref[...], preferred_element_type=F32)
    up = jnp.dot(hn, wu_ref[...], preferred_element_type=F32)
    act = (_silu(gate) * up).astype(BF16)
    acc_ref[...] += jnp.dot(act, wd_ref[...], preferred_element_type=F32)

    @pl.when(f == pl.num_programs(1) - 1)
    def _():
        o_ref[...] = acc_ref[...]


def _ffn_even(x, a, b, wo16, g, wg16, wu16, wd16, *, tm=640, tf=1408):
    rows, dm = x.shape
    aw = a.shape[1]
    bw = b.shape[1]
    dff = wg16.shape[1]
    return pl.pallas_call(
        _ffn_even_body,
        grid=(rows // tm, dff // tf),
        in_specs=[
            pl.BlockSpec((tm, dm), lambda i, f: (i, 0)),
            pl.BlockSpec((tm, aw), lambda i, f: (i, 0)),
            pl.BlockSpec((tm, bw), lambda i, f: (i, 0)),
            pl.BlockSpec((aw + bw, dm), lambda i, f: (0, 0)),
            pl.BlockSpec((1, dm), lambda i, f: (0, 0)),
            pl.BlockSpec((dm, tf), lambda i, f: (0, f)),
            pl.BlockSpec((dm, tf), lambda i, f: (0, f)),
            pl.BlockSpec((tf, dm), lambda i, f: (f, 0)),
        ],
        out_specs=pl.BlockSpec((tm, dm), lambda i, f: (i, 0)),
        out_shape=jax.ShapeDtypeStruct((rows, dm), F32),
        scratch_shapes=[pltpu.VMEM((tm, dm), BF16), pltpu.VMEM((tm, dm), F32)],
        compiler_params=_cparams(("arbitrary", "arbitrary")),
        name="ffn_even",
    )(x, a, b, wo16, g, wg16, wu16, wd16)


def _odd_in_body(x_ref, g_ref, w_ref, q_ref, k_ref, v_ref, kt_ref, vt_ref, *, n_prompt_tiles):
    cw = q_ref.shape[1]
    h = _rms(x_ref[...], g_ref[...]).astype(BF16)
    y = jnp.dot(h, w_ref[...], preferred_element_type=F32)
    q_ref[...] = y[:, :cw]
    k_ref[...] = y[:, cw:2 * cw]
    v_ref[...] = y[:, 2 * cw:]
    _feature_major_kv(pl.program_id(0), n_prompt_tiles, y[:, cw:2 * cw], y[:, 2 * cw:], kt_ref, vt_ref)


def _odd_in(x, bsz, t_len, g, w16, *, tm=256):
    rows, dm = x.shape
    cw = w16.shape[1] // 3
    npt = bsz * t_len // tm
    out = jax.ShapeDtypeStruct((rows, cw), F32)
    out_t = jax.ShapeDtypeStruct((bsz, cw, t_len), F32)
    spec = pl.BlockSpec((tm, cw), lambda i: (i, 0))
    fm_spec = _feature_major_spec(cw, tm, npt, t_len // tm)
    return pl.pallas_call(
        functools.partial(_odd_in_body, n_prompt_tiles=npt),
        grid=(rows // tm,),
        in_specs=[
            pl.BlockSpec((tm, dm), lambda i: (i, 0)),
            pl.BlockSpec((1, dm), lambda i: (0, 0)),
            pl.BlockSpec((dm, 3 * cw), lambda i: (0, 0)),
        ],
        out_specs=[spec, spec, spec, fm_spec, fm_spec],
        out_shape=[out, out, out, out_t, out_t],
        compiler_params=_cparams(("arbitrary",)),
        name="odd_in",
    )(x, g, w16)


def _dilated_prompt_body(*refs, patterns, tiles):
    q_refs, k_refs, v_refs = (refs[g * tiles:(g + 1) * tiles] for g in range(3))
    o_ref = refs[3 * tiles]
    state = refs[3 * tiles + 1:]
    t_len = q_refs[0].shape[0]
    blk = 128
    lo = _iota((blk, LANES), 1) < HEAD_DIM
    n_br = len(patterns)
    n_iter = t_len // blk
    masks = []
    for window, dil in patterns:
        assert window // dil == blk
        has_prev = t_len // (dil * blk) > 1
        n_keys = 2 * blk if has_prev else blk
        qrow = _iota((2 * blk, n_keys), 0) % blk
        kcol = _iota((2 * blk, n_keys), 1)
        if has_prev:
            masks.append(((kcol >= blk) & ((kcol - blk) <= qrow), (kcol < blk) & (kcol >= qrow)))
        else:
            masks.append((kcol <= qrow, None))

    def body(it, carry):
        for bi, (window, dil) in enumerate(patterns):
            nblk = t_len // (dil * blk)
            cur_ok, prev_ok = masks[bi]
            cls = it // nblk
            ib = it % nblk

            def rows_at(block, cls=cls, dil=dil):
                start = cls + dil * blk * block
                if dil == 1:
                    return pl.ds(pl.multiple_of(start, blk), blk)
                return pl.ds(start, blk, stride=dil)

            rows = rows_at(ib)
            prows = rows_at(jnp.maximum(ib - 1, 0))
            for tile in range(tiles):
                q_ref, k_ref, v_ref = q_refs[tile], k_refs[tile], v_refs[tile]
                lse_s, out_s = state[2 * (tile * n_br + bi):2 * (tile * n_br + bi) + 2]
                q = q_ref[rows, :] * QK_SCALE
                qst = jnp.concatenate([jnp.where(lo, q, 0.0), jnp.where(lo, 0.0, q)], axis=0).astype(BF16)
                kk = k_ref[rows, :].astype(BF16)
                vv = v_ref[rows, :].astype(BF16)
                ok = cur_ok
                if prev_ok is not None:
                    kk = jnp.concatenate([k_ref[prows, :].astype(BF16), kk], axis=0)
                    vv = jnp.concatenate([v_ref[prows, :].astype(BF16), vv], axis=0)
                    ok = cur_ok | (prev_ok & (ib > 0))
                s = jnp.where(ok, lax.dot_general(qst, kk, NT, preferred_element_type=F32), NEG)
                m = jnp.max(s, axis=1, keepdims=True)
                p = jnp.exp(s - m)
                l = jnp.sum(p, axis=1, keepdims=True)
                pv = jnp.dot(p.astype(BF16), vv, preferred_element_type=F32)
                lse = m + jnp.log(l)
                outn = pv / l
                lse_s[rows, :] = jnp.where(lo, lse[:blk], lse[blk:])
                out_s[rows, :] = jnp.where(lo, outn[:blk], outn[blk:])
        return carry

    lax.fori_loop(0, n_iter, body, 0)

    chunk = 128

    def merge(c, carry):
        rows = pl.ds(pl.multiple_of(c * chunk, chunk), chunk)
        for tile in range(tiles):
            st = state[2 * tile * n_br:2 * (tile + 1) * n_br]
            ms = [st[2 * r][rows, :] for r in range(n_br)]
            m = functools.reduce(jnp.maximum, ms)
            num = jnp.zeros((chunk, LANES), F32)
            den = jnp.zeros((chunk, LANES), F32)
            for r in range(n_br):
                w = jnp.exp(ms[r] - m)
                num = num + w * st[2 * r + 1][rows, :]
                den = den + w
            o_ref[rows, tile * LANES:(tile + 1) * LANES] = (num / den).astype(BF16)
        return carry

    lax.fori_loop(0, t_len // chunk, merge, 0)


def _dilated_prompt(q, k, v, bsz, t_len, *, tiles=4):
    width = q.shape[1]
    specs = [pl.BlockSpec((t_len, LANES), lambda b, j, t=t: (b, j * tiles + t)) for t in range(tiles)]
    return pl.pallas_call(
        functools.partial(_dilated_prompt_body, patterns=C_PATTERNS, tiles=tiles),
        grid=(bsz, width // (tiles * LANES)),
        in_specs=specs * 3,
        out_specs=pl.BlockSpec((t_len, tiles * LANES), lambda b, j: (b, j)),
        out_shape=jax.ShapeDtypeStruct((bsz * t_len, width), BF16),
        scratch_shapes=[pltpu.VMEM((t_len, LANES), F32)] * (2 * len(C_PATTERNS) * tiles),
        compiler_params=_cparams(("arbitrary", "arbitrary")),
        name="dilated_prompt",
    )(*([q] * tiles), *([k] * tiles), *([v] * tiles))


def _dilated_sample_body(q_ref, kn_ref, vn_ref, kc_ref, vc_ref, o_ref, *, patterns):
    t_new, cols = q_ref.shape
    w_len = kc_ref.shape[1]
    nrow = 2 * t_new
    lane = _iota((t_new, LANES), 1)

    def mult(delta):
        cnt = jnp.zeros(delta.shape, F32)
        for window, dil in patterns:
            hit = (delta >= 0) & (delta <= window) & ((delta % dil) == 0)
            cnt = cnt + hit.astype(F32)
        return cnt

    t_c = _iota((nrow, w_len), 0) % t_new
    w_c = mult(w_len + t_c - _iota((nrow, w_len), 1))
    t_n = _iota((nrow, LANES), 0) % t_new
    c_n = _iota((nrow, LANES), 1)
    w_n = jnp.where(c_n < t_new, mult(t_n - c_n), 0.0)
    pad = jnp.zeros((LANES - t_new, LANES), F32)
    for tile in range(cols // LANES):
        cs = slice(tile * LANES, (tile + 1) * LANES)
        q = q_ref[:, cs] * QK_SCALE
        qst = jnp.concatenate([jnp.where(lane < HEAD_DIM, q, 0.0), jnp.where(lane >= HEAD_DIM, q, 0.0)], axis=0)
        qst = qst.astype(BF16)
        kn = jnp.concatenate([kn_ref[:, cs], pad], axis=0).astype(BF16)
        vn = jnp.concatenate([vn_ref[:, cs], pad], axis=0).astype(BF16)
        s_c = jnp.dot(qst, kc_ref[cs, :].astype(BF16), preferred_element_type=F32)
        s_n = lax.dot_general(qst, kn, NT, preferred_element_type=F32)
        s_c = jnp.where(w_c > 0, s_c, NEG)
        s_n = jnp.where(w_n > 0, s_n, NEG)
        m = jnp.maximum(jnp.max(s_c, axis=1, keepdims=True), jnp.max(s_n, axis=1, keepdims=True))
        p_c = w_c * jnp.exp(s_c - m)
        p_n = w_n * jnp.exp(s_n - m)
        l = jnp.sum(p_c, axis=1, keepdims=True) + jnp.sum(p_n, axis=1, keepdims=True)
        acc = (lax.dot_general(p_c.astype(BF16), vc_ref[cs, :].astype(BF16), NT, preferred_element_type=F32)
               + jnp.dot(p_n.astype(BF16), vn, preferred_element_type=F32))
        out = acc / l
        o_ref[:, cs] = jnp.where(lane < HEAD_DIM, out[:t_new], out[t_new:])


def _dilated_sample(q, k, v, row_block0, cache_kt, cache_vt, *, tiles=4):
    bsz, width, w_len = cache_kt.shape
    t_new = 8
    cols = tiles * LANES
    new_spec = pl.BlockSpec((t_new, cols), lambda b, j: (row_block0 + b, j))
    cache_spec = pl.BlockSpec((None, cols, w_len), lambda b, j: (b, j, 0))
    return pl.pallas_call(
        functools.partial(_dilated_sample_body, patterns=C_PATTERNS),
        grid=(bsz, width // cols),
        in_specs=[new_spec, new_spec, new_spec, cache_spec, cache_spec],
        out_specs=pl.BlockSpec((t_new, cols), lambda b, j: (b, j)),
        out_shape=jax.ShapeDtypeStruct((bsz * t_new, width), F32),
        compiler_params=_cparams(("arbitrary", "arbitrary")),
        name="dilated_sample",
    )(q, k, v, cache_kt, cache_vt)


def _moe_route_body(x_ref, c_ref, wo_ref, g_ref, r_ref,
                    x3_ref, hn_ref, comb_ref, pos_ref, post_ref, cnt_ref, before_ref):
    tb = x_ref.shape[0]

    @pl.when(pl.program_id(0) == 0)
    def _():
        before_ref[...] = jnp.where(_iota((tb, tb), 1) < _iota((tb, tb), 0), 1.0, 0.0).astype(BF16)

    x3 = x_ref[...] + jnp.dot(c_ref[...], wo_ref[...], preferred_element_type=F32)
    x3_ref[...] = x3
    hf = _rms(x3, g_ref[...])
    hn = hf.astype(BF16)
    hn_ref[...] = hn
    lane = _iota((tb, LANES), 1)
    h_lo = (hf - hn.astype(F32)).astype(BF16)
    r = r_ref[...]
    r_hi = r.astype(BF16)
    r_lo = (r - r_hi.astype(F32)).astype(BF16)
    logits = (jnp.dot(hn, r_hi, preferred_element_type=F32)
              + (jnp.dot(hn, r_lo, preferred_element_type=F32)
                 + jnp.dot(h_lo, r_hi, preferred_element_type=F32)))
    lg = jnp.where(lane < N_EXPERTS, logits, -jnp.inf)
    m1 = jnp.max(lg, axis=1, keepdims=True)
    p1 = lane == jnp.min(jnp.where(lg == m1, lane, LANES), axis=1, keepdims=True)
    lg2 = jnp.where(p1, -jnp.inf, lg)
    m2 = jnp.max(lg2, axis=1, keepdims=True)
    p2 = lane == jnp.min(jnp.where(lg2 == m2, lane, LANES), axis=1, keepdims=True)
    e2 = jnp.exp(m2 - m1)
    den = 1.0 + e2
    comb_ref[...] = jnp.where(p1, 1.0 / den, 0.0) + jnp.where(p2, e2 / den, 0.0)
    routed = p1 | p2
    ind = jnp.where(routed, 1.0, 0.0)
    slot = jnp.where(routed, jnp.dot(before_ref[...], ind.astype(BF16), preferred_element_type=F32), -1.0)
    pos_ref[...] = slot
    eye = jnp.where(_iota((LANES, LANES), 0) == _iota((LANES, LANES), 1), 1.0, 0.0)
    post_ref[...] = lax.dot_general(eye, slot, NT, precision=HIGHEST, preferred_element_type=F32)
    cnt = jnp.sum(ind, axis=0, keepdims=True)
    cnt_ref[...] = jnp.broadcast_to(cnt, cnt_ref.shape).astype(jnp.int32)


def _moe_route(x, c, wo16, g, router_pad, *, tb):
    rows, dm = x.shape
    nblk = rows // tb
    row_spec = lambda width: pl.BlockSpec((tb, width), lambda i: (i, 0))
    return pl.pallas_call(
        _moe_route_body,
        grid=(nblk,),
        in_specs=[
            row_spec(dm), row_spec(dm),
            pl.BlockSpec((dm, dm), lambda i: (0, 0)),
            pl.BlockSpec((1, dm), lambda i: (0, 0)),
            pl.BlockSpec((dm, LANES), lambda i: (0, 0)),
        ],
        out_specs=[
            row_spec(dm), row_spec(dm), row_spec(LANES), row_spec(LANES),
            pl.BlockSpec((None, LANES, tb), lambda i: (i, 0, 0)),
            pl.BlockSpec((None, 8, LANES), lambda i: (i, 0, 0)),
        ],
        out_shape=[
            jax.ShapeDtypeStruct((rows, dm), F32),
            jax.ShapeDtypeStruct((rows, dm), BF16),
            jax.ShapeDtypeStruct((rows, LANES), F32),
            jax.ShapeDtypeStruct((rows, LANES), F32),
            jax.ShapeDtypeStruct((nblk, LANES, tb), F32),
            jax.ShapeDtypeStruct((nblk, 8, LANES), jnp.int32),
        ],
        scratch_shapes=[pltpu.VMEM((tb, tb), BF16)],
        compiler_params=_cparams(("arbitrary",)),
        name="moe_route",
    )(x, c, wo16, g, router_pad)


def _load_expert_weight(src_hbm, e, dst16, stage, sem):
    chunk = stage.shape[1]
    n_chunks = dst16.shape[0] // chunk

    def copy(c, slot):
        return pltpu.make_async_copy(src_hbm.at[e, pl.ds(c * chunk, chunk), :], stage.at[slot], sem.at[slot])

    copy(0, 0).start()
    for c in range(n_chunks):
        slot = c % 2
        if c + 1 < n_chunks:
            copy(c + 1, 1 - slot).start()
        copy(c, slot).wait()
        dst16[c * chunk:(c + 1) * chunk, :] = stage[slot].astype(BF16)


def _moe_expert_body(cnt_ref, acc_ref, hn_ref, comb_ref, pos_ref, post_ref, gf_ref,
                     wg_hbm, wu_hbm, wd_hbm, o_ref, *rest, e, last, tile_rows, tail_rows):
    tail_ref = rest[:1] if last else ()
    wg_ref, wu_ref, wd_ref, stage_in, stage_out, sem_in, sem_out = rest[len(tail_ref):]
    i = pl.program_id(0)
    tb = hn_ref.shape[0]

    @pl.when(i == 0)
    def _():
        _load_expert_weight(wg_hbm, e, wg_ref, stage_in, sem_in)
        _load_expert_weight(wu_hbm, e, wu_ref, stage_in, sem_in)
        _load_expert_weight(wd_hbm, e, wd_ref, stage_out, sem_out)

    n_tok = cnt_ref[i, e]
    lane = _iota((tb, LANES), 1)
    w_col = jnp.sum(jnp.where(lane == e, comb_ref[...], 0.0), axis=1, keepdims=True)
    slot_col = jnp.sum(jnp.where(lane == e, pos_ref[...], 0.0), axis=1, keepdims=True)
    slot_row = post_ref[pl.ds(e, 1), :]
    finish = (lambda v: _rms(v, gf_ref[...])) if last else (lambda v: v)

    def tile_out(rows, base):
        sub_of_row = _iota((rows, tb), 0).astype(F32)
        sub_of_col = _iota((tb, rows), 1).astype(F32)
        gather = jnp.where(slot_row - base == sub_of_row, 1.0, 0.0).astype(BF16)
        xe = jnp.dot(gather, hn_ref[...], preferred_element_type=F32).astype(BF16)
        gate = jnp.dot(xe, wg_ref[...], preferred_element_type=F32)
        up = jnp.dot(xe, wu_ref[...], preferred_element_type=F32)
        act = (_silu(gate) * up).astype(BF16)
        y = jnp.dot(act, wd_ref[...], preferred_element_type=F32).astype(BF16)
        scatter = jnp.where(slot_col - base == sub_of_col, 1.0, 0.0).astype(BF16)
        return w_col * jnp.dot(scatter, y, preferred_element_type=F32)

    @pl.when(n_tok == 0)
    def _():
        o_ref[...] = finish(acc_ref[...])

    lower = 0
    for rows in tile_rows:
        @pl.when((n_tok > lower) & (n_tok <= rows))
        def _(rows=rows):
            o_ref[...] = finish(acc_ref[...] + tile_out(rows, 0.0))
        lower = rows
    big = tile_rows[-1]

    @pl.when(n_tok > big)
    def _():
        o_ref[...] = acc_ref[...]

        def body(s, carry):
            o_ref[...] += tile_out(big, (s * big).astype(F32))
            return carry
        lax.fori_loop(0, (n_tok + big - 1) // big, body, 0)
        o_ref[...] = finish(o_ref[...])

    if last:
        @pl.when(i == pl.num_programs(0) - 1)
        def _():
            tb_rows = o_ref.shape[0]
            tail_ref[0][...] = o_ref[tb_rows - tail_rows:, :]


def _moe_experts(counts, x3, hn, comb, pos, post, g_final, wg, wu, wd, *, tb, tile_rows, tail_rows):
    rows, dm = x3.shape
    n_exp, _, dff = wg.shape
    assert rows % tb == 0 and tail_rows <= tb
    assert dm % WEIGHT_STAGE_ROWS == 0 and dff % WEIGHT_STAGE_ROWS == 0
    hbm = pl.BlockSpec(memory_space=pl.ANY)
    acc = x3
    for e in range(n_exp):
        last = e == n_exp - 1
        row_spec = pl.BlockSpec((tb, dm), lambda i, cnt: (i, 0))
        if last:
            out_specs = [row_spec, pl.BlockSpec((tail_rows, dm), lambda i, cnt: (0, 0))]
            out_shape = [jax.ShapeDtypeStruct((rows - tail_rows, dm), F32),
                         jax.ShapeDtypeStruct((tail_rows, dm), F32)]
        else:
            out_specs, out_shape = row_spec, jax.ShapeDtypeStruct((rows, dm), F32)
        acc = pl.pallas_call(
            functools.partial(_moe_expert_body, e=e, last=last, tile_rows=tile_rows, tail_rows=tail_rows),
            grid_spec=pltpu.PrefetchScalarGridSpec(
                num_scalar_prefetch=1,
                grid=(rows // tb,),
                in_specs=[
                    pl.BlockSpec((tb, dm), lambda i, cnt: (i, 0)),
                    pl.BlockSpec((tb, dm), lambda i, cnt: (i, 0)),
                    pl.BlockSpec((tb, LANES), lambda i, cnt: (i, 0)),
                    pl.BlockSpec((tb, LANES), lambda i, cnt: (i, 0)),
                    pl.BlockSpec((None, LANES, tb), lambda i, cnt: (i, 0, 0)),
                    pl.BlockSpec((1, dm), lambda i, cnt: (0, 0)),
                    hbm, hbm, hbm,
                ],
                out_specs=out_specs,
                scratch_shapes=[
                    pltpu.VMEM((dm, dff), BF16),
                    pltpu.VMEM((dm, dff), BF16),
                    pltpu.VMEM((dff, dm), BF16),
                    pltpu.VMEM((2, WEIGHT_STAGE_ROWS, dff), F32),
                    pltpu.VMEM((2, WEIGHT_STAGE_ROWS, dm), F32),
                    pltpu.SemaphoreType.DMA((2,)),
                    pltpu.SemaphoreType.DMA((2,)),
                ],
            ),
            out_shape=out_shape,
            compiler_params=_cparams(("arbitrary",)),
            name=f"moe_expert{e}",
        )(counts, acc, hn, comb, pos, post, g_final, wg, wu, wd)
    return acc


def _moe(x, c, wo16, g, router_pad, g_final, wg, wu, wd, tail_rows, *, tb=416,
         tile_rows=(128, 160, 192, 256)):
    x3, hn, comb, pos, post, cnt = _moe_route(x, c, wo16, g, router_pad, tb=tb)
    counts = cnt[:, 0, :N_EXPERTS]
    return _moe_experts(counts, x3, hn, comb, pos, post, g_final, wg, wu, wd, tb=tb,
                        tile_rows=tile_rows, tail_rows=tail_rows)


def kernel(x_prompt, x_sample, cache_b_k, cache_b_v, cache_c_k, cache_c_v, page_table, norm_mix, norm_ffn,
           norm_final, w_in_even, w_out_even, a_ln_g, a_ln_b, a_ws, a_bs, ffn_gate, ffn_up, ffn_down,
           w_in_odd, w_out_odd, router, moe_gate, moe_up, moe_down):
    bsz, t_len, dm = x_prompt.shape
    dbsz, t_new, _ = x_sample.shape
    assert norm_mix.shape[0] == 2 and w_in_even.shape[0] == 1 and w_in_odd.shape[0] == 1
    assert cache_b_k.shape[0] == 1 and cache_c_k.shape[0] == 1
    assert t_new == 8 and A_CHUNK % t_new == 0
    n_p = bsz * t_len
    n_s = dbsz * t_new
    aw = a_ln_g.shape[-1]
    n_groups = aw // HEAD_DIM
    seq_per_chunk = A_CHUNK // t_new


    ws = a_ws[0]
    ws_sample = jnp.tile(ws[:, :t_new, :t_new], (1, seq_per_chunk, seq_per_chunk))
    ws2 = jnp.stack([ws, ws_sample])
    bias_p = jnp.repeat(a_bs[0].T, HEAD_DIM, axis=1)
    bias_s = jnp.repeat(jnp.tile(a_bs[0][:, :t_new].T, (seq_per_chunk, 1)), HEAD_DIM, axis=1)
    bias2 = jnp.stack([bias_p, bias_s])
    row = lambda vec: vec.reshape(1, -1)
    router_pad = jnp.pad(router[0], ((0, 0), (0, LANES - router.shape[-1])))

    w_even = w_in_even[0].astype(BF16)
    a_out, v_a, q_b, k_b, v_b, kt_b, vt_b, v_a_last, x = _even_in(
        x_prompt.reshape(n_p, dm), x_sample.reshape(n_s, dm), bsz, t_len, row(norm_mix[0]), w_even, row(a_ln_g[0]), row(a_ln_b[0]), ws2, bias2)
    b_prompt = _moba_prompt(q_b, k_b, v_b, bsz, t_len)
    page = cache_b_k.shape[2]
    bw = q_b.shape[1]
    feature_major = lambda c: jnp.transpose(c, (0, 1, 3, 4, 2))
    pool_kt = feature_major(cache_b_k).reshape(-1, B_HEADS, HEAD_DIM, page)
    pool_vt = feature_major(cache_b_v).reshape(-1, B_HEADS, HEAD_DIM, page)
    sel = _moba_select(q_b, n_p // t_new, pool_kt.reshape(-1, bw, page), page_table)
    sel = sel.reshape(dbsz, B_HEADS * t_new, LANES)[:, :, :B_TOPK].reshape(dbsz, -1)
    new_t = lambda a: jnp.transpose(a[n_p:].reshape(dbsz, t_new, -1), (0, 2, 1))
    ppb = B_BLOCK // page
    logical = (jnp.maximum(sel, 0)[:, :, None] * ppb + jnp.arange(ppb, dtype=jnp.int32)).reshape(dbsz, -1)
    hit = logical[:, :, None] == jnp.arange(page_table.shape[1], dtype=jnp.int32)
    sel_pages = jnp.sum(jnp.where(hit, page_table[:, None, :], 0), axis=-1)
    b_sample_t = _moba_gather(new_t(q_b), new_t(k_b), new_t(v_b), pool_kt, pool_vt, sel_pages, sel)
    b_sample = jnp.transpose(b_sample_t, (0, 2, 1)).reshape(n_s, bw)
    b_out = jnp.concatenate([b_prompt, b_sample.astype(BF16)], axis=0)
    x2 = _ffn_even(x, a_out, b_out, w_out_even[0].astype(BF16), row(norm_ffn[0]),
                   ffn_gate[0].astype(BF16), ffn_up[0].astype(BF16), ffn_down[0].astype(BF16))

    w_odd = w_in_odd[0].astype(BF16)
    q_c, k_c, v_c, kt_c, vt_c = _odd_in(x2, bsz, t_len, row(norm_mix[1]), w_odd)
    c_prompt = _dilated_prompt(q_c, k_c, v_c, bsz, t_len)
    cw = q_c.shape[1]
    w_len = cache_c_k.shape[2]
    c_sample = _dilated_sample(q_c, k_c, v_c, n_p // t_new,
                               feature_major(cache_c_k).reshape(-1, cw, w_len),
                               feature_major(cache_c_v).reshape(-1, cw, w_len))
    c_out = jnp.concatenate([c_prompt, c_sample.astype(BF16)], axis=0)
    y_p, y_s = _moe(x2, c_out, w_out_odd[0].astype(BF16), row(norm_ffn[1]), router_pad, row(norm_final),
                    moe_gate[0], moe_up[0], moe_down[0], n_s)

    bh = (B_HEADS, HEAD_DIM)
    ch = (cw // HEAD_DIM, HEAD_DIM)
    assert t_len % 256 == 0
    c_keep_p = min(C_PATTERNS[-1][0], t_len)
    leaf = lambda a_t, hd: jnp.transpose(a_t.reshape(1, bsz, *hd, a_t.shape[-1]), (0, 1, 4, 2, 3))
    return (
        y_p.reshape(bsz, t_len, dm),
        y_s.reshape(dbsz, t_new, dm),
        v_a_last.reshape(1, bsz, A_CHUNK, aw),
        v_a[n_p:].reshape(1, dbsz, t_new, aw),
        leaf(kt_b, bh),
        leaf(vt_b, bh),
        k_b[n_p:].reshape(1, dbsz, t_new, *bh),
        v_b[n_p:].reshape(1, dbsz, t_new, *bh),
        leaf(kt_c[:, :, t_len - c_keep_p:], ch),
        leaf(vt_c[:, :, t_len - c_keep_p:], ch),
        k_c[n_p:].reshape(1, dbsz, t_new, *ch),
        v_c[n_p:].reshape(1, dbsz, t_new, *ch),
    )
```

```python
import functools

import jax
import jax.numpy as jnp
from jax import lax
from jax.experimental import pallas as pl
from jax.experimental.pallas import tpu as pltpu

F32 = jnp.float32
BF16 = jnp.bfloat16
HIGHEST = lax.Precision.HIGHEST

HEAD_DIM = 64
LANES = 128
A_CHUNK = 128
B_HEADS = 8
B_BLOCK = 256
B_TOPK = 3
C_PATTERNS = ((128, 1), (512, 4), (2048, 16))
N_EXPERTS = 8
RMS_EPS = 1e-6
LN_EPS = 1e-5
NEG = -1e30
QK_SCALE = HEAD_DIM ** -0.5
VMEM_LIMIT = 56 * 1024 * 1024
WEIGHT_STAGE_ROWS = 256

NT = (((1,), (1,)), ((), ()))


def _cparams(sem):
    return pltpu.CompilerParams(dimension_semantics=sem, vmem_limit_bytes=VMEM_LIMIT)


def _rms(x, g):
    return x * lax.rsqrt(jnp.mean(x * x, axis=-1, keepdims=True) + RMS_EPS) * g


def _gelu(x):
    return 0.5 * x * (1.0 + lax.erf(x * 0.7071067811865476))


def _iota(shape, dim):
    return lax.broadcasted_iota(jnp.int32, shape, dim)


def _feature_major_kv(i, n_prompt_tiles, k_tile, v_tile, kt_ref, vt_ref):
    @pl.when(i < n_prompt_tiles)
    def _():
        kt_ref[...] = k_tile.T
        vt_ref[...] = v_tile.T


def _feature_major_spec(width, tm, n_prompt_tiles, tiles_per_seq):
    def imap(i):
        ic = jnp.minimum(i, n_prompt_tiles - 1)
        return (ic // tiles_per_seq, 0, ic % tiles_per_seq)
    return pl.BlockSpec((None, width, tm), imap)


def _even_in_body(xp_ref, xs_ref, g_ref, w_ref, lng_ref, lnb_ref, ws_ref, bias_ref,
                  a_ref, v_ref, q_ref, k_ref, vv_ref, kt_ref, vt_ref, vlast_ref, xcat_ref,
                  *, n_prompt_tiles, tiles_per_seq, aw, bw):
    i = pl.program_id(0)
    tm = xp_ref.shape[0]
    x = jnp.where(i < n_prompt_tiles, xp_ref[...], xs_ref[...])
    xcat_ref[...] = x
    h = _rms(x, g_ref[...]).astype(BF16)
    y = jnp.dot(h, w_ref[...], preferred_element_type=F32)
    _feature_major_kv(i, n_prompt_tiles, y[:, 2 * aw + bw:2 * aw + 2 * bw], y[:, 2 * aw + 2 * bw:], kt_ref, vt_ref)
    u = _gelu(y[:, :aw])
    v = _gelu(y[:, aw:2 * aw])
    mu = jnp.mean(v, axis=-1, keepdims=True)
    vc = v - mu
    var = jnp.mean(vc * vc, axis=-1, keepdims=True)
    v = vc * lax.rsqrt(var + LN_EPS) * lng_ref[...] + lnb_ref[...]
    v_ref[...] = v

    @pl.when((i < n_prompt_tiles) & (i % tiles_per_seq == tiles_per_seq - 1))
    def _():
        vlast_ref[...] = v[tm - A_CHUNK:, :]
    q_ref[...] = y[:, 2 * aw:2 * aw + bw]
    k_ref[...] = y[:, 2 * aw + bw:2 * aw + 2 * bw]
    vv_ref[...] = y[:, 2 * aw + 2 * bw:]

    row = _iota((A_CHUNK, A_CHUNK), 0)
    col = _iota((A_CHUNK, A_CHUNK), 1)
    is_prompt = i < n_prompt_tiles
    ok = (col <= row) & (((row // 8) == (col // 8)) | is_prompt)
    lo = _iota((A_CHUNK, LANES), 1) < HEAD_DIM
    v16 = v.astype(BF16)
    n_groups = aw // HEAD_DIM
    wm = [jnp.where(ok, ws_ref[g], 0.0).astype(BF16) for g in range(n_groups)]
    for c in range(tm // A_CHUNK):
        rs = slice(c * A_CHUNK, (c + 1) * A_CHUNK)
        for j in range(aw // LANES):
            cs = slice(j * LANES, (j + 1) * LANES)
            vj = v16[rs, cs]
            ma = jnp.dot(wm[2 * j], vj, preferred_element_type=F32)
            mb = jnp.dot(wm[2 * j + 1], vj, preferred_element_type=F32)
            mixed = jnp.where(lo, ma, mb) + bias_ref[:, cs]
            a_ref[rs, cs] = (u[rs, cs] * mixed).astype(BF16)


def _even_in(xp, xs, bsz, t_len, g, w16, lng, lnb, ws2, bias2, *, tm=256):
    dm = xp.shape[1]
    assert xs.shape[0] == tm
    rows = xp.shape[0] + tm
    aw = lng.shape[-1]
    n_in = w16.shape[1]
    bw = (n_in - 2 * aw) // 3
    n_groups = aw // HEAD_DIM
    npt = bsz * t_len // tm
    sel = lambda i: jnp.minimum(i // npt, 1)
    row_spec = lambda width: pl.BlockSpec((tm, width), lambda i: (i, 0))
    tps = t_len // tm
    fm_spec = _feature_major_spec(bw, tm, npt, tps)
    return pl.pallas_call(
        functools.partial(_even_in_body, n_prompt_tiles=npt, tiles_per_seq=tps, aw=aw, bw=bw),
        grid=(rows // tm,),
        in_specs=[
            pl.BlockSpec((tm, dm), lambda i: (jnp.minimum(i, npt - 1), 0)),
            pl.BlockSpec((tm, dm), lambda i: (0, 0)),
            pl.BlockSpec((1, dm), lambda i: (0, 0)),
            pl.BlockSpec((dm, n_in), lambda i: (0, 0)),
            pl.BlockSpec((1, aw), lambda i: (0, 0)),
            pl.BlockSpec((1, aw), lambda i: (0, 0)),
            pl.BlockSpec((None, n_groups, A_CHUNK, A_CHUNK), lambda i: (sel(i), 0, 0, 0)),
            pl.BlockSpec((None, A_CHUNK, aw), lambda i: (sel(i), 0, 0)),
        ],
        out_specs=[row_spec(aw), row_spec(aw), row_spec(bw), row_spec(bw), row_spec(bw), fm_spec, fm_spec,
                   pl.BlockSpec((None, A_CHUNK, aw), lambda i: (jnp.minimum(i, npt - 1) // tps, 0, 0)),
                   row_spec(dm)],
        out_shape=[
            jax.ShapeDtypeStruct((rows, aw), BF16),
            jax.ShapeDtypeStruct((rows, aw), F32),
            jax.ShapeDtypeStruct((rows, bw), F32),
            jax.ShapeDtypeStruct((rows, bw), F32),
            jax.ShapeDtypeStruct((rows, bw), F32),
            jax.ShapeDtypeStruct((bsz, bw, t_len), F32),
            jax.ShapeDtypeStruct((bsz, bw, t_len), F32),
            jax.ShapeDtypeStruct((bsz, A_CHUNK, aw), F32),
            jax.ShapeDtypeStruct((rows, dm), F32),
        ],
        compiler_params=_cparams(("arbitrary",)),
        name="even_in",
    )(xp, xs, g, w16, lng, lnb, ws2, bias2)


def _moba_prompt_body(q_ref, k_ref, v_ref, o_ref, k16, v16, mt_ref, s_scr, p_scr, *, nb):
    i = pl.program_id(1)
    t_len, width = k_ref.shape
    n_gate = B_HEADS * nb

    @pl.when(i == 0)
    def _():
        k = k_ref[...]
        k16[...] = k.astype(BF16)
        v16[...] = v_ref[...].astype(BF16)
        r = _iota((LANES, t_len), 0)
        key = _iota((LANES, t_len), 1)
        avg = jnp.where(((key // B_BLOCK) == (r % nb)) & (r < n_gate), 1.0 / B_BLOCK, 0.0).astype(F32)
        mt = jnp.dot(avg, k, precision=HIGHEST, preferred_element_type=F32)
        rr = _iota((LANES, width), 0)
        cc = _iota((LANES, width), 1)
        mt_ref[...] = jnp.where(((rr // nb) == (cc // HEAD_DIM)) & (rr < n_gate), mt, 0.0)

    q = q_ref[...]
    tq = q.shape[0]
    mt = mt_ref[...]
    q_hi = q.astype(BF16)
    q_lo = (q - q_hi.astype(F32)).astype(BF16)
    mt_hi = mt.astype(BF16)
    mt_lo = (mt - mt_hi.astype(F32)).astype(BF16)
    gate = (lax.dot_general(q_hi, mt_hi, NT, preferred_element_type=F32)
            + (lax.dot_general(q_hi, mt_lo, NT, preferred_element_type=F32)
               + lax.dot_general(q_lo, mt_hi, NT, preferred_element_type=F32)))
    lane = _iota((tq, LANES), 1)
    n_of = lane % nb
    valid = (n_of < i) & (lane < n_gate)
    gm = jnp.where(valid, gate, -jnp.inf)
    rank = jnp.zeros((tq, LANES), jnp.int32)
    for s in range(1, nb):
        fwd = pltpu.roll(gm, LANES - s, 1)
        bwd = pltpu.roll(gm, nb - s, 1)
        wrap = (n_of + s) >= nb
        other = jnp.where(wrap, bwd, fwd)
        m_idx = jnp.where(wrap, n_of + s - nb, n_of + s)
        beats = (other > gm) | ((other == gm) & (m_idx < n_of))
        rank = rank + beats.astype(jnp.int32)
    selbias = jnp.where(valid & (rank < B_TOPK), 0.0, NEG)

    row2 = _iota((2 * tq, B_BLOCK), 0) % tq
    col2 = _iota((2 * tq, B_BLOCK), 1)
    causal_bias = jnp.where(col2 <= row2, 0.0, NEG)
    lo = lane < HEAD_DIM
    own = pl.multiple_of(i * B_BLOCK, B_BLOCK)
    half = B_BLOCK // 2

    key_lane = _iota((B_BLOCK, LANES), 1)

    group = s_scr.shape[0]

    def zero_body(n, carry):
        for g in range(group):
            p_scr[g, n] = jnp.zeros((2 * tq, B_BLOCK), BF16)
        return carry

    lax.fori_loop(i + 1, nb, zero_body, 0)
    for j0 in range(0, width // LANES, group):
        cols = [slice((j0 + g) * LANES, (j0 + g + 1) * LANES) for g in range(group)]
        q_aug, m_tiles = [], []
        for g in range(group):
            j = j0 + g
            qj = q[:, cols[g]] * QK_SCALE
            qst = jnp.concatenate([jnp.where(lo, qj, 0.0), jnp.where(lo, 0.0, qj)], axis=0).astype(BF16)
            s = (lax.dot_general(qst, k16[pl.ds(own, B_BLOCK), cols[g]], NT, preferred_element_type=F32)
                 + causal_bias)
            s_scr[g, i] = s
            m_tiles.append(jnp.maximum(s[:, :half], s[:, half:]))
            sb = []
            for a in range(2):
                off = (2 * j + a) * nb
                moved = selbias if off == 0 else pltpu.roll(selbias, LANES - off, 1)
                sb.append(jnp.where(lane < nb, moved, 0.0))
            q_aug.append(jnp.concatenate([qst, jnp.concatenate(sb, axis=0).astype(BF16)], axis=1))

        def score_body(n, m_tiles, q_aug=q_aug, cols=cols):
            start = pl.multiple_of(n * B_BLOCK, B_BLOCK)
            onehot = jnp.where(key_lane == n, 1.0, 0.0).astype(BF16)
            out = []
            for g in range(group):
                k_aug = jnp.concatenate([k16[pl.ds(start, B_BLOCK), cols[g]], onehot], axis=1)
                s = lax.dot_general(q_aug[g], k_aug, NT, preferred_element_type=F32)
                s_scr[g, n] = s
                out.append(jnp.maximum(m_tiles[g], jnp.maximum(s[:, :half], s[:, half:])))
            return tuple(out)

        m_tiles = lax.fori_loop(0, i, score_body, tuple(m_tiles))
        ms = [jnp.max(m_tiles[g], axis=1, keepdims=True) for g in range(group)]

        def prob_body(n, l_tiles, ms=ms):
            out = []
            for g in range(group):
                p = jnp.exp(s_scr[g, n] - ms[g])
                p_scr[g, n] = p.astype(BF16)
                out.append(l_tiles[g] + p[:, :half] + p[:, half:])
            return tuple(out)

        l_tiles = lax.fori_loop(0, i + 1, prob_body,
                                tuple(jnp.zeros((2 * tq, half), F32) for _ in range(group)))
        for g in range(group):
            l = jnp.sum(l_tiles[g], axis=1, keepdims=True)
            acc = jnp.dot(p_scr[g, 0], v16[0:B_BLOCK, cols[g]], preferred_element_type=F32)
            for n in range(1, nb):
                acc = acc + jnp.dot(p_scr[g, n], v16[n * B_BLOCK:(n + 1) * B_BLOCK, cols[g]],
                                    preferred_element_type=F32)
            o = acc / l
            o_ref[:, cols[g]] = jnp.where(lo, o[:tq], o[tq:]).astype(BF16)


def _moba_prompt(q, k, v, bsz, t_len):
    width = q.shape[1]
    nb = t_len // B_BLOCK
    return pl.pallas_call(
        functools.partial(_moba_prompt_body, nb=nb),
        grid=(bsz, nb),
        in_specs=[
            pl.BlockSpec((B_BLOCK, width), lambda b, i: (b * nb + i, 0)),
            pl.BlockSpec((t_len, width), lambda b, i: (b, 0)),
            pl.BlockSpec((t_len, width), lambda b, i: (b, 0)),
        ],
        out_specs=pl.BlockSpec((B_BLOCK, width), lambda b, i: (b * nb + i, 0)),
        out_shape=jax.ShapeDtypeStruct((bsz * t_len, width), BF16),
        scratch_shapes=[
            pltpu.VMEM((t_len, width), BF16),
            pltpu.VMEM((t_len, width), BF16),
            pltpu.VMEM((LANES, width), F32),
            pltpu.VMEM((2, nb, 2 * B_BLOCK, B_BLOCK), F32),
            pltpu.VMEM((2, nb, 2 * B_BLOCK, B_BLOCK), BF16),
        ],
        compiler_params=_cparams(("arbitrary", "arbitrary")),
        name="moba_prompt",
    )(q, k, v)


def _moba_select_body(pt_ref, q_ref, *refs, npg, ppb, n_blocks):
    pages = refs[:npg]
    idx_ref = refs[npg]
    mt_ref = refs[npg + 1]
    s = pl.program_id(1)
    t_new, width = q_ref.shape
    bps = npg // ppb

    @pl.when(s == 0)
    def _():
        mt_ref[...] = jnp.zeros(mt_ref.shape, F32)

    lane_w = _iota((width, LANES), 1)
    cur = mt_ref[...]
    for blk in range(bps):
        tot = pages[blk * ppb][...]
        for r in range(1, ppb):
            tot = tot + pages[blk * ppb + r][...]
        col = jnp.sum(tot, axis=1, keepdims=True) * (1.0 / B_BLOCK)
        cur = jnp.where(lane_w == s * bps + blk, col, cur)
    mt_ref[...] = cur

    @pl.when(s == pl.num_programs(1) - 1)
    def _():
        ht = B_HEADS * t_new
        rr = _iota((ht, width), 0)
        cc = _iota((ht, width), 1)
        q = q_ref[...]
        qst = jnp.where((rr // t_new) == (cc // HEAD_DIM), jnp.concatenate([q] * B_HEADS, axis=0), 0.0)
        gate = jnp.dot(qst, cur, precision=HIGHEST, preferred_element_type=F32)
        lane = _iota((ht, LANES), 1)
        g = jnp.where(lane < n_blocks, gate, -jnp.inf)
        out = jnp.full((ht, LANES), -1, jnp.int32)
        for r in range(B_TOPK):
            mx = jnp.max(g, axis=1, keepdims=True)
            idx = jnp.min(jnp.where(g == mx, lane, LANES), axis=1, keepdims=True)
            out = jnp.where(lane == r, jnp.where(mx > -jnp.inf, idx, -1), out)
            g = jnp.where(lane == idx, -jnp.inf, g)
        idx_ref[...] = out


def _moba_select(q, row_block0, pool_kt, page_table, *, npg=64):
    bsz, n_pages = page_table.shape
    _, width, page = pool_kt.shape
    ppb = B_BLOCK // page
    t_new = 8
    ht = B_HEADS * t_new
    page_specs = [
        pl.BlockSpec((None, width, page), lambda b, s, pt, r=r: (pt[b, s * npg + r], 0, 0))
        for r in range(npg)
    ]
    return pl.pallas_call(
        functools.partial(_moba_select_body, npg=npg, ppb=ppb, n_blocks=n_pages // ppb),
        grid_spec=pltpu.PrefetchScalarGridSpec(
            num_scalar_prefetch=1,
            grid=(bsz, n_pages // npg),
            in_specs=[pl.BlockSpec((t_new, width), lambda b, s, pt: (row_block0 + b, 0))] + page_specs,
            out_specs=pl.BlockSpec((ht, LANES), lambda b, s, pt: (b, 0)),
            scratch_shapes=[pltpu.VMEM((width, LANES), F32)],
        ),
        out_shape=jax.ShapeDtypeStruct((bsz * ht, LANES), jnp.int32),
        compiler_params=_cparams(("arbitrary", "arbitrary")),
        name="moba_select",
    )(page_table, q, *([pool_kt] * npg))


def _moba_gather_body(pages_ref, sel_ref, qt_ref, knt_ref, vnt_ref, pool_k, pool_v, o_ref,
                      kbuf, vbuf, sem, *, ppb):
    b = pl.program_id(0)
    h = pl.program_id(1)
    n_heads = pl.num_programs(1)
    t_new = qt_ref.shape[1]
    n_chunk = t_new * B_TOPK * ppb
    step = b * n_heads + h
    slot = step % 2

    def chunk_copies(bb, hh, c, to_slot):
        page = pages_ref[bb, hh * n_chunk + c]
        return (pltpu.make_async_copy(pool_k.at[page, hh], kbuf.at[to_slot, c], sem.at[to_slot, 0]),
                pltpu.make_async_copy(pool_v.at[page, hh], vbuf.at[to_slot, c], sem.at[to_slot, 1]))

    def start_all(bb, hh, to_slot):
        def body(c, carry):
            for cp in chunk_copies(bb, hh, c, to_slot):
                cp.start()
            return carry
        lax.fori_loop(0, n_chunk, body, 0)

    @pl.when(step == 0)
    def _():
        start_all(b, h, slot)

    nxt = step + 1

    @pl.when(nxt < pl.num_programs(0) * n_heads)
    def _():
        start_all(nxt // n_heads, nxt % n_heads, 1 - slot)

    pltpu.make_async_copy(pool_k.at[pl.ds(0, n_chunk), 0], kbuf.at[slot], sem.at[slot, 0]).wait()
    pltpu.make_async_copy(pool_v.at[pl.ds(0, n_chunk), 0], vbuf.at[slot], sem.at[slot, 1]).wait()
    page_rows = kbuf.shape[3]
    kch = [kbuf.at[slot, c] for c in range(n_chunk)]
    vch = [vbuf.at[slot, c] for c in range(n_chunk)]
    qt = qt_ref[...] * QK_SCALE
    knt = knt_ref[...]
    vnt = vnt_ref[...]
    per_t = B_TOPK * ppb
    rows, own_rows = [], []
    for t in range(t_new):
        qcol = qt[:, t:t + 1]
        own_rows.append(jnp.sum(qcol * knt, axis=0, keepdims=True))
        parts = []
        for r in range(B_TOPK):
            picked = sel_ref[b, (h * t_new + t) * B_TOPK + r] >= 0
            for pg in range(ppb):
                kc = kch[(t * B_TOPK + r) * ppb + pg][...]
                parts.append(jnp.where(picked, jnp.sum(qcol * kc, axis=0, keepdims=True), NEG))
        rows.append(jnp.concatenate(parts, axis=1))
    s_all = jnp.concatenate(rows, axis=0)
    s_own = jnp.concatenate(own_rows, axis=0)
    s_own = jnp.where(_iota((t_new, t_new), 1) <= _iota((t_new, t_new), 0), s_own, NEG)
    m = jnp.maximum(jnp.max(s_all, axis=1, keepdims=True), jnp.max(s_own, axis=1, keepdims=True))
    p_all = jnp.exp(s_all - m)
    p_own = jnp.exp(s_own - m)
    l = jnp.sum(p_all, axis=1, keepdims=True) + jnp.sum(p_own, axis=1, keepdims=True)
    lane_o = _iota((HEAD_DIM, t_new), 1)
    out = jnp.zeros((HEAD_DIM, t_new), F32)
    for t in range(t_new):
        accv = jnp.zeros((HEAD_DIM, page_rows), F32)
        for c in range(per_t):
            accv = accv + p_all[t:t + 1, c * page_rows:(c + 1) * page_rows] * vch[t * per_t + c][...]
        acc = (jnp.sum(accv, axis=1, keepdims=True)
               + jnp.sum(p_own[t:t + 1, :] * vnt, axis=1, keepdims=True))
        out = jnp.where(lane_o == t, acc / l[t:t + 1, :], out)
    o_ref[...] = out


def _moba_gather(qt, knt, vnt, pool_kt, pool_vt, page_table, sel):
    bsz, width, t_new = qt.shape
    _, n_heads, hd, page = pool_kt.shape
    ppb = B_BLOCK // page
    n_chunk = t_new * B_TOPK * ppb
    new_spec = pl.BlockSpec((None, hd, t_new), lambda b, h, pt, sl: (b, h, 0))
    hbm = pl.BlockSpec(memory_space=pl.ANY)
    return pl.pallas_call(
        functools.partial(_moba_gather_body, ppb=ppb),
        grid_spec=pltpu.PrefetchScalarGridSpec(
            num_scalar_prefetch=2,
            grid=(bsz, n_heads),
            in_specs=[new_spec, new_spec, new_spec, hbm, hbm],
            out_specs=pl.BlockSpec((None, hd, t_new), lambda b, h, pt, sl: (b, h, 0)),
            scratch_shapes=[
                pltpu.VMEM((2, n_chunk, hd, page), F32),
                pltpu.VMEM((2, n_chunk, hd, page), F32),
                pltpu.SemaphoreType.DMA((2, 2)),
            ],
        ),
        out_shape=jax.ShapeDtypeStruct((bsz, width, t_new), F32),
        compiler_params=_cparams(("arbitrary", "arbitrary")),
        name="moba_gather",
    )(page_table, sel, qt, knt, vnt, pool_kt, pool_vt)


def _silu(x):
    return x / (1.0 + jnp.exp(-x))


def _ffn_even_body(x_ref, a_ref, b_ref, wo_ref, g_ref, wg_ref, wu_ref, wd_ref, o_ref, hn_ref, acc_ref):
    f = pl.program_id(1)
    aw = a_ref.shape[1]

    @pl.when(f == 0)
    def _():
        x1 = (x_ref[...]
              + jnp.dot(a_ref[...], wo_ref[:aw, :], preferred_element_type=F32)
              + jnp.dot(b_ref[...], wo_ref[aw:, :], preferred_element_type=F32))
        acc_ref[...] = x1
        hn_ref[...] = _rms(x1, g_ref[...]).astype(BF16)

    hn = hn_ref[...]
    gate = jnp.dot(hn, wg_ref[...], preferred_element_type=F32)
    up = jnp.dot(hn, wu_ref[...], preferred_element_type=F32)
    act = (_silu(gate) * up).astype(BF16)
    acc_ref[...] += jnp.dot(act, wd_ref[...], preferred_element_type=F32)

    @pl.when(f == pl.num_programs(1) - 1)
    def _():
        o_ref[...] = acc_ref[...]


def _ffn_even(x, a, b, wo16, g, wg16, wu16, wd16, *, tm=640, tf=1408):
    rows, dm = x.shape
    aw = a.shape[1]
    bw = b.shape[1]
    dff = wg16.shape[1]
    return pl.pallas_call(
        _ffn_even_body,
        grid=(rows // tm, dff // tf),
        in_specs=[
            pl.BlockSpec((tm, dm), lambda i, f: (i, 0)),
            pl.BlockSpec((tm, aw), lambda i, f: (i, 0)),
            pl.BlockSpec((tm, bw), lambda i, f: (i, 0)),
            pl.BlockSpec((aw + bw, dm), lambda i, f: (0, 0)),
            pl.BlockSpec((1, dm), lambda i, f: (0, 0)),
            pl.BlockSpec((dm, tf), lambda i, f: (0, f)),
            pl.BlockSpec((dm, tf), lambda i, f: (0, f)),
            pl.BlockSpec((tf, dm), lambda i, f: (f, 0)),
        ],
        out_specs=pl.BlockSpec((tm, dm), lambda i, f: (i, 0)),
        out_shape=jax.ShapeDtypeStruct((rows, dm), F32),
        scratch_shapes=[pltpu.VMEM((tm, dm), BF16), pltpu.VMEM((tm, dm), F32)],
        compiler_params=_cparams(("arbitrary", "arbitrary")),
        name="ffn_even",
    )(x, a, b, wo16, g, wg16, wu16, wd16)


def _odd_in_body(x_ref, g_ref, w_ref, q_ref, k_ref, v_ref, kt_ref, vt_ref, *, n_prompt_tiles):
    cw = q_ref.shape[1]
    h = _rms(x_ref[...], g_ref[...]).astype(BF16)
    y = jnp.dot(h, w_ref[...], preferred_element_type=F32)
    q_ref[...] = y[:, :cw]
    k_ref[...] = y[:, cw:2 * cw]
    v_ref[...] = y[:, 2 * cw:]
    _feature_major_kv(pl.program_id(0), n_prompt_tiles, y[:, cw:2 * cw], y[:, 2 * cw:], kt_ref, vt_ref)


def _odd_in(x, bsz, t_len, g, w16, *, tm=256):
    rows, dm = x.shape
    cw = w16.shape[1] // 3
    npt = bsz * t_len // tm
    out = jax.ShapeDtypeStruct((rows, cw), F32)
    out_t = jax.ShapeDtypeStruct((bsz, cw, t_len), F32)
    spec = pl.BlockSpec((tm, cw), lambda i: (i, 0))
    fm_spec = _feature_major_spec(cw, tm, npt, t_len // tm)
    return pl.pallas_call(
        functools.partial(_odd_in_body, n_prompt_tiles=npt),
        grid=(rows // tm,),
        in_specs=[
            pl.BlockSpec((tm, dm), lambda i: (i, 0)),
            pl.BlockSpec((1, dm), lambda i: (0, 0)),
            pl.BlockSpec((dm, 3 * cw), lambda i: (0, 0)),
        ],
        out_specs=[spec, spec, spec, fm_spec, fm_spec],
        out_shape=[out, out, out, out_t, out_t],
        compiler_params=_cparams(("arbitrary",)),
        name="odd_in",
    )(x, g, w16)


def _dilated_prompt_body(*refs, patterns, tiles):
    q_refs, k_refs, v_refs = (refs[g * tiles:(g + 1) * tiles] for g in range(3))
    o_ref = refs[3 * tiles]
    state = refs[3 * tiles + 1:]
    t_len = q_refs[0].shape[0]
    blk = 128
    lo = _iota((blk, LANES), 1) < HEAD_DIM
    n_br = len(patterns)
    n_iter = t_len // blk
    masks = []
    for window, dil in patterns:
        assert window // dil == blk
        has_prev = t_len // (dil * blk) > 1
        n_keys = 2 * blk if has_prev else blk
        qrow = _iota((2 * blk, n_keys), 0) % blk
        kcol = _iota((2 * blk, n_keys), 1)
        if has_prev:
            masks.append(((kcol >= blk) & ((kcol - blk) <= qrow), (kcol < blk) & (kcol >= qrow)))
        else:
            masks.append((kcol <= qrow, None))

    def body(it, carry):
        for bi, (window, dil) in enumerate(patterns):
            nblk = t_len // (dil * blk)
            cur_ok, prev_ok = masks[bi]
            cls = it // nblk
            ib = it % nblk

            def rows_at(block, cls=cls, dil=dil):
                start = cls + dil * blk * block
                if dil == 1:
                    return pl.ds(pl.multiple_of(start, blk), blk)
                return pl.ds(start, blk, stride=dil)

            rows = rows_at(ib)
            prows = rows_at(jnp.maximum(ib - 1, 0))
            for tile in range(tiles):
                q_ref, k_ref, v_ref = q_refs[tile], k_refs[tile], v_refs[tile]
                lse_s, out_s = state[2 * (tile * n_br + bi):2 * (tile * n_br + bi) + 2]
                q = q_ref[rows, :] * QK_SCALE
                qst = jnp.concatenate([jnp.where(lo, q, 0.0), jnp.where(lo, 0.0, q)], axis=0).astype(BF16)
                kk = k_ref[rows, :].astype(BF16)
                vv = v_ref[rows, :].astype(BF16)
                ok = cur_ok
                if prev_ok is not None:
                    kk = jnp.concatenate([k_ref[prows, :].astype(BF16), kk], axis=0)
                    vv = jnp.concatenate([v_ref[prows, :].astype(BF16), vv], axis=0)
                    ok = cur_ok | (prev_ok & (ib > 0))
                s = jnp.where(ok, lax.dot_general(qst, kk, NT, preferred_element_type=F32), NEG)
                m = jnp.max(s, axis=1, keepdims=True)
                p = jnp.exp(s - m)
                l = jnp.sum(p, axis=1, keepdims=True)
                pv = jnp.dot(p.astype(BF16), vv, preferred_element_type=F32)
                lse = m + jnp.log(l)
                outn = pv / l
                lse_s[rows, :] = jnp.where(lo, lse[:blk], lse[blk:])
                out_s[rows, :] = jnp.where(lo, outn[:blk], outn[blk:])
        return carry

    lax.fori_loop(0, n_iter, body, 0)

    chunk = 128

    def merge(c, carry):
        rows = pl.ds(pl.multiple_of(c * chunk, chunk), chunk)
        for tile in range(tiles):
            st = state[2 * tile * n_br:2 * (tile + 1) * n_br]
            ms = [st[2 * r][rows, :] for r in range(n_br)]
            m = functools.reduce(jnp.maximum, ms)
            num = jnp.zeros((chunk, LANES), F32)
            den = jnp.zeros((chunk, LANES), F32)
            for r in range(n_br):
                w = jnp.exp(ms[r] - m)
                num = num + w * st[2 * r + 1][rows, :]
                den = den + w
            o_ref[rows, tile * LANES:(tile + 1) * LANES] = (num / den).astype(BF16)
        return carry

    lax.fori_loop(0, t_len // chunk, merge, 0)


def _dilated_prompt(q, k, v, bsz, t_len, *, tiles=4):
    width = q.shape[1]
    specs = [pl.BlockSpec((t_len, LANES), lambda b, j, t=t: (b, j * tiles + t)) for t in range(tiles)]
    return pl.pallas_call(
        functools.partial(_dilated_prompt_body, patterns=C_PATTERNS, tiles=tiles),
        grid=(bsz, width // (tiles * LANES)),
        in_specs=specs * 3,
        out_specs=pl.BlockSpec((t_len, tiles * LANES), lambda b, j: (b, j)),
        out_shape=jax.ShapeDtypeStruct((bsz * t_len, width), BF16),
        scratch_shapes=[pltpu.VMEM((t_len, LANES), F32)] * (2 * len(C_PATTERNS) * tiles),
        compiler_params=_cparams(("arbitrary", "arbitrary")),
        name="dilated_prompt",
    )(*([q] * tiles), *([k] * tiles), *([v] * tiles))


def _dilated_sample_body(q_ref, kn_ref, vn_ref, kc_ref, vc_ref, o_ref, *, patterns):
    t_new, cols = q_ref.shape
    w_len = kc_ref.shape[1]
    nrow = 2 * t_new
    lane = _iota((t_new, LANES), 1)

    def mult(delta):
        cnt = jnp.zeros(delta.shape, F32)
        for window, dil in patterns:
            hit = (delta >= 0) & (delta <= window) & ((delta % dil) == 0)
            cnt = cnt + hit.astype(F32)
        return cnt

    t_c = _iota((nrow, w_len), 0) % t_new
    w_c = mult(w_len + t_c - _iota((nrow, w_len), 1))
    t_n = _iota((nrow, LANES), 0) % t_new
    c_n = _iota((nrow, LANES), 1)
    w_n = jnp.where(c_n < t_new, mult(t_n - c_n), 0.0)
    pad = jnp.zeros((LANES - t_new, LANES), F32)
    for tile in range(cols // LANES):
        cs = slice(tile * LANES, (tile + 1) * LANES)
        q = q_ref[:, cs] * QK_SCALE
        qst = jnp.concatenate([jnp.where(lane < HEAD_DIM, q, 0.0), jnp.where(lane >= HEAD_DIM, q, 0.0)], axis=0)
        qst = qst.astype(BF16)
        kn = jnp.concatenate([kn_ref[:, cs], pad], axis=0).astype(BF16)
        vn = jnp.concatenate([vn_ref[:, cs], pad], axis=0).astype(BF16)
        s_c = jnp.dot(qst, kc_ref[cs, :].astype(BF16), preferred_element_type=F32)
        s_n = lax.dot_general(qst, kn, NT, preferred_element_type=F32)
        s_c = jnp.where(w_c > 0, s_c, NEG)
        s_n = jnp.where(w_n > 0, s_n, NEG)
        m = jnp.maximum(jnp.max(s_c, axis=1, keepdims=True), jnp.max(s_n, axis=1, keepdims=True))
        p_c = w_c * jnp.exp(s_c - m)
        p_n = w_n * jnp.exp(s_n - m)
        l = jnp.sum(p_c, axis=1, keepdims=True) + jnp.sum(p_n, axis=1, keepdims=True)
        acc = (lax.dot_general(p_c.astype(BF16), vc_ref[cs, :].astype(BF16), NT, preferred_element_type=F32)
               + jnp.dot(p_n.astype(BF16), vn, preferred_element_type=F32))
        out = acc / l
        o_ref[:, cs] = jnp.where(lane < HEAD_DIM, out[:t_new], out[t_new:])


def _dilated_sample(q, k, v, row_block0, cache_kt, cache_vt, *, tiles=4):
    bsz, width, w_len = cache_kt.shape
    t_new = 8
    cols = tiles * LANES
    new_spec = pl.BlockSpec((t_new, cols), lambda b, j: (row_block0 + b, j))
    cache_spec = pl.BlockSpec((None, cols, w_len), lambda b, j: (b, j, 0))
    return pl.pallas_call(
        functools.partial(_dilated_sample_body, patterns=C_PATTERNS),
        grid=(bsz, width // cols),
        in_specs=[new_spec, new_spec, new_spec, cache_spec, cache_spec],
        out_specs=pl.BlockSpec((t_new, cols), lambda b, j: (b, j)),
        out_shape=jax.ShapeDtypeStruct((bsz * t_new, width), F32),
        compiler_params=_cparams(("arbitrary", "arbitrary")),
        name="dilated_sample",
    )(q, k, v, cache_kt, cache_vt)


def _moe_route_body(x_ref, c_ref, wo_ref, g_ref, r_ref,
                    x3_ref, hn_ref, comb_ref, pos_ref, post_ref, cnt_ref, before_ref):
    tb = x_ref.shape[0]

    @pl.when(pl.program_id(0) == 0)
    def _():
        before_ref[...] = jnp.where(_iota((tb, tb), 1) < _iota((tb, tb), 0), 1.0, 0.0).astype(BF16)

    x3 = x_ref[...] + jnp.dot(c_ref[...], wo_ref[...], preferred_element_type=F32)
    x3_ref[...] = x3
    hf = _rms(x3, g_ref[...])
    hn = hf.astype(BF16)
    hn_ref[...] = hn
    lane = _iota((tb, LANES), 1)
    h_lo = (hf - hn.astype(F32)).astype(BF16)
    r = r_ref[...]
    r_hi = r.astype(BF16)
    r_lo = (r - r_hi.astype(F32)).astype(BF16)
    logits = (jnp.dot(hn, r_hi, preferred_element_type=F32)
              + (jnp.dot(hn, r_lo, preferred_element_type=F32)
                 + jnp.dot(h_lo, r_hi, preferred_element_type=F32)))
    lg = jnp.where(lane < N_EXPERTS, logits, -jnp.inf)
    m1 = jnp.max(lg, axis=1, keepdims=True)
    p1 = lane == jnp.min(jnp.where(lg == m1, lane, LANES), axis=1, keepdims=True)
    lg2 = jnp.where(p1, -jnp.inf, lg)
    m2 = jnp.max(lg2, axis=1, keepdims=True)
    p2 = lane == jnp.min(jnp.where(lg2 == m2, lane, LANES), axis=1, keepdims=True)
    e2 = jnp.exp(m2 - m1)
    den = 1.0 + e2
    comb_ref[...] = jnp.where(p1, 1.0 / den, 0.0) + jnp.where(p2, e2 / den, 0.0)
    routed = p1 | p2
    ind = jnp.where(routed, 1.0, 0.0)
    slot = jnp.where(routed, jnp.dot(before_ref[...], ind.astype(BF16), preferred_element_type=F32), -1.0)
    pos_ref[...] = slot
    eye = jnp.where(_iota((LANES, LANES), 0) == _iota((LANES, LANES), 1), 1.0, 0.0)
    post_ref[...] = lax.dot_general(eye, slot, NT, precision=HIGHEST, preferred_element_type=F32)
    cnt = jnp.sum(ind, axis=0, keepdims=True)
    cnt_ref[...] = jnp.broadcast_to(cnt, cnt_ref.shape).astype(jnp.int32)


def _moe_route(x, c, wo16, g, router_pad, *, tb):
    rows, dm = x.shape
    nblk = rows // tb
    row_spec = lambda width: pl.BlockSpec((tb, width), lambda i: (i, 0))
    return pl.pallas_call(
        _moe_route_body,
        grid=(nblk,),
        in_specs=[
            row_spec(dm), row_spec(dm),
            pl.BlockSpec((dm, dm), lambda i: (0, 0)),
            pl.BlockSpec((1, dm), lambda i: (0, 0)),
            pl.BlockSpec((dm, LANES), lambda i: (0, 0)),
        ],
        out_specs=[
            row_spec(dm), row_spec(dm), row_spec(LANES), row_spec(LANES),
            pl.BlockSpec((None, LANES, tb), lambda i: (i, 0, 0)),
            pl.BlockSpec((None, 8, LANES), lambda i: (i, 0, 0)),
        ],
        out_shape=[
            jax.ShapeDtypeStruct((rows, dm), F32),
            jax.ShapeDtypeStruct((rows, dm), BF16),
            jax.ShapeDtypeStruct((rows, LANES), F32),
            jax.ShapeDtypeStruct((rows, LANES), F32),
            jax.ShapeDtypeStruct((nblk, LANES, tb), F32),
            jax.ShapeDtypeStruct((nblk, 8, LANES), jnp.int32),
        ],
        scratch_shapes=[pltpu.VMEM((tb, tb), BF16)],
        compiler_params=_cparams(("arbitrary",)),
        name="moe_route",
    )(x, c, wo16, g, router_pad)


def _load_expert_weight(src_hbm, e, dst16, stage, sem):
    chunk = stage.shape[1]
    n_chunks = dst16.shape[0] // chunk

    def copy(c, slot):
        return pltpu.make_async_copy(src_hbm.at[e, pl.ds(c * chunk, chunk), :], stage.at[slot], sem.at[slot])

    copy(0, 0).start()
    for c in range(n_chunks):
        slot = c % 2
        if c + 1 < n_chunks:
            copy(c + 1, 1 - slot).start()
        copy(c, slot).wait()
        dst16[c * chunk:(c + 1) * chunk, :] = stage[slot].astype(BF16)


def _moe_expert_body(cnt_ref, acc_ref, hn_ref, comb_ref, pos_ref, post_ref, gf_ref,
                     wg_hbm, wu_hbm, wd_hbm, o_ref, *rest, e, last, tile_rows, tail_rows):
    tail_ref = rest[:1] if last else ()
    wg_ref, wu_ref, wd_ref, stage_in, stage_out, sem_in, sem_out = rest[len(tail_ref):]
    i = pl.program_id(0)
    tb = hn_ref.shape[0]

    @pl.when(i == 0)
    def _():
        _load_expert_weight(wg_hbm, e, wg_ref, stage_in, sem_in)
        _load_expert_weight(wu_hbm, e, wu_ref, stage_in, sem_in)
        _load_expert_weight(wd_hbm, e, wd_ref, stage_out, sem_out)

    n_tok = cnt_ref[i, e]
    lane = _iota((tb, LANES), 1)
    w_col = jnp.sum(jnp.where(lane == e, comb_ref[...], 0.0), axis=1, keepdims=True)
    slot_col = jnp.sum(jnp.where(lane == e, pos_ref[...], 0.0), axis=1, keepdims=True)
    slot_row = post_ref[pl.ds(e, 1), :]
    finish = (lambda v: _rms(v, gf_ref[...])) if last else (lambda v: v)

    def tile_out(rows, base):
        sub_of_row = _iota((rows, tb), 0).astype(F32)
        sub_of_col = _iota((tb, rows), 1).astype(F32)
        gather = jnp.where(slot_row - base == sub_of_row, 1.0, 0.0).astype(BF16)
        xe = jnp.dot(gather, hn_ref[...], preferred_element_type=F32).astype(BF16)
        gate = jnp.dot(xe, wg_ref[...], preferred_element_type=F32)
        up = jnp.dot(xe, wu_ref[...], preferred_element_type=F32)
        act = (_silu(gate) * up).astype(BF16)
        y = jnp.dot(act, wd_ref[...], preferred_element_type=F32).astype(BF16)
        scatter = jnp.where(slot_col - base == sub_of_col, 1.0, 0.0).astype(BF16)
        return w_col * jnp.dot(scatter, y, preferred_element_type=F32)

    @pl.when(n_tok == 0)
    def _():
        o_ref[...] = finish(acc_ref[...])

    lower = 0
    for rows in tile_rows:
        @pl.when((n_tok > lower) & (n_tok <= rows))
        def _(rows=rows):
            o_ref[...] = finish(acc_ref[...] + tile_out(rows, 0.0))
        lower = rows
    big = tile_rows[-1]

    @pl.when(n_tok > big)
    def _():
        o_ref[...] = acc_ref[...]

        def body(s, carry):
            o_ref[...] += tile_out(big, (s * big).astype(F32))
            return carry
        lax.fori_loop(0, (n_tok + big - 1) // big, body, 0)
        o_ref[...] = finish(o_ref[...])

    if last:
        @pl.when(i == pl.num_programs(0) - 1)
        def _():
            tb_rows = o_ref.shape[0]
            tail_ref[0][...] = o_ref[tb_rows - tail_rows:, :]


def _moe_experts(counts, x3, hn, comb, pos, post, g_final, wg, wu, wd, *, tb, tile_rows, tail_rows):
    rows, dm = x3.shape
    n_exp, _, dff = wg.shape
    assert rows % tb == 0 and tail_rows <= tb
    assert dm % WEIGHT_STAGE_ROWS == 0 and dff % WEIGHT_STAGE_ROWS == 0
    hbm = pl.BlockSpec(memory_space=pl.ANY)
    acc = x3
    for e in range(n_exp):
        last = e == n_exp - 1
        row_spec = pl.BlockSpec((tb, dm), lambda i, cnt: (i, 0))
        if last:
            out_specs = [row_spec, pl.BlockSpec((tail_rows, dm), lambda i, cnt: (0, 0))]
            out_shape = [jax.ShapeDtypeStruct((rows - tail_rows, dm), F32),
                         jax.ShapeDtypeStruct((tail_rows, dm), F32)]
        else:
            out_specs, out_shape = row_spec, jax.ShapeDtypeStruct((rows, dm), F32)
        acc = pl.pallas_call(
            functools.partial(_moe_expert_body, e=e, last=last, tile_rows=tile_rows, tail_rows=tail_rows),
            grid_spec=pltpu.PrefetchScalarGridSpec(
                num_scalar_prefetch=1,
                grid=(rows // tb,),
                in_specs=[
                    pl.BlockSpec((tb, dm), lambda i, cnt: (i, 0)),
                    pl.BlockSpec((tb, dm), lambda i, cnt: (i, 0)),
                    pl.BlockSpec((tb, LANES), lambda i, cnt: (i, 0)),
                    pl.BlockSpec((tb, LANES), lambda i, cnt: (i, 0)),
                    pl.BlockSpec((None, LANES, tb), lambda i, cnt: (i, 0, 0)),
                    pl.BlockSpec((1, dm), lambda i, cnt: (0, 0)),
                    hbm, hbm, hbm,
                ],
                out_specs=out_specs,
                scratch_shapes=[
                    pltpu.VMEM((dm, dff), BF16),
                    pltpu.VMEM((dm, dff), BF16),
                    pltpu.VMEM((dff, dm), BF16),
                    pltpu.VMEM((2, WEIGHT_STAGE_ROWS, dff), F32),
                    pltpu.VMEM((2, WEIGHT_STAGE_ROWS, dm), F32),
                    pltpu.SemaphoreType.DMA((2,)),
                    pltpu.SemaphoreType.DMA((2,)),
                ],
            ),
            out_shape=out_shape,
            compiler_params=_cparams(("arbitrary",)),
            name=f"moe_expert{e}",
        )(counts, acc, hn, comb, pos, post, g_final, wg, wu, wd)
    return acc


def _moe(x, c, wo16, g, router_pad, g_final, wg, wu, wd, tail_rows, *, tb=416,
         tile_rows=(128, 160, 192, 256)):
    x3, hn, comb, pos, post, cnt = _moe_route(x, c, wo16, g, router_pad, tb=tb)
    counts = cnt[:, 0, :N_EXPERTS]
    return _moe_experts(counts, x3, hn, comb, pos, post, g_final, wg, wu, wd, tb=tb,
                        tile_rows=tile_rows, tail_rows=tail_rows)


def kernel(x_prompt, x_sample, cache_b_k, cache_b_v, cache_c_k, cache_c_v, page_table, norm_mix, norm_ffn,
           norm_final, w_in_even, w_out_even, a_ln_g, a_ln_b, a_ws, a_bs, ffn_gate, ffn_up, ffn_down,
           w_in_odd, w_out_odd, router, moe_gate, moe_up, moe_down):
    bsz, t_len, dm = x_prompt.shape
    dbsz, t_new, _ = x_sample.shape
    assert norm_mix.shape[0] == 2 and w_in_even.shape[0] == 1 and w_in_odd.shape[0] == 1
    assert cache_b_k.shape[0] == 1 and cache_c_k.shape[0] == 1
    assert t_new == 8 and A_CHUNK % t_new == 0
    n_p = bsz * t_len
    n_s = dbsz * t_new
    aw = a_ln_g.shape[-1]
    n_groups = aw // HEAD_DIM
    seq_per_chunk = A_CHUNK // t_new


    ws = a_ws[0]
    ws_sample = jnp.tile(ws[:, :t_new, :t_new], (1, seq_per_chunk, seq_per_chunk))
    ws2 = jnp.stack([ws, ws_sample])
    bias_p = jnp.repeat(a_bs[0].T, HEAD_DIM, axis=1)
    bias_s = jnp.repeat(jnp.tile(a_bs[0][:, :t_new].T, (seq_per_chunk, 1)), HEAD_DIM, axis=1)
    bias2 = jnp.stack([bias_p, bias_s])
    row = lambda vec: vec.reshape(1, -1)
    router_pad = jnp.pad(router[0], ((0, 0), (0, LANES - router.shape[-1])))

    w_even = w_in_even[0].astype(BF16)
    a_out, v_a, q_b, k_b, v_b, kt_b, vt_b, v_a_last, x = _even_in(
        x_prompt.reshape(n_p, dm), x_sample.reshape(n_s, dm), bsz, t_len, row(norm_mix[0]), w_even, row(a_ln_g[0]), row(a_ln_b[0]), ws2, bias2)
    b_prompt = _moba_prompt(q_b, k_b, v_b, bsz, t_len)
    page = cache_b_k.shape[2]
    bw = q_b.shape[1]
    feature_major = lambda c: jnp.transpose(c, (0, 1, 3, 4, 2))
    pool_kt = feature_major(cache_b_k).reshape(-1, B_HEADS, HEAD_DIM, page)
    pool_vt = feature_major(cache_b_v).reshape(-1, B_HEADS, HEAD_DIM, page)
    sel = _moba_select(q_b, n_p // t_new, pool_kt.reshape(-1, bw, page), page_table)
    sel = sel.reshape(dbsz, B_HEADS * t_new, LANES)[:, :, :B_TOPK].reshape(dbsz, -1)
    new_t = lambda a: jnp.transpose(a[n_p:].reshape(dbsz, t_new, -1), (0, 2, 1))
    ppb = B_BLOCK // page
    logical = (jnp.maximum(sel, 0)[:, :, None] * ppb + jnp.arange(ppb, dtype=jnp.int32)).reshape(dbsz, -1)
    hit = logical[:, :, None] == jnp.arange(page_table.shape[1], dtype=jnp.int32)
    sel_pages = jnp.sum(jnp.where(hit, page_table[:, None, :], 0), axis=-1)
    b_sample_t = _moba_gather(new_t(q_b), new_t(k_b), new_t(v_b), pool_kt, pool_vt, sel_pages, sel)
    b_sample = jnp.transpose(b_sample_t, (0, 2, 1)).reshape(n_s, bw)
    b_out = jnp.concatenate([b_prompt, b_sample.astype(BF16)], axis=0)
    x2 = _ffn_even(x, a_out, b_out, w_out_even[0].astype(BF16), row(norm_ffn[0]),
                   ffn_gate[0].astype(BF16), ffn_up[0].astype(BF16), ffn_down[0].astype(BF16))

    w_odd = w_in_odd[0].astype(BF16)
    q_c, k_c, v_c, kt_c, vt_c = _odd_in(x2, bsz, t_len, row(norm_mix[1]), w_odd)
    c_prompt = _dilated_prompt(q_c, k_c, v_c, bsz, t_len)
    cw = q_c.shape[1]
    w_len = cache_c_k.shape[2]
    c_sample = _dilated_sample(q_c, k_c, v_c, n_p // t_new,
                               feature_major(cache_c_k).reshape(-1, cw, w_len),
                               feature_major(cache_c_v).reshape(-1, cw, w_len))
    c_out = jnp.concatenate([c_prompt, c_sample.astype(BF16)], axis=0)
    y_p, y_s = _moe(x2, c_out, w_out_odd[0].astype(BF16), row(norm_ffn[1]), router_pad, row(norm_final),
                    moe_gate[0], moe_up[0], moe_down[0], n_s)

    bh = (B_HEADS, HEAD_DIM)
    ch = (cw // HEAD_DIM, HEAD_DIM)
    assert t_len % 256 == 0
    c_keep_p = min(C_PATTERNS[-1][0], t_len)
    leaf = lambda a_t, hd: jnp.transpose(a_t.reshape(1, bsz, *hd, a_t.shape[-1]), (0, 1, 4, 2, 3))
    return (
        y_p.reshape(bsz, t_len, dm),
        y_s.reshape(dbsz, t_new, dm),
        v_a_last.reshape(1, bsz, A_CHUNK, aw),
        v_a[n_p:].reshape(1, dbsz, t_new, aw),
        leaf(kt_b, bh),
        leaf(vt_b, bh),
        k_b[n_p:].reshape(1, dbsz, t_new, *bh),
        v_b[n_p:].reshape(1, dbsz, t_new, *bh),
        leaf(kt_c[:, :, t_len - c_keep_p:], ch),
        leaf(vt_c[:, :, t_len - c_keep_p:], ch),
        k_c[n_p:].reshape(1, dbsz, t_new, *ch),
        v_c[n_p:].reshape(1, dbsz, t_new, *ch),
    )
```

```python
import functools

import jax
import jax.numpy as jnp
from jax import lax
from jax.experimental import pallas as pl
from jax.experimental.pallas import tpu as pltpu

F32 = jnp.float32
BF16 = jnp.bfloat16
HIGHEST = lax.Precision.HIGHEST

HEAD_DIM = 64
LANES = 128
A_CHUNK = 128
B_HEADS = 8
B_BLOCK = 256
B_TOPK = 3
C_PATTERNS = ((128, 1), (512, 4), (2048, 16))
N_EXPERTS = 8
RMS_EPS = 1e-6
LN_EPS = 1e-5
NEG = -1e30
QK_SCALE = HEAD_DIM ** -0.5
VMEM_LIMIT = 56 * 1024 * 1024
WEIGHT_STAGE_ROWS = 256

NT = (((1,), (1,)), ((), ()))


def _cparams(sem):
    return pltpu.CompilerParams(dimension_semantics=sem, vmem_limit_bytes=VMEM_LIMIT)


def _rms(x, g):
    return x * lax.rsqrt(jnp.mean(x * x, axis=-1, keepdims=True) + RMS_EPS) * g


def _gelu(x):
    return 0.5 * x * (1.0 + lax.erf(x * 0.7071067811865476))


def _iota(shape, dim):
    return lax.broadcasted_iota(jnp.int32, shape, dim)


def _feature_major_kv(i, n_prompt_tiles, k_tile, v_tile, kt_ref, vt_ref):
    @pl.when(i < n_prompt_tiles)
    def _():
        kt_ref[...] = k_tile.T
        vt_ref[...] = v_tile.T


def _feature_major_spec(width, tm, n_prompt_tiles, tiles_per_seq):
    def imap(i):
        ic = jnp.minimum(i, n_prompt_tiles - 1)
        return (ic // tiles_per_seq, 0, ic % tiles_per_seq)
    return pl.BlockSpec((None, width, tm), imap)


def _even_in_body(xp_ref, xs_ref, g_ref, w_ref, lng_ref, lnb_ref, ws_ref, bias_ref,
                  a_ref, v_ref, q_ref, k_ref, vv_ref, kt_ref, vt_ref, vlast_ref, xcat_ref,
                  *, n_prompt_tiles, tiles_per_seq, aw, bw):
    i = pl.program_id(0)
    tm = xp_ref.shape[0]
    x = jnp.where(i < n_prompt_tiles, xp_ref[...], xs_ref[...])
    xcat_ref[...] = x
    h = _rms(x, g_ref[...]).astype(BF16)
    y = jnp.dot(h, w_ref[...], preferred_element_type=F32)
    _feature_major_kv(i, n_prompt_tiles, y[:, 2 * aw + bw:2 * aw + 2 * bw], y[:, 2 * aw + 2 * bw:], kt_ref, vt_ref)
    u = _gelu(y[:, :aw])
    v = _gelu(y[:, aw:2 * aw])
    mu = jnp.mean(v, axis=-1, keepdims=True)
    vc = v - mu
    var = jnp.mean(vc * vc, axis=-1, keepdims=True)
    v = vc * lax.rsqrt(var + LN_EPS) * lng_ref[...] + lnb_ref[...]
    v_ref[...] = v

    @pl.when((i < n_prompt_tiles) & (i % tiles_per_seq == tiles_per_seq - 1))
    def _():
        vlast_ref[...] = v[tm - A_CHUNK:, :]
    q_ref[...] = y[:, 2 * aw:2 * aw + bw]
    k_ref[...] = y[:, 2 * aw + bw:2 * aw + 2 * bw]
    vv_ref[...] = y[:, 2 * aw + 2 * bw:]

    row = _iota((A_CHUNK, A_CHUNK), 0)
    col = _iota((A_CHUNK, A_CHUNK), 1)
    is_prompt = i < n_prompt_tiles
    ok = (col <= row) & (((row // 8) == (col // 8)) | is_prompt)
    lo = _iota((A_CHUNK, LANES), 1) < HEAD_DIM
    v16 = v.astype(BF16)
    n_groups = aw // HEAD_DIM
    wm = [jnp.where(ok, ws_ref[g], 0.0).astype(BF16) for g in range(n_groups)]
    for c in range(tm // A_CHUNK):
        rs = slice(c * A_CHUNK, (c + 1) * A_CHUNK)
        for j in range(aw // LANES):
            cs = slice(j * LANES, (j + 1) * LANES)
            vj = v16[rs, cs]
            ma = jnp.dot(wm[2 * j], vj, preferred_element_type=F32)
            mb = jnp.dot(wm[2 * j + 1], vj, preferred_element_type=F32)
            mixed = jnp.where(lo, ma, mb) + bias_ref[:, cs]
            a_ref[rs, cs] = (u[rs, cs] * mixed).astype(BF16)


def _even_in(xp, xs, bsz, t_len, g, w16, lng, lnb, ws2, bias2, *, tm=256):
    dm = xp.shape[1]
    assert xs.shape[0] == tm
    rows = xp.shape[0] + tm
    aw = lng.shape[-1]
    n_in = w16.shape[1]
    bw = (n_in - 2 * aw) // 3
    n_groups = aw // HEAD_DIM
    npt = bsz * t_len // tm
    sel = lambda i: jnp.minimum(i // npt, 1)
    row_spec = lambda width: pl.BlockSpec((tm, width), lambda i: (i, 0))
    tps = t_len // tm
    fm_spec = _feature_major_spec(bw, tm, npt, tps)
    return pl.pallas_call(
        functools.partial(_even_in_body, n_prompt_tiles=npt, tiles_per_seq=tps, aw=aw, bw=bw),
        grid=(rows // tm,),
        in_specs=[
            pl.BlockSpec((tm, dm), lambda i: (jnp.minimum(i, npt - 1), 0)),
            pl.BlockSpec((tm, dm), lambda i: (0, 0)),
            pl.BlockSpec((1, dm), lambda i: (0, 0)),
            pl.BlockSpec((dm, n_in), lambda i: (0, 0)),
            pl.BlockSpec((1, aw), lambda i: (0, 0)),
            pl.BlockSpec((1, aw), lambda i: (0, 0)),
            pl.BlockSpec((None, n_groups, A_CHUNK, A_CHUNK), lambda i: (sel(i), 0, 0, 0)),
            pl.BlockSpec((None, A_CHUNK, aw), lambda i: (sel(i), 0, 0)),
        ],
        out_specs=[row_spec(aw), row_spec(aw), row_spec(bw), row_spec(bw), row_spec(bw), fm_spec, fm_spec,
                   pl.BlockSpec((None, A_CHUNK, aw), lambda i: (jnp.minimum(i, npt - 1) // tps, 0, 0)),
                   row_spec(dm)],
        out_shape=[
            jax.ShapeDtypeStruct((rows, aw), BF16),
            jax.ShapeDtypeStruct((rows, aw), F32),
            jax.ShapeDtypeStruct((rows, bw), F32),
            jax.ShapeDtypeStruct((rows, bw), F32),
            jax.ShapeDtypeStruct((rows, bw), F32),
            jax.ShapeDtypeStruct((bsz, bw, t_len), F32),
            jax.ShapeDtypeStruct((bsz, bw, t_len), F32),
            jax.ShapeDtypeStruct((bsz, A_CHUNK, aw), F32),
            jax.ShapeDtypeStruct((rows, dm), F32),
        ],
        compiler_params=_cparams(("arbitrary",)),
        name="even_in",
    )(xp, xs, g, w16, lng, lnb, ws2, bias2)


def _moba_prompt_body(q_ref, k_ref, v_ref, o_ref, k16, v16, mt_ref, s_scr, p_scr, *, nb):
    i = pl.program_id(1)
    t_len, width = k_ref.shape
    n_gate = B_HEADS * nb

    @pl.when(i == 0)
    def _():
        k = k_ref[...]
        k16[...] = k.astype(BF16)
        v16[...] = v_ref[...].astype(BF16)
        r = _iota((LANES, t_len), 0)
        key = _iota((LANES, t_len), 1)
        avg = jnp.where(((key // B_BLOCK) == (r % nb)) & (r < n_gate), 1.0 / B_BLOCK, 0.0).astype(F32)
        mt = jnp.dot(avg, k, precision=HIGHEST, preferred_element_type=F32)
        rr = _iota((LANES, width), 0)
        cc = _iota((LANES, width), 1)
        mt_ref[...] = jnp.where(((rr // nb) == (cc // HEAD_DIM)) & (rr < n_gate), mt, 0.0)

    q = q_ref[...]
    tq = q.shape[0]
    mt = mt_ref[...]
    q_hi = q.astype(BF16)
    q_lo = (q - q_hi.astype(F32)).astype(BF16)
    mt_hi = mt.astype(BF16)
    mt_lo = (mt - mt_hi.astype(F32)).astype(BF16)
    gate = (lax.dot_general(q_hi, mt_hi, NT, preferred_element_type=F32)
            + (lax.dot_general(q_hi, mt_lo, NT, preferred_element_type=F32)
               + lax.dot_general(q_lo, mt_hi, NT, preferred_element_type=F32)))
    lane = _iota((tq, LANES), 1)
    n_of = lane % nb
    valid = (n_of < i) & (lane < n_gate)
    gm = jnp.where(valid, gate, -jnp.inf)
    rank = jnp.zeros((tq, LANES), jnp.int32)
    for s in range(1, nb):
        fwd = pltpu.roll(gm, LANES - s, 1)
        bwd = pltpu.roll(gm, nb - s, 1)
        wrap = (n_of + s) >= nb
        other = jnp.where(wrap, bwd, fwd)
        m_idx = jnp.where(wrap, n_of + s - nb, n_of + s)
        beats = (other > gm) | ((other == gm) & (m_idx < n_of))
        rank = rank + beats.astype(jnp.int32)
    selbias = jnp.where(valid & (rank < B_TOPK), 0.0, NEG)

    row2 = _iota((2 * tq, B_BLOCK), 0) % tq
    col2 = _iota((2 * tq, B_BLOCK), 1)
    causal_bias = jnp.where(col2 <= row2, 0.0, NEG)
    lo = lane < HEAD_DIM
    own = pl.multiple_of(i * B_BLOCK, B_BLOCK)
    half = B_BLOCK // 2

    key_lane = _iota((B_BLOCK, LANES), 1)

    group = s_scr.shape[0]

    def zero_body(n, carry):
        for g in range(group):
            p_scr[g, n] = jnp.zeros((2 * tq, B_BLOCK), BF16)
        return carry

    lax.fori_loop(i + 1, nb, zero_body, 0)
    for j0 in range(0, width // LANES, group):
        cols = [slice((j0 + g) * LANES, (j0 + g + 1) * LANES) for g in range(group)]
        q_aug, m_tiles = [], []
        for g in range(group):
            j = j0 + g
            qj = q[:, cols[g]] * QK_SCALE
            qst = jnp.concatenate([jnp.where(lo, qj, 0.0), jnp.where(lo, 0.0, qj)], axis=0).astype(BF16)
            s = (lax.dot_general(qst, k16[pl.ds(own, B_BLOCK), cols[g]], NT, preferred_element_type=F32)
                 + causal_bias)
            s_scr[g, i] = s
            m_tiles.append(jnp.maximum(s[:, :half], s[:, half:]))
            sb = []
            for a in range(2):
                off = (2 * j + a) * nb
                moved = selbias if off == 0 else pltpu.roll(selbias, LANES - off, 1)
                sb.append(jnp.where(lane < nb, moved, 0.0))
            q_aug.append(jnp.concatenate([qst, jnp.concatenate(sb, axis=0).astype(BF16)], axis=1))

        def score_body(n, m_tiles, q_aug=q_aug, cols=cols):
            start = pl.multiple_of(n * B_BLOCK, B_BLOCK)
            onehot = jnp.where(key_lane == n, 1.0, 0.0).astype(BF16)
            out = []
            for g in range(group):
                k_aug = jnp.concatenate([k16[pl.ds(start, B_BLOCK), cols[g]], onehot], axis=1)
                s = lax.dot_general(q_aug[g], k_aug, NT, preferred_element_type=F32)
                s_scr[g, n] = s
                out.append(jnp.maximum(m_tiles[g], jnp.maximum(s[:, :half], s[:, half:])))
            return tuple(out)

        m_tiles = lax.fori_loop(0, i, score_body, tuple(m_tiles))
        ms = [jnp.max(m_tiles[g], axis=1, keepdims=True) for g in range(group)]

        def prob_body(n, l_tiles, ms=ms):
            out = []
            for g in range(group):
                p = jnp.exp(s_scr[g, n] - ms[g])
                p_scr[g, n] = p.astype(BF16)
                out.append(l_tiles[g] + p[:, :half] + p[:, half:])
            return tuple(out)

        l_tiles = lax.fori_loop(0, i + 1, prob_body,
                                tuple(jnp.zeros((2 * tq, half), F32) for _ in range(group)))
        for g in range(group):
            l = jnp.sum(l_tiles[g], axis=1, keepdims=True)
            acc = jnp.dot(p_scr[g, 0], v16[0:B_BLOCK, cols[g]], preferred_element_type=F32)
            for n in range(1, nb):
                acc = acc + jnp.dot(p_scr[g, n], v16[n * B_BLOCK:(n + 1) * B_BLOCK, cols[g]],
                                    preferred_element_type=F32)
            o = acc / l
            o_ref[:, cols[g]] = jnp.where(lo, o[:tq], o[tq:]).astype(BF16)


def _moba_prompt(q, k, v, bsz, t_len):
    width = q.shape[1]
    nb = t_len // B_BLOCK
    return pl.pallas_call(
        functools.partial(_moba_prompt_body, nb=nb),
        grid=(bsz, nb),
        in_specs=[
            pl.BlockSpec((B_BLOCK, width), lambda b, i: (b * nb + i, 0)),
            pl.BlockSpec((t_len, width), lambda b, i: (b, 0)),
            pl.BlockSpec((t_len, width), lambda b, i: (b, 0)),
        ],
        out_specs=pl.BlockSpec((B_BLOCK, width), lambda b, i: (b * nb + i, 0)),
        out_shape=jax.ShapeDtypeStruct((bsz * t_len, width), BF16),
        scratch_shapes=[
            pltpu.VMEM((t_len, width), BF16),
            pltpu.VMEM((t_len, width), BF16),
            pltpu.VMEM((LANES, width), F32),
            pltpu.VMEM((2, nb, 2 * B_BLOCK, B_BLOCK), F32),
            pltpu.VMEM((2, nb, 2 * B_BLOCK, B_BLOCK), BF16),
        ],
        compiler_params=_cparams(("arbitrary", "arbitrary")),
        name="moba_prompt",
    )(q, k, v)


def _moba_select_body(pt_ref, q_ref, *refs, npg, ppb, n_blocks):
    pages = refs[:npg]
    idx_ref = refs[npg]
    mt_ref = refs[npg + 1]
    s = pl.program_id(1)
    t_new, width = q_ref.shape
    bps = npg // ppb

    @pl.when(s == 0)
    def _():
        mt_ref[...] = jnp.zeros(mt_ref.shape, F32)

    lane_w = _iota((width, LANES), 1)
    cur = mt_ref[...]
    for blk in range(bps):
        tot = pages[blk * ppb][...]
        for r in range(1, ppb):
            tot = tot + pages[blk * ppb + r][...]
        col = jnp.sum(tot, axis=1, keepdims=True) * (1.0 / B_BLOCK)
        cur = jnp.where(lane_w == s * bps + blk, col, cur)
    mt_ref[...] = cur

    @pl.when(s == pl.num_programs(1) - 1)
    def _():
        ht = B_HEADS * t_new
        rr = _iota((ht, width), 0)
        cc = _iota((ht, width), 1)
        q = q_ref[...]
        qst = jnp.where((rr // t_new) == (cc // HEAD_DIM), jnp.concatenate([q] * B_HEADS, axis=0), 0.0)
        gate = jnp.dot(qst, cur, precision=HIGHEST, preferred_element_type=F32)
        lane = _iota((ht, LANES), 1)
        g = jnp.where(lane < n_blocks, gate, -jnp.inf)
        out = jnp.full((ht, LANES), -1, jnp.int32)
        for r in range(B_TOPK):
            mx = jnp.max(g, axis=1, keepdims=True)
            idx = jnp.min(jnp.where(g == mx, lane, LANES), axis=1, keepdims=True)
            out = jnp.where(lane == r, jnp.where(mx > -jnp.inf, idx, -1), out)
            g = jnp.where(lane == idx, -jnp.inf, g)
        idx_ref[...] = out


def _moba_select(q, row_block0, pool_kt, page_table, *, npg=64):
    bsz, n_pages = page_table.shape
    _, width, page = pool_kt.shape
    ppb = B_BLOCK // page
    t_new = 8
    ht = B_HEADS * t_new
    page_specs = [
        pl.BlockSpec((None, width, page), lambda b, s, pt, r=r: (pt[b, s * npg + r], 0, 0))
        for r in range(npg)
    ]
    return pl.pallas_call(
        functools.partial(_moba_select_body, npg=npg, ppb=ppb, n_blocks=n_pages // ppb),
        grid_spec=pltpu.PrefetchScalarGridSpec(
            num_scalar_prefetch=1,
            grid=(bsz, n_pages // npg),
            in_specs=[pl.BlockSpec((t_new, width), lambda b, s, pt: (row_block0 + b, 0))] + page_specs,
            out_specs=pl.BlockSpec((ht, LANES), lambda b, s, pt: (b, 0)),
            scratch_shapes=[pltpu.VMEM((width, LANES), F32)],
        ),
        out_shape=jax.ShapeDtypeStruct((bsz * ht, LANES), jnp.int32),
        compiler_params=_cparams(("arbitrary", "arbitrary")),
        name="moba_select",
    )(page_table, q, *([pool_kt] * npg))


def _moba_gather_body(pages_ref, sel_ref, qt_ref, knt_ref, vnt_ref, pool_k, pool_v, o_ref,
                      kbuf, vbuf, sem, *, ppb):
    b = pl.program_id(0)
    h = pl.program_id(1)
    n_heads = pl.num_programs(1)
    t_new = qt_ref.shape[1]
    n_chunk = t_new * B_TOPK * ppb
    step = b * n_heads + h
    slot = step % 2

    def chunk_copies(bb, hh, c, to_slot):
        page = pages_ref[bb, hh * n_chunk + c]
        return (pltpu.make_async_copy(pool_k.at[page, hh], kbuf.at[to_slot, c], sem.at[to_slot, 0]),
                pltpu.make_async_copy(pool_v.at[page, hh], vbuf.at[to_slot, c], sem.at[to_slot, 1]))

    def start_all(bb, hh, to_slot):
        def body(c, carry):
            for cp in chunk_copies(bb, hh, c, to_slot):
                cp.start()
            return carry
        lax.fori_loop(0, n_chunk, body, 0)

    @pl.when(step == 0)
    def _():
        start_all(b, h, slot)

    nxt = step + 1

    @pl.when(nxt < pl.num_programs(0) * n_heads)
    def _():
        start_all(nxt // n_heads, nxt % n_heads, 1 - slot)

    pltpu.make_async_copy(pool_k.at[pl.ds(0, n_chunk), 0], kbuf.at[slot], sem.at[slot, 0]).wait()
    pltpu.make_async_copy(pool_v.at[pl.ds(0, n_chunk), 0], vbuf.at[slot], sem.at[slot, 1]).wait()
    page_rows = kbuf.shape[3]
    kch = [kbuf.at[slot, c] for c in range(n_chunk)]
    vch = [vbuf.at[slot, c] for c in range(n_chunk)]
    qt = qt_ref[...] * QK_SCALE
    knt = knt_ref[...]
    vnt = vnt_ref[...]
    per_t = B_TOPK * ppb
    rows, own_rows = [], []
    for t in range(t_new):
        qcol = qt[:, t:t + 1]
        own_rows.append(jnp.sum(qcol * knt, axis=0, keepdims=True))
        parts = []
        for r in range(B_TOPK):
            picked = sel_ref[b, (h * t_new + t) * B_TOPK + r] >= 0
            for pg in range(ppb):
                kc = kch[(t * B_TOPK + r) * ppb + pg][...]
                parts.append(jnp.where(picked, jnp.sum(qcol * kc, axis=0, keepdims=True), NEG))
        rows.append(jnp.concatenate(parts, axis=1))
    s_all = jnp.concatenate(rows, axis=0)
    s_own = jnp.concatenate(own_rows, axis=0)
    s_own = jnp.where(_iota((t_new, t_new), 1) <= _iota((t_new, t_new), 0), s_own, NEG)
    m = jnp.maximum(jnp.max(s_all, axis=1, keepdims=True), jnp.max(s_own, axis=1, keepdims=True))
    p_all = jnp.exp(s_all - m)
    p_own = jnp.exp(s_own - m)
    l = jnp.sum(p_all, axis=1, keepdims=True) + jnp.sum(p_own, axis=1, keepdims=True)
    lane_o = _iota((HEAD_DIM, t_new), 1)
    out = jnp.zeros((HEAD_DIM, t_new), F32)
    for t in range(t_new):
        accv = jnp.zeros((HEAD_DIM, page_rows), F32)
        for c in range(per_t):
            accv = accv + p_all[t:t + 1, c * page_rows:(c + 1) * page_rows] * vch[t * per_t + c][...]
        acc = (jnp.sum(accv, axis=1, keepdims=True)
               + jnp.sum(p_own[t:t + 1, :] * vnt, axis=1, keepdims=True))
        out = jnp.where(lane_o == t, acc / l[t:t + 1, :], out)
    o_ref[...] = out


def _moba_gather(qt, knt, vnt, pool_kt, pool_vt, page_table, sel):
    bsz, width, t_new = qt.shape
    _, n_heads, hd, page = pool_kt.shape
    ppb = B_BLOCK // page
    n_chunk = t_new * B_TOPK * ppb
    new_spec = pl.BlockSpec((None, hd, t_new), lambda b, h, pt, sl: (b, h, 0))
    hbm = pl.BlockSpec(memory_space=pl.ANY)
    return pl.pallas_call(
        functools.partial(_moba_gather_body, ppb=ppb),
        grid_spec=pltpu.PrefetchScalarGridSpec(
            num_scalar_prefetch=2,
            grid=(bsz, n_heads),
            in_specs=[new_spec, new_spec, new_spec, hbm, hbm],
            out_specs=pl.BlockSpec((None, hd, t_new), lambda b, h, pt, sl: (b, h, 0)),
            scratch_shapes=[
                pltpu.VMEM((2, n_chunk, hd, page), F32),
                pltpu.VMEM((2, n_chunk, hd, page), F32),
                pltpu.SemaphoreType.DMA((2, 2)),
            ],
        ),
        out_shape=jax.ShapeDtypeStruct((bsz, width, t_new), F32),
        compiler_params=_cparams(("arbitrary", "arbitrary")),
        name="moba_gather",
    )(page_table, sel, qt, knt, vnt, pool_kt, pool_vt)


def _silu(x):
    return x / (1.0 + jnp.exp(-x))


def _ffn_even_body(x_ref, a_ref, b_ref, wo_ref, g_ref, wg_ref, wu_ref, wd_ref, o_ref, hn_ref, acc_ref):
    f = pl.program_id(1)
    aw = a_ref.shape[1]

    @pl.when(f == 0)
    def _():
        x1 = (x_ref[...]
              + jnp.dot(a_ref[...], wo_ref[:aw, :], preferred_element_type=F32)
              + jnp.dot(b_ref[...], wo_ref[aw:, :], preferred_element_type=F32))
        acc_ref[...] = x1
        hn_ref[...] = _rms(x1, g_ref[...]).astype(BF16)

    hn = hn_ref[...]
    gate = jnp.dot(hn, wg_ref[...], preferred_element_type=F32)
    up = jnp.dot(hn, wu_ref[...], preferred_element_type=F32)
    act = (_silu(gate) * up).astype(BF16)
    acc_ref[...] += jnp.dot(act, wd_ref[...], preferred_element_type=F32)

    @pl.when(f == pl.num_programs(1) - 1)
    def _():
        o_ref[...] = acc_ref[...]


def _ffn_even(x, a, b, wo16, g, wg16, wu16, wd16, *, tm=320, tf=2816):
    rows, dm = x.shape
    aw = a.shape[1]
    bw = b.shape[1]
    dff = wg16.shape[1]
    return pl.pallas_call(
        _ffn_even_body,
        grid=(rows // tm, dff // tf),
        in_specs=[
            pl.BlockSpec((tm, dm), lambda i, f: (i, 0)),
            pl.BlockSpec((tm, aw), lambda i, f: (i, 0)),
            pl.BlockSpec((tm, bw), lambda i, f: (i, 0)),
            pl.BlockSpec((aw + bw, dm), lambda i, f: (0, 0)),
            pl.BlockSpec((1, dm), lambda i, f: (0, 0)),
            pl.BlockSpec((dm, tf), lambda i, f: (0, f), pipeline_mode=pl.Buffered(1)),
            pl.BlockSpec((dm, tf), lambda i, f: (0, f), pipeline_mode=pl.Buffered(1)),
            pl.BlockSpec((tf, dm), lambda i, f: (f, 0), pipeline_mode=pl.Buffered(1)),
        ],
        out_specs=pl.BlockSpec((tm, dm), lambda i, f: (i, 0)),
        out_shape=jax.ShapeDtypeStruct((rows, dm), F32),
        scratch_shapes=[pltpu.VMEM((tm, dm), BF16), pltpu.VMEM((tm, dm), F32)],
        compiler_params=_cparams(("arbitrary", "arbitrary")),
        name="ffn_even",
    )(x, a, b, wo16, g, wg16, wu16, wd16)


def _odd_in_body(x_ref, g_ref, w_ref, q_ref, k_ref, v_ref, kt_ref, vt_ref, *, n_prompt_tiles):
    cw = q_ref.shape[1]
    h = _rms(x_ref[...], g_ref[...]).astype(BF16)
    y = jnp.dot(h, w_ref[...], preferred_element_type=F32)
    q_ref[...] = y[:, :cw]
    k_ref[...] = y[:, cw:2 * cw]
    v_ref[...] = y[:, 2 * cw:]
    _feature_major_kv(pl.program_id(0), n_prompt_tiles, y[:, cw:2 * cw], y[:, 2 * cw:], kt_ref, vt_ref)


def _odd_in(x, bsz, t_len, g, w16, *, tm=256):
    rows, dm = x.shape
    cw = w16.shape[1] // 3
    npt = bsz * t_len // tm
    out = jax.ShapeDtypeStruct((rows, cw), F32)
    out_t = jax.ShapeDtypeStruct((bsz, cw, t_len), F32)
    spec = pl.BlockSpec((tm, cw), lambda i: (i, 0))
    fm_spec = _feature_major_spec(cw, tm, npt, t_len // tm)
    return pl.pallas_call(
        functools.partial(_odd_in_body, n_prompt_tiles=npt),
        grid=(rows // tm,),
        in_specs=[
            pl.BlockSpec((tm, dm), lambda i: (i, 0)),
            pl.BlockSpec((1, dm), lambda i: (0, 0)),
            pl.BlockSpec((dm, 3 * cw), lambda i: (0, 0)),
        ],
        out_specs=[spec, spec, spec, fm_spec, fm_spec],
        out_shape=[out, out, out, out_t, out_t],
        compiler_params=_cparams(("arbitrary",)),
        name="odd_in",
    )(x, g, w16)


def _dilated_prompt_body(*refs, patterns, tiles):
    q_refs, k_refs, v_refs = (refs[g * tiles:(g + 1) * tiles] for g in range(3))
    o_ref = refs[3 * tiles]
    state = refs[3 * tiles + 1:]
    t_len = q_refs[0].shape[0]
    blk = 128
    lo = _iota((blk, LANES), 1) < HEAD_DIM
    n_br = len(patterns)
    n_iter = t_len // blk
    masks = []
    for window, dil in patterns:
        assert window // dil == blk
        has_prev = t_len // (dil * blk) > 1
        n_keys = 2 * blk if has_prev else blk
        qrow = _iota((2 * blk, n_keys), 0) % blk
        kcol = _iota((2 * blk, n_keys), 1)
        if has_prev:
            masks.append(((kcol >= blk) & ((kcol - blk) <= qrow), (kcol < blk) & (kcol >= qrow)))
        else:
            masks.append((kcol <= qrow, None))

    def body(it, carry):
        for bi, (window, dil) in enumerate(patterns):
            nblk = t_len // (dil * blk)
            cur_ok, prev_ok = masks[bi]
            cls = it // nblk
            ib = it % nblk

            def rows_at(block, cls=cls, dil=dil):
                start = cls + dil * blk * block
                if dil == 1:
                    return pl.ds(pl.multiple_of(start, blk), blk)
                return pl.ds(start, blk, stride=dil)

            rows = rows_at(ib)
            prows = rows_at(jnp.maximum(ib - 1, 0))
            for tile in range(tiles):
                q_ref, k_ref, v_ref = q_refs[tile], k_refs[tile], v_refs[tile]
                lse_s, out_s = state[2 * (tile * n_br + bi):2 * (tile * n_br + bi) + 2]
                q = q_ref[rows, :] * QK_SCALE
                qst = jnp.concatenate([jnp.where(lo, q, 0.0), jnp.where(lo, 0.0, q)], axis=0).astype(BF16)
                kk = k_ref[rows, :].astype(BF16)
                vv = v_ref[rows, :].astype(BF16)
                ok = cur_ok
                if prev_ok is not None:
                    kk = jnp.concatenate([k_ref[prows, :].astype(BF16), kk], axis=0)
                    vv = jnp.concatenate([v_ref[prows, :].astype(BF16), vv], axis=0)
                    ok = cur_ok | (prev_ok & (ib > 0))
                s = jnp.where(ok, lax.dot_general(qst, kk, NT, preferred_element_type=F32), NEG)
                m = jnp.max(s, axis=1, keepdims=True)
                p = jnp.exp(s - m)
                l = jnp.sum(p, axis=1, keepdims=True)
                pv = jnp.dot(p.astype(BF16), vv, preferred_element_type=F32)
                lse = m + jnp.log(l)
                outn = pv / l
                lse_s[rows, :] = jnp.where(lo, lse[:blk], lse[blk:])
                out_s[rows, :] = jnp.where(lo, outn[:blk], outn[blk:])
        return carry

    lax.fori_loop(0, n_iter, body, 0)

    chunk = 128

    def merge(c, carry):
        rows = pl.ds(pl.multiple_of(c * chunk, chunk), chunk)
        for tile in range(tiles):
            st = state[2 * tile * n_br:2 * (tile + 1) * n_br]
            ms = [st[2 * r][rows, :] for r in range(n_br)]
            m = functools.reduce(jnp.maximum, ms)
            num = jnp.zeros((chunk, LANES), F32)
            den = jnp.zeros((chunk, LANES), F32)
            for r in range(n_br):
                w = jnp.exp(ms[r] - m)
                num = num + w * st[2 * r + 1][rows, :]
                den = den + w
            o_ref[rows, tile * LANES:(tile + 1) * LANES] = (num / den).astype(BF16)
        return carry

    lax.fori_loop(0, t_len // chunk, merge, 0)


def _dilated_prompt(q, k, v, bsz, t_len, *, tiles=4):
    width = q.shape[1]
    specs = [pl.BlockSpec((t_len, LANES), lambda b, j, t=t: (b, j * tiles + t)) for t in range(tiles)]
    return pl.pallas_call(
        functools.partial(_dilated_prompt_body, patterns=C_PATTERNS, tiles=tiles),
        grid=(bsz, width // (tiles * LANES)),
        in_specs=specs * 3,
        out_specs=pl.BlockSpec((t_len, tiles * LANES), lambda b, j: (b, j)),
        out_shape=jax.ShapeDtypeStruct((bsz * t_len, width), BF16),
        scratch_shapes=[pltpu.VMEM((t_len, LANES), F32)] * (2 * len(C_PATTERNS) * tiles),
        compiler_params=_cparams(("arbitrary", "arbitrary")),
        name="dilated_prompt",
    )(*([q] * tiles), *([k] * tiles), *([v] * tiles))


def _dilated_sample_body(q_ref, kn_ref, vn_ref, kc_ref, vc_ref, o_ref, *, patterns):
    t_new, cols = q_ref.shape
    w_len = kc_ref.shape[1]
    nrow = 2 * t_new
    lane = _iota((t_new, LANES), 1)

    def mult(delta):
        cnt = jnp.zeros(delta.shape, F32)
        for window, dil in patterns:
            hit = (delta >= 0) & (delta <= window) & ((delta % dil) == 0)
            cnt = cnt + hit.astype(F32)
        return cnt

    t_c = _iota((nrow, w_len), 0) % t_new
    w_c = mult(w_len + t_c - _iota((nrow, w_len), 1))
    t_n = _iota((nrow, LANES), 0) % t_new
    c_n = _iota((nrow, LANES), 1)
    w_n = jnp.where(c_n < t_new, mult(t_n - c_n), 0.0)
    pad = jnp.zeros((LANES - t_new, LANES), F32)
    for tile in range(cols // LANES):
        cs = slice(tile * LANES, (tile + 1) * LANES)
        q = q_ref[:, cs] * QK_SCALE
        qst = jnp.concatenate([jnp.where(lane < HEAD_DIM, q, 0.0), jnp.where(lane >= HEAD_DIM, q, 0.0)], axis=0)
        qst = qst.astype(BF16)
        kn = jnp.concatenate([kn_ref[:, cs], pad], axis=0).astype(BF16)
        vn = jnp.concatenate([vn_ref[:, cs], pad], axis=0).astype(BF16)
        s_c = jnp.dot(qst, kc_ref[cs, :].astype(BF16), preferred_element_type=F32)
        s_n = lax.dot_general(qst, kn, NT, preferred_element_type=F32)
        s_c = jnp.where(w_c > 0, s_c, NEG)
        s_n = jnp.where(w_n > 0, s_n, NEG)
        m = jnp.maximum(jnp.max(s_c, axis=1, keepdims=True), jnp.max(s_n, axis=1, keepdims=True))
        p_c = w_c * jnp.exp(s_c - m)
        p_n = w_n * jnp.exp(s_n - m)
        l = jnp.sum(p_c, axis=1, keepdims=True) + jnp.sum(p_n, axis=1, keepdims=True)
        acc = (lax.dot_general(p_c.astype(BF16), vc_ref[cs, :].astype(BF16), NT, preferred_element_type=F32)
               + jnp.dot(p_n.astype(BF16), vn, preferred_element_type=F32))
        out = acc / l
        o_ref[:, cs] = jnp.where(lane < HEAD_DIM, out[:t_new], out[t_new:])


def _dilated_sample(q, k, v, row_block0, cache_kt, cache_vt, *, tiles=4):
    bsz, width, w_len = cache_kt.shape
    t_new = 8
    cols = tiles * LANES
    new_spec = pl.BlockSpec((t_new, cols), lambda b, j: (row_block0 + b, j))
    cache_spec = pl.BlockSpec((None, cols, w_len), lambda b, j: (b, j, 0))
    return pl.pallas_call(
        functools.partial(_dilated_sample_body, patterns=C_PATTERNS),
        grid=(bsz, width // cols),
        in_specs=[new_spec, new_spec, new_spec, cache_spec, cache_spec],
        out_specs=pl.BlockSpec((t_new, cols), lambda b, j: (b, j)),
        out_shape=jax.ShapeDtypeStruct((bsz * t_new, width), F32),
        compiler_params=_cparams(("arbitrary", "arbitrary")),
        name="dilated_sample",
    )(q, k, v, cache_kt, cache_vt)


def _moe_route_body(x_ref, c_ref, wo_ref, g_ref, r_ref,
                    x3_ref, hn_ref, comb_ref, pos_ref, post_ref, cnt_ref, before_ref):
    tb = x_ref.shape[0]

    @pl.when(pl.program_id(0) == 0)
    def _():
        before_ref[...] = jnp.where(_iota((tb, tb), 1) < _iota((tb, tb), 0), 1.0, 0.0).astype(BF16)

    x3 = x_ref[...] + jnp.dot(c_ref[...], wo_ref[...], preferred_element_type=F32)
    x3_ref[...] = x3
    hf = _rms(x3, g_ref[...])
    hn = hf.astype(BF16)
    hn_ref[...] = hn
    lane = _iota((tb, LANES), 1)
    h_lo = (hf - hn.astype(F32)).astype(BF16)
    r = r_ref[...]
    r_hi = r.astype(BF16)
    r_lo = (r - r_hi.astype(F32)).astype(BF16)
    logits = (jnp.dot(hn, r_hi, preferred_element_type=F32)
              + (jnp.dot(hn, r_lo, preferred_element_type=F32)
                 + jnp.dot(h_lo, r_hi, preferred_element_type=F32)))
    lg = jnp.where(lane < N_EXPERTS, logits, -jnp.inf)
    m1 = jnp.max(lg, axis=1, keepdims=True)
    p1 = lane == jnp.min(jnp.where(lg == m1, lane, LANES), axis=1, keepdims=True)
    lg2 = jnp.where(p1, -jnp.inf, lg)
    m2 = jnp.max(lg2, axis=1, keepdims=True)
    p2 = lane == jnp.min(jnp.where(lg2 == m2, lane, LANES), axis=1, keepdims=True)
    e2 = jnp.exp(m2 - m1)
    den = 1.0 + e2
    comb_ref[...] = jnp.where(p1, 1.0 / den, 0.0) + jnp.where(p2, e2 / den, 0.0)
    routed = p1 | p2
    ind = jnp.where(routed, 1.0, 0.0)
    slot = jnp.where(routed, jnp.dot(before_ref[...], ind.astype(BF16), preferred_element_type=F32), -1.0)
    pos_ref[...] = slot
    eye = jnp.where(_iota((LANES, LANES), 0) == _iota((LANES, LANES), 1), 1.0, 0.0)
    post_ref[...] = lax.dot_general(eye, slot, NT, precision=HIGHEST, preferred_element_type=F32)
    cnt = jnp.sum(ind, axis=0, keepdims=True)
    cnt_ref[...] = jnp.broadcast_to(cnt, cnt_ref.shape).astype(jnp.int32)


def _moe_route(x, c, wo16, g, router_pad, *, tb):
    rows, dm = x.shape
    nblk = rows // tb
    row_spec = lambda width: pl.BlockSpec((tb, width), lambda i: (i, 0))
    return pl.pallas_call(
        _moe_route_body,
        grid=(nblk,),
        in_specs=[
            row_spec(dm), row_spec(dm),
            pl.BlockSpec((dm, dm), lambda i: (0, 0)),
            pl.BlockSpec((1, dm), lambda i: (0, 0)),
            pl.BlockSpec((dm, LANES), lambda i: (0, 0)),
        ],
        out_specs=[
            row_spec(dm), row_spec(dm), row_spec(LANES), row_spec(LANES),
            pl.BlockSpec((None, LANES, tb), lambda i: (i, 0, 0)),
            pl.BlockSpec((None, 8, LANES), lambda i: (i, 0, 0)),
        ],
        out_shape=[
            jax.ShapeDtypeStruct((rows, dm), F32),
            jax.ShapeDtypeStruct((rows, dm), BF16),
            jax.ShapeDtypeStruct((rows, LANES), F32),
            jax.ShapeDtypeStruct((rows, LANES), F32),
            jax.ShapeDtypeStruct((nblk, LANES, tb), F32),
            jax.ShapeDtypeStruct((nblk, 8, LANES), jnp.int32),
        ],
        scratch_shapes=[pltpu.VMEM((tb, tb), BF16)],
        compiler_params=_cparams(("arbitrary",)),
        name="moe_route",
    )(x, c, wo16, g, router_pad)


def _load_expert_weight(src_hbm, e, dst16, stage, sem):
    chunk = stage.shape[1]
    n_chunks = dst16.shape[0] // chunk

    def copy(c, slot):
        return pltpu.make_async_copy(src_hbm.at[e, pl.ds(c * chunk, chunk), :], stage.at[slot], sem.at[slot])

    copy(0, 0).start()
    for c in range(n_chunks):
        slot = c % 2
        if c + 1 < n_chunks:
            copy(c + 1, 1 - slot).start()
        copy(c, slot).wait()
        dst16[c * chunk:(c + 1) * chunk, :] = stage[slot].astype(BF16)


def _moe_expert_body(cnt_ref, acc_ref, hn_ref, comb_ref, pos_ref, post_ref, gf_ref,
                     wg_hbm, wu_hbm, wd_hbm, o_ref, *rest, e, last, tile_rows, tail_rows):
    tail_ref = rest[:1] if last else ()
    wg_ref, wu_ref, wd_ref, stage_in, stage_out, sem_in, sem_out = rest[len(tail_ref):]
    i = pl.program_id(0)
    tb = hn_ref.shape[0]

    @pl.when(i == 0)
    def _():
        _load_expert_weight(wg_hbm, e, wg_ref, stage_in, sem_in)
        _load_expert_weight(wu_hbm, e, wu_ref, stage_in, sem_in)
        _load_expert_weight(wd_hbm, e, wd_ref, stage_out, sem_out)

    n_tok = cnt_ref[i, e]
    lane = _iota((tb, LANES), 1)
    w_col = jnp.sum(jnp.where(lane == e, comb_ref[...], 0.0), axis=1, keepdims=True)
    slot_col = jnp.sum(jnp.where(lane == e, pos_ref[...], 0.0), axis=1, keepdims=True)
    slot_row = post_ref[pl.ds(e, 1), :]
    finish = (lambda v: _rms(v, gf_ref[...])) if last else (lambda v: v)

    def tile_out(rows, base):
        sub_of_row = _iota((rows, tb), 0).astype(F32)
        sub_of_col = _iota((tb, rows), 1).astype(F32)
        gather = jnp.where(slot_row - base == sub_of_row, 1.0, 0.0).astype(BF16)
        xe = jnp.dot(gather, hn_ref[...], preferred_element_type=F32).astype(BF16)
        gate = jnp.dot(xe, wg_ref[...], preferred_element_type=F32)
        up = jnp.dot(xe, wu_ref[...], preferred_element_type=F32)
        act = (_silu(gate) * up).astype(BF16)
        y = jnp.dot(act, wd_ref[...], preferred_element_type=F32).astype(BF16)
        scatter = jnp.where(slot_col - base == sub_of_col, 1.0, 0.0).astype(BF16)
        return w_col * jnp.dot(scatter, y, preferred_element_type=F32)

    @pl.when(n_tok == 0)
    def _():
        o_ref[...] = finish(acc_ref[...])

    lower = 0
    for rows in tile_rows:
        @pl.when((n_tok > lower) & (n_tok <= rows))
        def _(rows=rows):
            o_ref[...] = finish(acc_ref[...] + tile_out(rows, 0.0))
        lower = rows
    big = tile_rows[-1]

    @pl.when(n_tok > big)
    def _():
        o_ref[...] = acc_ref[...]

        def body(s, carry):
            o_ref[...] += tile_out(big, (s * big).astype(F32))
            return carry
        lax.fori_loop(0, (n_tok + big - 1) // big, body, 0)
        o_ref[...] = finish(o_ref[...])

    if last:
        @pl.when(i == pl.num_programs(0) - 1)
        def _():
            tb_rows = o_ref.shape[0]
            tail_ref[0][...] = o_ref[tb_rows - tail_rows:, :]


def _moe_experts(counts, x3, hn, comb, pos, post, g_final, wg, wu, wd, *, tb, tile_rows, tail_rows):
    rows, dm = x3.shape
    n_exp, _, dff = wg.shape
    assert rows % tb == 0 and tail_rows <= tb
    assert dm % WEIGHT_STAGE_ROWS == 0 and dff % WEIGHT_STAGE_ROWS == 0
    hbm = pl.BlockSpec(memory_space=pl.ANY)
    acc = x3
    for e in range(n_exp):
        last = e == n_exp - 1
        row_spec = pl.BlockSpec((tb, dm), lambda i, cnt: (i, 0))
        if last:
            out_specs = [row_spec, pl.BlockSpec((tail_rows, dm), lambda i, cnt: (0, 0))]
            out_shape = [jax.ShapeDtypeStruct((rows - tail_rows, dm), F32),
                         jax.ShapeDtypeStruct((tail_rows, dm), F32)]
        else:
            out_specs, out_shape = row_spec, jax.ShapeDtypeStruct((rows, dm), F32)
        acc = pl.pallas_call(
            functools.partial(_moe_expert_body, e=e, last=last, tile_rows=tile_rows, tail_rows=tail_rows),
            grid_spec=pltpu.PrefetchScalarGridSpec(
                num_scalar_prefetch=1,
                grid=(rows // tb,),
                in_specs=[
                    pl.BlockSpec((tb, dm), lambda i, cnt: (i, 0)),
                    pl.BlockSpec((tb, dm), lambda i, cnt: (i, 0)),
                    pl.BlockSpec((tb, LANES), lambda i, cnt: (i, 0)),
                    pl.BlockSpec((tb, LANES), lambda i, cnt: (i, 0)),
                    pl.BlockSpec((None, LANES, tb), lambda i, cnt: (i, 0, 0)),
                    pl.BlockSpec((1, dm), lambda i, cnt: (0, 0)),
                    hbm, hbm, hbm,
                ],
                out_specs=out_specs,
                scratch_shapes=[
                    pltpu.VMEM((dm, dff), BF16),
                    pltpu.VMEM((dm, dff), BF16),
                    pltpu.VMEM((dff, dm), BF16),
                    pltpu.VMEM((2, WEIGHT_STAGE_ROWS, dff), F32),
                    pltpu.VMEM((2, WEIGHT_STAGE_ROWS, dm), F32),
                    pltpu.SemaphoreType.DMA((2,)),
                    pltpu.SemaphoreType.DMA((2,)),
                ],
            ),
            out_shape=out_shape,
            compiler_params=_cparams(("arbitrary",)),
            name=f"moe_expert{e}",
        )(counts, acc, hn, comb, pos, post, g_final, wg, wu, wd)
    return acc


def _moe(x, c, wo16, g, router_pad, g_final, wg, wu, wd, tail_rows, *, tb=416,
         tile_rows=(128, 160, 192, 256)):
    x3, hn, comb, pos, post, cnt = _moe_route(x, c, wo16, g, router_pad, tb=tb)
    counts = cnt[:, 0, :N_EXPERTS]
    return _moe_experts(counts, x3, hn, comb, pos, post, g_final, wg, wu, wd, tb=tb,
                        tile_rows=tile_rows, tail_rows=tail_rows)


def kernel(x_prompt, x_sample, cache_b_k, cache_b_v, cache_c_k, cache_c_v, page_table, norm_mix, norm_ffn,
           norm_final, w_in_even, w_out_even, a_ln_g, a_ln_b, a_ws, a_bs, ffn_gate, ffn_up, ffn_down,
           w_in_odd, w_out_odd, router, moe_gate, moe_up, moe_down):
    bsz, t_len, dm = x_prompt.shape
    dbsz, t_new, _ = x_sample.shape
    assert norm_mix.shape[0] == 2 and w_in_even.shape[0] == 1 and w_in_odd.shape[0] == 1
    assert cache_b_k.shape[0] == 1 and cache_c_k.shape[0] == 1
    assert t_new == 8 and A_CHUNK % t_new == 0
    n_p = bsz * t_len
    n_s = dbsz * t_new
    aw = a_ln_g.shape[-1]
    n_groups = aw // HEAD_DIM
    seq_per_chunk = A_CHUNK // t_new


    ws = a_ws[0]
    ws_sample = jnp.tile(ws[:, :t_new, :t_new], (1, seq_per_chunk, seq_per_chunk))
    ws2 = jnp.stack([ws, ws_sample])
    bias_p = jnp.repeat(a_bs[0].T, HEAD_DIM, axis=1)
    bias_s = jnp.repeat(jnp.tile(a_bs[0][:, :t_new].T, (seq_per_chunk, 1)), HEAD_DIM, axis=1)
    bias2 = jnp.stack([bias_p, bias_s])
    row = lambda vec: vec.reshape(1, -1)
    router_pad = jnp.pad(router[0], ((0, 0), (0, LANES - router.shape[-1])))

    w_even = w_in_even[0].astype(BF16)
    a_out, v_a, q_b, k_b, v_b, kt_b, vt_b, v_a_last, x = _even_in(
        x_prompt.reshape(n_p, dm), x_sample.reshape(n_s, dm), bsz, t_len, row(norm_mix[0]), w_even, row(a_ln_g[0]), row(a_ln_b[0]), ws2, bias2)
    b_prompt = _moba_prompt(q_b, k_b, v_b, bsz, t_len)
    page = cache_b_k.shape[2]
    bw = q_b.shape[1]
    feature_major = lambda c: jnp.transpose(c, (0, 1, 3, 4, 2))
    pool_kt = feature_major(cache_b_k).reshape(-1, B_HEADS, HEAD_DIM, page)
    pool_vt = feature_major(cache_b_v).reshape(-1, B_HEADS, HEAD_DIM, page)
    sel = _moba_select(q_b, n_p // t_new, pool_kt.reshape(-1, bw, page), page_table)
    sel = sel.reshape(dbsz, B_HEADS * t_new, LANES)[:, :, :B_TOPK].reshape(dbsz, -1)
    new_t = lambda a: jnp.transpose(a[n_p:].reshape(dbsz, t_new, -1), (0, 2, 1))
    ppb = B_BLOCK // page
    logical = (jnp.maximum(sel, 0)[:, :, None] * ppb + jnp.arange(ppb, dtype=jnp.int32)).reshape(dbsz, -1)
    hit = logical[:, :, None] == jnp.arange(page_table.shape[1], dtype=jnp.int32)
    sel_pages = jnp.sum(jnp.where(hit, page_table[:, None, :], 0), axis=-1)
    b_sample_t = _moba_gather(new_t(q_b), new_t(k_b), new_t(v_b), pool_kt, pool_vt, sel_pages, sel)
    b_sample = jnp.transpose(b_sample_t, (0, 2, 1)).reshape(n_s, bw)
    b_out = jnp.concatenate([b_prompt, b_sample.astype(BF16)], axis=0)
    x2 = _ffn_even(x, a_out, b_out, w_out_even[0].astype(BF16), row(norm_ffn[0]),
                   ffn_gate[0].astype(BF16), ffn_up[0].astype(BF16), ffn_down[0].astype(BF16))

    w_odd = w_in_odd[0].astype(BF16)
    q_c, k_c, v_c, kt_c, vt_c = _odd_in(x2, bsz, t_len, row(norm_mix[1]), w_odd)
    c_prompt = _dilated_prompt(q_c, k_c, v_c, bsz, t_len)
    cw = q_c.shape[1]
    w_len = cache_c_k.shape[2]
    c_sample = _dilated_sample(q_c, k_c, v_c, n_p // t_new,
                               feature_major(cache_c_k).reshape(-1, cw, w_len),
                               feature_major(cache_c_v).reshape(-1, cw, w_len))
    c_out = jnp.concatenate([c_prompt, c_sample.astype(BF16)], axis=0)
    y_p, y_s = _moe(x2, c_out, w_out_odd[0].astype(BF16), row(norm_ffn[1]), router_pad, row(norm_final),
                    moe_gate[0], moe_up[0], moe_down[0], n_s)

    bh = (B_HEADS, HEAD_DIM)
    ch = (cw // HEAD_DIM, HEAD_DIM)
    assert t_len % 256 == 0
    c_keep_p = min(C_PATTERNS[-1][0], t_len)
    leaf = lambda a_t, hd: jnp.transpose(a_t.reshape(1, bsz, *hd, a_t.shape[-1]), (0, 1, 4, 2, 3))
    return (
        y_p.reshape(bsz, t_len, dm),
        y_s.reshape(dbsz, t_new, dm),
        v_a_last.reshape(1, bsz, A_CHUNK, aw),
        v_a[n_p:].reshape(1, dbsz, t_new, aw),
        leaf(kt_b, bh),
        leaf(vt_b, bh),
        k_b[n_p:].reshape(1, dbsz, t_new, *bh),
        v_b[n_p:].reshape(1, dbsz, t_new, *bh),
        leaf(kt_c[:, :, t_len - c_keep_p:], ch),
        leaf(vt_c[:, :, t_len - c_keep_p:], ch),
        k_c[n_p:].reshape(1, dbsz, t_new, *ch),
        v_c[n_p:].reshape(1, dbsz, t_new, *ch),
    )
```
